```python
import jax, jax.numpy as jnp
from jax import lax
import numpy as np

D_MODEL = 2048
BATCH = 4
SEQ = 2048
DEPTH = 1
DEC_BATCH = 128
DEC_SEQ = 1
PAST_LEN = 16384
PAGE_SIZE = 128

N_META = 16
N_HEADS = 8
DK = D_MODEL // 16
DV = D_MODEL // N_HEADS
D_CONV = D_MODEL
CONV_W = 3
D_FF = 11 * D_MODEL // 4
CHUNK = 128
EPS = 1e-6
D_IN_PROJ = 4 * D_CONV + 2 * N_HEADS * DK + 3 * N_HEADS * DV + 2 * N_HEADS

kernel_name = 'hybrid_shortconv_mlstm_decoder_step'


def rmsnorm(x, g):
    xf = x.astype(jnp.float32)
    y = xf * lax.rsqrt(jnp.mean(xf * xf, axis=-1, keepdims=True) + EPS)
    return (y * g.astype(jnp.float32)).astype(x.dtype)


def causal_dwconv(u, buf, w):
    t = u.shape[1]
    full = jnp.concatenate([buf.astype(u.dtype), u], axis=1)
    y = full[:, 0:t] * w[0]
    for j in range(1, CONV_W):
        y = y + full[:, j:j + t] * w[j]
    return y, full[:, t:]


def mlstm_chunk(state, inp):
    c_prev, n_prev, m_prev = state
    q, k, v, ig, lf = inp
    l = q.shape[2]
    b = jnp.cumsum(lf, axis=-1)
    causal = jnp.tril(jnp.ones((l, l), dtype=bool))
    dmat = jnp.where(causal, b[..., :, None] - b[..., None, :] + ig[..., None, :], -jnp.inf)
    m_t = jnp.maximum(b + m_prev[..., None], jnp.max(dmat, axis=-1))
    w_ts = jnp.exp(dmat - m_t[..., None])
    s = jnp.einsum('bhtd,bhsd->bhts', q, k) * w_ts
    decay = jnp.exp(b + m_prev[..., None] - m_t)
    num = decay[..., None] * jnp.einsum('bhtd,bhdv->bhtv', q, c_prev) + jnp.einsum('bhts,bhsv->bhtv', s, v)
    den = decay * jnp.einsum('bhtd,bhd->bht', q, n_prev) + jnp.sum(s, axis=-1)
    h = num / jnp.maximum(jnp.abs(den), jnp.exp(-m_t))[..., None]
    m_last = m_t[..., -1]
    g = jnp.exp(b[..., -1:] - b + ig - m_last[..., None])
    carry = jnp.exp(b[..., -1] + m_prev - m_last)
    c_new = carry[..., None, None] * c_prev + jnp.einsum('bhs,bhsd,bhsv->bhdv', g, k, v)
    n_new = carry[..., None] * n_prev + jnp.einsum('bhs,bhsd->bhd', g, k)
    return (c_new, n_new, m_last), h


def mlstm_sequence(q, k, v, ig, lf, state, first):
    state, h_first = mlstm_chunk(state, (q[:, :, :first], k[:, :, :first], v[:, :, :first],
                                         ig[:, :, :first], lf[:, :, :first]))
    rest = q.shape[2] - first
    if rest == 0:
        return h_first, state
    nc = rest // CHUNK

    def to_chunks(a):
        a = a[:, :, first:]
        a = a.reshape(a.shape[0], a.shape[1], nc, CHUNK, *a.shape[3:])
        return jnp.moveaxis(a, 2, 0)

    state, h_rest = lax.scan(mlstm_chunk, state, tuple(to_chunks(a) for a in (q, k, v, ig, lf)))
    h_rest = jnp.moveaxis(h_rest, 0, 2)
    h_rest = h_rest.reshape(h_rest.shape[0], h_rest.shape[1], rest, DV)
    return jnp.concatenate([h_first, h_rest], axis=2), state


def block(x, c0, n0, m0, sc0, fc0, first, norm1_g, w_in, b_if, w_shortconv, w_out,
          norm2_g, w_up, w_ffconv, w_down):
    bsz, t, _ = x.shape
    h = rmsnorm(x, norm1_g)
    z = h @ w_in
    sizes = (D_CONV, D_CONV, D_CONV, N_HEADS * DK, N_HEADS * DK, N_HEADS * DV,
             N_HEADS * DV, 2 * N_HEADS, D_CONV, N_HEADS * DV)
    u, cg, bg, q, k, v, o, gif, g_a, g_b = jnp.split(z, np.cumsum(sizes)[:-1].tolist(), axis=-1)
    conv_out, sc = causal_dwconv(cg * u, sc0, w_shortconv)
    a_out = bg * conv_out
    def to_heads(a, d):
        return a.reshape(bsz, t, N_HEADS, d).transpose(0, 2, 1, 3).astype(jnp.float32)
    qh = to_heads(q, DK)
    kh = to_heads(k, DK) * (DK ** -0.5)
    vh = to_heads(v, DV)
    gif = (gif + b_if).astype(jnp.float32).transpose(0, 2, 1)
    ig = gif[:, :N_HEADS]
    lf = jax.nn.log_sigmoid(gif[:, N_HEADS:])
    state0 = (c0.astype(jnp.float32), n0.astype(jnp.float32), m0.astype(jnp.float32))
    hb, (c, n, m) = mlstm_sequence(qh, kh, vh, ig, lf, state0, first)
    hb = hb.transpose(0, 2, 1, 3).reshape(bsz, t, N_HEADS * DV).astype(x.dtype)
    b_out = jax.nn.sigmoid(o) * hb
    mix = jax.nn.sigmoid(g_a) * a_out + jax.nn.sigmoid(g_b) * b_out
    x = x + mix @ w_out
    h2 = rmsnorm(x, norm2_g)
    up, fc = causal_dwconv(h2 @ w_up, fc0, w_ffconv)
    x = x + (jax.nn.silu(up[..., D_FF:]) * up[..., :D_FF]) @ w_down
    return x, c, n, m, sc, fc


def setup_inputs(seed: int = 0) -> dict:
    key = jax.random.key(seed)
    ks = jax.random.split(key, 20)

    def nrm(k, shape, scale):
        return jax.random.normal(k, shape, jnp.float32) * scale

    return {
        'x_prompt': nrm(ks[0], (BATCH, SEQ, D_MODEL), 1.0),
        'x_sample': nrm(ks[1], (DEC_BATCH, DEC_SEQ, D_MODEL), 1.0),
        'state_mlstm_C': nrm(ks[2], (DEPTH, DEC_BATCH, N_HEADS, DK, DV), DK ** -0.5),
        'state_mlstm_n': nrm(ks[3], (DEPTH, DEC_BATCH, N_HEADS, DK), DK ** -0.5),
        'state_mlstm_m': nrm(ks[4], (DEPTH, DEC_BATCH, N_HEADS), 1.0),
        'state_shortconv': nrm(ks[5], (DEPTH, DEC_BATCH, CONV_W - 1, D_CONV), 0.5),
        'state_ffnconv': nrm(ks[6], (DEPTH, DEC_BATCH, CONV_W - 1, 2 * D_FF), 0.5),
        'meta_tokens': nrm(ks[7], (N_META, D_MODEL), 1.0),
        'norm1_g': 1.0 + nrm(ks[8], (DEPTH, D_MODEL), 0.02),
        'w_in': nrm(ks[9], (DEPTH, D_MODEL, D_IN_PROJ), D_MODEL ** -0.5),
        'b_if': jnp.concatenate([nrm(ks[10], (DEPTH, N_HEADS), 0.1),
                                 3.0 + nrm(ks[11], (DEPTH, N_HEADS), 0.5)], axis=-1),
        'w_shortconv': nrm(ks[12], (DEPTH, CONV_W, D_CONV), CONV_W ** -0.5),
        'w_out': nrm(ks[13], (DEPTH, D_MODEL, D_MODEL), D_MODEL ** -0.5),
        'norm2_g': 1.0 + nrm(ks[14], (DEPTH, D_MODEL), 0.02),
        'w_up': nrm(ks[15], (DEPTH, D_MODEL, 2 * D_FF), D_MODEL ** -0.5),
        'w_ffconv': nrm(ks[16], (DEPTH, CONV_W, 2 * D_FF), CONV_W ** -0.5),
        'w_down': nrm(ks[17], (DEPTH, D_FF, D_MODEL), D_FF ** -0.5),
        'norm_f_g': 1.0 + nrm(ks[18], (D_MODEL,), 0.02),
    }


def reference(x_prompt, x_sample, state_mlstm_C, state_mlstm_n, state_mlstm_m, state_shortconv,
              state_ffnconv, meta_tokens, norm1_g, w_in, b_if, w_shortconv, w_out, norm2_g,
              w_up, w_ffconv, w_down, norm_f_g):
    bsz = x_prompt.shape[0]
    meta = jnp.broadcast_to(meta_tokens.astype(x_prompt.dtype)[None], (bsz, N_META, D_MODEL))
    xp = jnp.concatenate([meta, x_prompt], axis=1)
    xs = x_sample
    pc, pn, pm, psc, pfc = [], [], [], [], []
    sc_l, sn_l, sm_l, ssc_l, sfc_l = [], [], [], [], []
    for l in range(DEPTH):
        xp, c, n, m, sc, fc = block(
            xp,
            jnp.zeros((bsz, N_HEADS, DK, DV), jnp.float32),
            jnp.zeros((bsz, N_HEADS, DK), jnp.float32),
            jnp.zeros((bsz, N_HEADS), jnp.float32),
            jnp.zeros((bsz, CONV_W - 1, D_CONV), xp.dtype),
            jnp.zeros((bsz, CONV_W - 1, 2 * D_FF), xp.dtype),
            N_META, norm1_g[l], w_in[l], b_if[l], w_shortconv[l], w_out[l],
            norm2_g[l], w_up[l], w_ffconv[l], w_down[l])
        pc.append(c); pn.append(n); pm.append(m); psc.append(sc); pfc.append(fc)
        xs, c, n, m, sc, fc = block(
            xs, state_mlstm_C[l], state_mlstm_n[l], state_mlstm_m[l], state_shortconv[l],
            state_ffnconv[l], xs.shape[1], norm1_g[l], w_in[l], b_if[l], w_shortconv[l],
            w_out[l], norm2_g[l], w_up[l], w_ffconv[l], w_down[l])
        sc_l.append(c); sn_l.append(n); sm_l.append(m); ssc_l.append(sc); sfc_l.append(fc)
    y_prompt = rmsnorm(xp, norm_f_g)[:, N_META:]
    y_sample = rmsnorm(xs, norm_f_g)
    return (y_prompt, y_sample,
            jnp.stack(pc), jnp.stack(pn), jnp.stack(pm), jnp.stack(psc), jnp.stack(pfc),
            jnp.stack(sc_l), jnp.stack(sn_l), jnp.stack(sm_l), jnp.stack(ssc_l), jnp.stack(sfc_l))
```

```python
import functools

import jax
import jax.numpy as jnp
from jax import lax
from jax.experimental import pallas as pl
from jax.experimental.pallas import tpu as pltpu

D_MODEL = 2048
N_META = 16
N_HEADS = 8
DK = 128
DV = 256
D_CONV = D_MODEL
CONV_W = 3
D_FF = 5632
CHUNK = 128
EPS = 1e-6
K_SCALE = DK ** -0.5
NEG = -1e30
HEAD_COLS = 7 * DV + 2 * DK
V7X_VMEM_LIMIT = 56 * 1024 * 1024
HALO = 8
F32 = jnp.float32
BF16 = jnp.bfloat16

_U, _CG, _BG, _GA, _V, _O, _GB, _Q, _K = (0, 256, 512, 768, 1024, 1280, 1536, 1792, 1920)


def _params(n_grid):
    return pltpu.CompilerParams(dimension_semantics=("arbitrary",) * n_grid,
                                vmem_limit_bytes=V7X_VMEM_LIMIT)


def _log_sigmoid(x):
    return jnp.minimum(x, 0.0) - jnp.log1p(jnp.exp(-jnp.abs(x)))


def _rms(x, g):
    return x * lax.rsqrt(jnp.mean(x * x, axis=-1, keepdims=True) + EPS) * g


def _norm_gates_seq_kernel(x_ref, g_ref, wg_ref, bif_ref, hn_ref, gates_ref, *, tm, n_valid):
    hn = _rms(x_ref[...], g_ref[...]).astype(BF16)
    hn_ref[...] = hn
    gif = jnp.dot(hn, wg_ref[...], preferred_element_type=F32) + bif_ref[...]
    row = lax.broadcasted_iota(jnp.int32, (CHUNK, CHUNK), 0)
    col = lax.broadcasted_iota(jnp.int32, (CHUNK, CHUNK), 1)
    upper = (row <= col).astype(F32)
    for c in range(tm // CHUNK):
        gt = gif[c * CHUNK:(c + 1) * CHUNK, :].T
        ig = gt[0:N_HEADS, :]
        lf = _log_sigmoid(gt[N_HEADS:2 * N_HEADS, :])
        if n_valid < CHUNK:
            tok = lax.broadcasted_iota(jnp.int32, (N_HEADS, CHUNK), 1)
            ig = jnp.where(tok < n_valid, ig, NEG)
            lf = jnp.where(tok < n_valid, lf, 0.0)
        b = jnp.dot(lf, upper, preferred_element_type=F32, precision=lax.Precision.HIGHEST)
        a = ig - b
        for j in range(N_HEADS):
            gates_ref[j, 0:1, c * CHUNK:(c + 1) * CHUNK] = a[j:j + 1, :]
            gates_ref[j, 1:2, c * CHUNK:(c + 1) * CHUNK] = b[j:j + 1, :]


def _norm_gates_seq(x, g, wg, bif, *, tm, n_valid):
    t = x.shape[0]
    return pl.pallas_call(
        functools.partial(_norm_gates_seq_kernel, tm=tm, n_valid=n_valid),
        grid=(t // tm,),
        in_specs=[pl.BlockSpec((tm, D_MODEL), lambda r: (r, 0)),
                  pl.BlockSpec((1, D_MODEL), lambda r: (0, 0)),
                  pl.BlockSpec((D_MODEL, 128), lambda r: (0, 0)),
                  pl.BlockSpec((1, 128), lambda r: (0, 0))],
        out_specs=[pl.BlockSpec((tm, D_MODEL), lambda r: (r, 0)),
                   pl.BlockSpec((N_HEADS, 2, tm), lambda r: (0, 0, r))],
        out_shape=[jax.ShapeDtypeStruct((t, D_MODEL), BF16),
                   jax.ShapeDtypeStruct((N_HEADS, 2, t), F32)],
        compiler_params=_params(1),
        name="norm_gates_seq",
    )(x, g, wg, bif)


def _norm_gates_step_kernel(x_ref, g_ref, wg_ref, bif_ref, hn_ref, gif_ref):
    hn = _rms(x_ref[...], g_ref[...]).astype(BF16)
    hn_ref[...] = hn
    gif_ref[...] = jnp.dot(hn, wg_ref[...], preferred_element_type=F32) + bif_ref[...]


def _norm_gates_step(x, g, wg, bif):
    t = x.shape[0]
    return pl.pallas_call(
        _norm_gates_step_kernel,
        grid=(1,),
        in_specs=[pl.BlockSpec((t, D_MODEL), lambda r: (0, 0)),
                  pl.BlockSpec((1, D_MODEL), lambda r: (0, 0)),
                  pl.BlockSpec((D_MODEL, 128), lambda r: (0, 0)),
                  pl.BlockSpec((1, 128), lambda r: (0, 0))],
        out_specs=[pl.BlockSpec((t, D_MODEL), lambda r: (0, 0)),
                   pl.BlockSpec((t, 128), lambda r: (0, 0))],
        out_shape=[jax.ShapeDtypeStruct((t, D_MODEL), BF16),
                   jax.ShapeDtypeStruct((t, 128), F32)],
        compiler_params=_params(1),
        name="norm_gates_step",
    )(x, g, wg, bif)


def _mixers_seq_kernel(hn_ref, w_ref, gates_ref, wsc_ref, c0_ref, n0_ref, m0_ref, sc0_ref,
                       mix_ref, cout_ref, nout_ref, mout_ref, scout_ref,
                       z_ref, cu_ref, c_ref, n_ref, m_ref, *, tm, tiles_per_batch, n_valid):
    rb = pl.program_id(1) % tiles_per_batch

    @pl.when(rb == 0)
    def _():
        c_ref[...] = c0_ref[...]
        n_ref[...] = n0_ref[...]
        m_ref[...] = m0_ref[...]
        cu_ref[0:HALO, :] = sc0_ref[...]

    z_ref[...] = jnp.dot(hn_ref[...], w_ref[...], preferred_element_type=F32)

    row = lax.broadcasted_iota(jnp.int32, (CHUNK, CHUNK), 0)
    col = lax.broadcasted_iota(jnp.int32, (CHUNK, CHUNK), 1)
    causal = col <= row
    eye = col == row
    w0 = wsc_ref[0:1, :]
    w1 = wsc_ref[1:2, :]
    w2 = wsc_ref[2:3, :]

    for c in range(tm // CHUNK):
        lo = c * CHUNK
        rows = slice(lo, lo + CHUNK)
        cu = z_ref[rows, _CG:_CG + DV] * z_ref[rows, _U:_U + DV]
        cu_ref[HALO + lo:HALO + lo + CHUNK, :] = cu
        conv = (cu * w2 + cu_ref[HALO - 1 + lo:HALO - 1 + lo + CHUNK, :] * w1
                + cu_ref[HALO - 2 + lo:HALO - 2 + lo + CHUNK, :] * w0)
        a_out = z_ref[rows, _BG:_BG + DV] * conv

        q = z_ref[rows, _Q:_Q + DK]
        k = z_ref[rows, _K:_K + DK] * K_SCALE
        v = z_ref[rows, _V:_V + DV]
        a_row = gates_ref[0:1, rows]
        b_row = gates_ref[1:2, rows]
        a_col = jnp.sum(jnp.where(eye, a_row, 0.0), axis=-1, keepdims=True)
        b_col = jnp.sum(jnp.where(eye, b_row, 0.0), axis=-1, keepdims=True)
        m_prev = m_ref[:, 0:1]
        dmat = jnp.where(causal, b_col + a_row, NEG)
        bm = b_col + m_prev
        m_t = jnp.maximum(bm, jnp.max(dmat, axis=-1, keepdims=True))
        w_ts = jnp.exp(dmat - m_t)
        qb = q.astype(BF16)
        kb = k.astype(BF16)
        s = lax.dot_general(qb, kb, (((1,), (1,)), ((), ())), preferred_element_type=F32) * w_ts
        decay = jnp.exp(bm - m_t)
        c_prev = c_ref[...]
        n_prev = n_ref[...]
        num = (decay * jnp.dot(qb, c_prev.astype(BF16), preferred_element_type=F32)
               + jnp.dot(s.astype(BF16), v.astype(BF16), preferred_element_type=F32))
        den = (decay * jnp.sum(q * n_prev, axis=-1, keepdims=True)
               + jnp.sum(s, axis=-1, keepdims=True))
        hb = num / jnp.maximum(jnp.abs(den), jnp.exp(-m_t))
        m_last = m_t[CHUNK - 1:CHUNK, :]
        b_last = b_col[CHUNK - 1:CHUNK, :]
        g_col = jnp.exp(b_last + a_col - m_last)
        carry = jnp.exp(b_last + m_prev - m_last)
        c_ref[...] = carry * c_prev + jnp.dot(k.T.astype(BF16), (g_col * v).astype(BF16),
                                              preferred_element_type=F32)
        n_ref[...] = carry * n_prev + jnp.sum(g_col * k, axis=0, keepdims=True)
        m_ref[...] = jnp.broadcast_to(m_last, (1, DK))
        b_out = jax.nn.sigmoid(z_ref[rows, _O:_O + DV]) * hb

        mix = (jax.nn.sigmoid(z_ref[rows, _GA:_GA + DV]) * a_out
               + jax.nn.sigmoid(z_ref[rows, _GB:_GB + DV]) * b_out)
        mix_ref[rows, :] = mix.astype(BF16)

    @pl.when(rb == tiles_per_batch - 1)
    def _():
        cout_ref[...] = c_ref[...]
        nout_ref[...] = n_ref[...]
        mout_ref[...] = m_ref[...]
        scout_ref[...] = cu_ref[HALO + n_valid - 2:HALO + n_valid, :]

    cu_ref[0:HALO, :] = cu_ref[tm:tm + HALO, :]


def _mixers_seq(hn, wp, gates, wsc, c0, n0, m0, sc0, *, n_batch, tm, n_valid):
    t = hn.shape[0]
    tpb = t // n_batch // tm
    return pl.pallas_call(
        functools.partial(_mixers_seq_kernel, tm=tm, tiles_per_batch=tpb, n_valid=n_valid),
        grid=(N_HEADS, t // tm),
        in_specs=[pl.BlockSpec((tm, D_MODEL), lambda h, r: (r, 0)),
                  pl.BlockSpec((D_MODEL, HEAD_COLS), lambda h, r: (0, h)),
                  pl.BlockSpec((None, 2, tm), lambda h, r: (h, 0, r)),
                  pl.BlockSpec((CONV_W, DV), lambda h, r: (0, h)),
                  pl.BlockSpec((None, DK, DV), lambda h, r: (h, 0, 0)),
                  pl.BlockSpec((None, 1, DK), lambda h, r: (h, 0, 0)),
                  pl.BlockSpec((None, 1, DK), lambda h, r: (h, 0, 0)),
                  pl.BlockSpec((HALO, DV), lambda h, r: (0, h))],
        out_specs=[pl.BlockSpec((tm, DV), lambda h, r: (r, h)),
                   pl.BlockSpec((None, None, DK, DV), lambda h, r: (r // tpb, h, 0, 0)),
                   pl.BlockSpec((None, None, 1, DK), lambda h, r: (r // tpb, h, 0, 0)),
                   pl.BlockSpec((None, None, 1, DK), lambda h, r: (r // tpb, h, 0, 0)),
                   pl.BlockSpec((None, 2, DV), lambda h, r: (r // tpb, 0, h))],
        out_shape=[jax.ShapeDtypeStruct((t, D_MODEL), BF16),
                   jax.ShapeDtypeStruct((n_batch, N_HEADS, DK, DV), F32),
                   jax.ShapeDtypeStruct((n_batch, N_HEADS, 1, DK), F32),
                   jax.ShapeDtypeStruct((n_batch, N_HEADS, 1, DK), F32),
                   jax.ShapeDtypeStruct((n_batch, 2, D_CONV), F32)],
        scratch_shapes=[pltpu.VMEM((tm, HEAD_COLS), F32),
                        pltpu.VMEM((tm + HALO, DV), F32),
                        pltpu.VMEM((DK, DV), F32),
                        pltpu.VMEM((1, DK), F32),
                        pltpu.VMEM((1, DK), F32)],
        compiler_params=_params(2),
        name="mixers_seq",
    )(hn, wp, gates, wsc, c0, n0, m0, sc0)


def _mixers_step_kernel(hn_ref, w_ref, gif_ref, mprev_ref, wsc_ref, s0_ref, s1_ref, c_ref, n_ref,
                        mix_ref, cout_ref, nout_ref, mout_ref, cuout_ref,
                        z_ref, aout_ref, qt_ref, kt_ref, hb_ref, *, bb, n_blocks):
    h = pl.program_id(0)
    j = pl.program_id(1)
    nb = bb * n_blocks

    @pl.when(j == 0)
    def _():
        z = jnp.dot(hn_ref[...], w_ref[...], preferred_element_type=F32)
        z_ref[...] = z
        cu = z[:, _CG:_CG + DV] * z[:, _U:_U + DV]
        conv = s0_ref[...] * wsc_ref[0:1, :] + s1_ref[...] * wsc_ref[1:2, :] + cu * wsc_ref[2:3, :]
        aout_ref[...] = z[:, _BG:_BG + DV] * conv
        cuout_ref[...] = cu
        qt = z[:, _Q:_Q + DK].T
        kt = (z[:, _K:_K + DK] * K_SCALE).T
        for jj in range(n_blocks):
            qt_ref[jj] = qt[:, jj * bb:(jj + 1) * bb]
            kt_ref[jj] = kt[:, jj * bb:(jj + 1) * bb]

    rows = pl.ds(pl.multiple_of(j * bb, bb), bb)
    lane = lax.broadcasted_iota(jnp.int32, (bb, 128), 1)
    gif = gif_ref[rows, :]
    ig = jnp.sum(jnp.where(lane == h, gif, 0.0), axis=-1, keepdims=True)
    lf = _log_sigmoid(jnp.sum(jnp.where(lane == h + N_HEADS, gif, 0.0), axis=-1, keepdims=True))
    m_prev = jnp.sum(jnp.where(lane == h, mprev_ref[rows, :], 0.0), axis=-1, keepdims=True)
    q = z_ref[rows, _Q:_Q + DK]
    k = z_ref[rows, _K:_K + DK] * K_SCALE
    v = z_ref[rows, _V:_V + DV]
    n_prev = n_ref[...]
    m_t = jnp.maximum(lf + m_prev, ig)
    w_in = jnp.exp(ig - m_t)
    decay = jnp.exp(lf + m_prev - m_t)
    s = jnp.sum(q * k, axis=-1, keepdims=True) * w_in
    den = decay * jnp.sum(q * n_prev, axis=-1, keepdims=True) + s
    inv = 1.0 / jnp.maximum(jnp.abs(den), jnp.exp(-m_t))
    nout_ref[...] = decay * n_prev + w_in * k
    mout_ref[...] = jnp.broadcast_to(m_t, (bb, DK))
    qt = qt_ref[j]
    kt = kt_ref[j]
    for i in range(bb):
        c_prev = c_ref[i]
        d_i = decay[i:i + 1, :]
        v_i = v[i:i + 1, :]
        qc = jnp.sum(qt[:, i:i + 1] * c_prev, axis=0, keepdims=True)
        hb_ref[i:i + 1, :] = (d_i * qc + s[i:i + 1, :] * v_i) * inv[i:i + 1, :]
        cout_ref[i] = d_i * c_prev + (w_in[i:i + 1, :] * kt[:, i:i + 1]) * v_i
    b_out = jax.nn.sigmoid(z_ref[rows, _O:_O + DV]) * hb_ref[...]
    mix = (jax.nn.sigmoid(z_ref[rows, _GA:_GA + DV]) * aout_ref[rows, :]
           + jax.nn.sigmoid(z_ref[rows, _GB:_GB + DV]) * b_out)
    mix_ref[...] = mix.astype(BF16)


def _mixers_step(hn, wp, gif, mprev, wsc, s0, s1, c, n, *, bb):
    nb = hn.shape[0]
    n_blocks = nb // bb
    return pl.pallas_call(
        functools.partial(_mixers_step_kernel, bb=bb, n_blocks=n_blocks),
        grid=(N_HEADS, n_blocks),
        in_specs=[pl.BlockSpec((nb, D_MODEL), lambda h, j: (0, 0)),
                  pl.BlockSpec((D_MODEL, HEAD_COLS), lambda h, j: (0, h)),
                  pl.BlockSpec((nb, 128), lambda h, j: (0, 0)),
                  pl.BlockSpec((nb, 128), lambda h, j: (0, 0)),
                  pl.BlockSpec((CONV_W, DV), lambda h, j: (0, h)),
                  pl.BlockSpec((nb, DV), lambda h, j: (0, h)),
                  pl.BlockSpec((nb, DV), lambda h, j: (0, h)),
                  pl.BlockSpec((bb, None, DK, DV), lambda h, j: (j, h, 0, 0)),
                  pl.BlockSpec((None, bb, DK), lambda h, j: (h, j, 0))],
        out_specs=[pl.BlockSpec((bb, DV), lambda h, j: (j, h)),
                   pl.BlockSpec((bb, None, DK, DV), lambda h, j: (j, h, 0, 0)),
                   pl.BlockSpec((None, bb, DK), lambda h, j: (h, j, 0)),
                   pl.BlockSpec((None, bb, DK), lambda h, j: (h, j, 0)),
                   pl.BlockSpec((nb, DV), lambda h, j: (0, h))],
        out_shape=[jax.ShapeDtypeStruct((nb, D_MODEL), BF16),
                   jax.ShapeDtypeStruct((nb, N_HEADS, DK, DV), F32),
                   jax.ShapeDtypeStruct((N_HEADS, nb, DK), F32),
                   jax.ShapeDtypeStruct((N_HEADS, nb, DK), F32),
                   jax.ShapeDtypeStruct((nb, D_CONV), F32)],
        scratch_shapes=[pltpu.VMEM((nb, HEAD_COLS), F32),
                        pltpu.VMEM((nb, DV), F32),
                        pltpu.VMEM((n_blocks, DK, bb), F32),
                        pltpu.VMEM((n_blocks, DK, bb), F32),
                        pltpu.VMEM((bb, DV), F32)],
        compiler_params=_params(2),
        name="mixers_step",
    )(hn, wp, gif, mprev, wsc, s0, s1, c, n)


def _outproj_norm_kernel(x_ref, mix_ref, wo_ref, g_ref, x1_ref, h2_ref):
    x1 = x_ref[...] + jnp.dot(mix_ref[...], wo_ref[...], preferred_element_type=F32)
    x1_ref[...] = x1
    h2_ref[...] = _rms(x1, g_ref[...]).astype(BF16)


def _outproj_norm(x, mix, wo, g, *, tm):
    t = x.shape[0]
    return pl.pallas_call(
        _outproj_norm_kernel,
        grid=(t // tm,),
        in_specs=[pl.BlockSpec((tm, D_MODEL), lambda r: (r, 0)),
                  pl.BlockSpec((tm, D_MODEL), lambda r: (r, 0)),
                  pl.BlockSpec((D_MODEL, D_MODEL), lambda r: (0, 0)),
                  pl.BlockSpec((1, D_MODEL), lambda r: (0, 0))],
        out_specs=[pl.BlockSpec((tm, D_MODEL), lambda r: (r, 0)),
                   pl.BlockSpec((tm, D_MODEL), lambda r: (r, 0))],
        out_shape=[jax.ShapeDtypeStruct((t, D_MODEL), F32),
                   jax.ShapeDtypeStruct((t, D_MODEL), BF16)],
        compiler_params=_params(1),
        name="outproj_norm",
    )(x, mix, wo, g)


def _ffn_seq_kernel(h2_ref, x1_ref, wua_ref, wub_ref, wca_ref, wcb_ref, wd_ref, g_ref, ha0_ref, hb0_ref,
                    y_ref, fca_ref, fcb_ref,
                    acc_ref, ua_ref, ub_ref, ha_ref, hb_ref, *, tm, tiles_per_batch, n_valid, n_f):
    r = pl.program_id(0)
    f = pl.program_id(1)
    rb = r % tiles_per_batch

    @pl.when(f == 0)
    def _():
        acc_ref[...] = jnp.zeros_like(acc_ref)

    def half(wu_ref, wc_ref, u_ref, halo_ref, halo0_ref, tail_ref):
        u_ref[HALO:tm + HALO, :] = jnp.dot(h2_ref[...], wu_ref[...], preferred_element_type=F32)

        @pl.when(rb == 0)
        def _():
            u_ref[0:HALO, :] = halo0_ref[...]

        @pl.when(rb != 0)
        def _():
            u_ref[0:HALO, :] = halo_ref[f]

        conv = (u_ref[HALO:tm + HALO, :] * wc_ref[2:3, :]
                + u_ref[HALO - 1:tm + HALO - 1, :] * wc_ref[1:2, :]
                + u_ref[HALO - 2:tm + HALO - 2, :] * wc_ref[0:1, :])
        halo_ref[f] = u_ref[tm:tm + HALO, :]
        tail_ref[f] = u_ref[HALO + n_valid - 2:HALO + n_valid, :]
        return conv

    up_a = half(wua_ref, wca_ref, ua_ref, ha_ref, ha0_ref, fca_ref)
    up_b = half(wub_ref, wcb_ref, ub_ref, hb_ref, hb0_ref, fcb_ref)
    act = (up_b * jax.nn.sigmoid(up_b) * up_a).astype(BF16)
    acc_ref[...] += jnp.dot(act, wd_ref[...], preferred_element_type=F32)

    @pl.when(f == n_f - 1)
    def _():
        y_ref[...] = _rms(x1_ref[...] + acc_ref[...], g_ref[...])


def _ffn_seq(h2, x1, wu, wc, wd, g, halo_a, halo_b, *, n_batch, tm, tf, n_valid):
    t = h2.shape[0]
    tpb = t // n_batch // tm
    n_f = D_FF // tf
    return pl.pallas_call(
        functools.partial(_ffn_seq_kernel, tm=tm, tiles_per_batch=tpb, n_valid=n_valid, n_f=n_f),
        grid=(t // tm, n_f),
        in_specs=[pl.BlockSpec((tm, D_MODEL), lambda r, f: (r, 0)),
                  pl.BlockSpec((tm, D_MODEL), lambda r, f: (r, 0)),
                  pl.BlockSpec((D_MODEL, tf), lambda r, f: (0, f)),
                  pl.BlockSpec((D_MODEL, tf), lambda r, f: (0, n_f + f)),
                  pl.BlockSpec((CONV_W, tf), lambda r, f: (0, f)),
                  pl.BlockSpec((CONV_W, tf), lambda r, f: (0, n_f + f)),
                  pl.BlockSpec((tf, D_MODEL), lambda r, f: (f, 0)),
                  pl.BlockSpec((1, D_MODEL), lambda r, f: (0, 0)),
                  pl.BlockSpec((HALO, tf), lambda r, f: (0, f)),
                  pl.BlockSpec((HALO, tf), lambda r, f: (0, f))],
        out_specs=[pl.BlockSpec((tm, D_MODEL), lambda r, f: (r, 0)),
                   pl.BlockSpec((None, n_f, 2, tf), lambda r, f: (r // tpb, 0, 0, 0)),
                   pl.BlockSpec((None, n_f, 2, tf), lambda r, f: (r // tpb, 0, 0, 0))],
        out_shape=[jax.ShapeDtypeStruct((t, D_MODEL), F32),
                   jax.ShapeDtypeStruct((n_batch, n_f, 2, tf), F32),
                   jax.ShapeDtypeStruct((n_batch, n_f, 2, tf), F32)],
        scratch_shapes=[pltpu.VMEM((tm, D_MODEL), F32),
                        pltpu.VMEM((tm + HALO, tf), F32),
                        pltpu.VMEM((tm + HALO, tf), F32),
                        pltpu.VMEM((n_f, HALO, tf), F32),
                        pltpu.VMEM((n_f, HALO, tf), F32)],
        compiler_params=_params(2),
        name="ffn_seq",
    )(h2, x1, wu, wu, wc, wc, wd, g, halo_a, halo_b)


def _ffn_step_kernel(h2_ref, x1_ref, wua_ref, wub_ref, wca_ref, wcb_ref, wd_ref, g_ref,
                     s0a_ref, s1a_ref, s0b_ref, s1b_ref,
                     y_ref, upa_ref, upb_ref, acc_ref, *, n_f):
    f = pl.program_id(0)

    @pl.when(f == 0)
    def _():
        acc_ref[...] = jnp.zeros_like(acc_ref)

    def half(wu_ref, wc_ref, s0_ref, s1_ref, up_ref):
        up = jnp.dot(h2_ref[...], wu_ref[...], preferred_element_type=F32)
        up_ref[...] = up
        return s0_ref[...] * wc_ref[0:1, :] + s1_ref[...] * wc_ref[1:2, :] + up * wc_ref[2:3, :]

    up_a = half(wua_ref, wca_ref, s0a_ref, s1a_ref, upa_ref)
    up_b = half(wub_ref, wcb_ref, s0b_ref, s1b_ref, upb_ref)
    act = (up_b * jax.nn.sigmoid(up_b) * up_a).astype(BF16)
    acc_ref[...] += jnp.dot(act, wd_ref[...], preferred_element_type=F32)

    @pl.when(f == n_f - 1)
    def _():
        y_ref[...] = _rms(x1_ref[...] + acc_ref[...], g_ref[...])


def _ffn_step(h2, x1, wu, wc, wd, g, s0, s1, *, tf):
    nb = h2.shape[0]
    n_f = D_FF // tf
    lo = lambda f: (0, f)
    hi = lambda f: (0, n_f + f)
    return pl.pallas_call(
        functools.partial(_ffn_step_kernel, n_f=n_f),
        grid=(n_f,),
        in_specs=[pl.BlockSpec((nb, D_MODEL), lambda f: (0, 0)),
                  pl.BlockSpec((nb, D_MODEL), lambda f: (0, 0)),
                  pl.BlockSpec((D_MODEL, tf), lo),
                  pl.BlockSpec((D_MODEL, tf), hi),
                  pl.BlockSpec((CONV_W, tf), lo),
                  pl.BlockSpec((CONV_W, tf), hi),
                  pl.BlockSpec((tf, D_MODEL), lambda f: (f, 0)),
                  pl.BlockSpec((1, D_MODEL), lambda f: (0, 0)),
                  pl.BlockSpec((nb, tf), lo),
                  pl.BlockSpec((nb, tf), lo),
                  pl.BlockSpec((nb, tf), hi),
                  pl.BlockSpec((nb, tf), hi)],
        out_specs=[pl.BlockSpec((nb, D_MODEL), lambda f: (0, 0)),
                   pl.BlockSpec((nb, tf), lo),
                   pl.BlockSpec((nb, tf), lo)],
        out_shape=[jax.ShapeDtypeStruct((nb, D_MODEL), F32),
                   jax.ShapeDtypeStruct((nb, D_FF), F32),
                   jax.ShapeDtypeStruct((nb, D_FF), F32)],
        scratch_shapes=[pltpu.VMEM((nb, D_MODEL), F32)],
        compiler_params=_params(1),
        name="ffn_step",
    )(h2, x1, wu, wu, wc, wc, wd, g, s0, s1, s0, s1)


def _tail_to_halo(tail):
    return jnp.pad(tail, ((HALO - 2, 0), (0, 0)))


def _join_tail_blocks(t):
    b, n_f, _, tf = t.shape
    return t.transpose(0, 2, 1, 3).reshape(b, 2, n_f * tf)


def kernel(x_prompt, x_sample, state_mlstm_C, state_mlstm_n, state_mlstm_m, state_shortconv,
           state_ffnconv, meta_tokens, norm1_g, w_in, b_if, w_shortconv, w_out, norm2_g,
           w_up, w_ffconv, w_down, norm_f_g):
    assert w_in.shape[0] == 1, "single-layer trunk"
    n_batch, seq, _ = x_prompt.shape
    n_dec = x_sample.shape[0]

    w = w_in[0]

    def heads(off, d):
        return w[:, off:off + N_HEADS * d].reshape(D_MODEL, N_HEADS, d)

    o_q = 3 * D_CONV
    o_k = o_q + N_HEADS * DK
    o_v = o_k + N_HEADS * DK
    o_o = o_v + N_HEADS * DV
    o_gif = o_o + N_HEADS * DV
    o_ga = o_gif + 2 * N_HEADS
    o_gb = o_ga + D_CONV
    wp = jnp.concatenate(
        [heads(0, DV), heads(D_CONV, DV), heads(2 * D_CONV, DV), heads(o_ga, DV), heads(o_v, DV),
         heads(o_o, DV), heads(o_gb, DV), heads(o_q, DK), heads(o_k, DK)], axis=2)
    wp = wp.reshape(D_MODEL, N_HEADS * HEAD_COLS).astype(BF16)
    wg = jnp.pad(w[:, o_gif:o_gif + 2 * N_HEADS], ((0, 0), (0, 128 - 2 * N_HEADS))).astype(BF16)
    bif = jnp.pad(b_if[0], (0, 128 - 2 * N_HEADS)).reshape(1, 128)
    wo = w_out[0].astype(BF16)
    wu = w_up[0].astype(BF16)
    wd = w_down[0].astype(BF16)
    wsc = w_shortconv[0]
    wfc = w_ffconv[0]
    g1 = norm1_g[0].reshape(1, D_MODEL)
    g2 = norm2_g[0].reshape(1, D_MODEL)
    gf = norm_f_g.reshape(1, D_MODEL)

    xm = jnp.pad(meta_tokens, ((0, CHUNK - N_META), (0, 0)))
    hn_m, gates_m = _norm_gates_seq(xm, g1, wg, bif, tm=CHUNK, n_valid=N_META)
    zc = jnp.zeros((N_HEADS, DK, DV), F32)
    zn = jnp.zeros((N_HEADS, 1, DK), F32)
    mix_m, c_m, n_m, m_m, sc_m = _mixers_seq(
        hn_m, wp, gates_m, wsc, zc, zn, zn, jnp.zeros((HALO, D_CONV), F32),
        n_batch=1, tm=CHUNK, n_valid=N_META)
    x1_m, h2_m = _outproj_norm(xm, mix_m, wo, g2, tm=CHUNK)
    zh = jnp.zeros((HALO, D_FF), F32)
    _, fca_m, fcb_m = _ffn_seq(h2_m, x1_m, wu, wfc, wd, gf, zh, zh,
                               n_batch=1, tm=CHUNK, tf=512, n_valid=N_META)
    fca_m = _join_tail_blocks(fca_m)
    fcb_m = _join_tail_blocks(fcb_m)

    xp = x_prompt.reshape(n_batch * seq, D_MODEL)
    tm = 512
    hn_p, gates_p = _norm_gates_seq(xp, g1, wg, bif, tm=tm, n_valid=CHUNK)
    mix_p, c_p, n_p, m_p, sc_p = _mixers_seq(
        hn_p, wp, gates_p, wsc, c_m[0], n_m[0], m_m[0], _tail_to_halo(sc_m[0]),
        n_batch=n_batch, tm=tm, n_valid=tm)
    x1_p, h2_p = _outproj_norm(xp, mix_p, wo, g2, tm=tm)
    y_p, fca_p, fcb_p = _ffn_seq(h2_p, x1_p, wu, wfc, wd, gf,
                                 _tail_to_halo(fca_m[0]), _tail_to_halo(fcb_m[0]),
                                 n_batch=n_batch, tm=tm, tf=512, n_valid=tm)

    xs = x_sample.reshape(n_dec, D_MODEL)
    hn_s, gif_s = _norm_gates_step(xs, g1, wg, bif)
    mprev = jnp.pad(state_mlstm_m[0], ((0, 0), (0, 128 - N_HEADS)))
    n_hb = state_mlstm_n[0].transpose(1, 0, 2)
    mix_s, c_s, n_s, m_s, cu_s = _mixers_step(
        hn_s, wp, gif_s, mprev, wsc, state_shortconv[0, :, 0, :], state_shortconv[0, :, 1, :],
        state_mlstm_C[0], n_hb, bb=32)
    x1_s, h2_s = _outproj_norm(xs, mix_s, wo, g2, tm=n_dec)
    y_s, upa_s, upb_s = _ffn_step(h2_s, x1_s, wu, wfc, wd, gf,
                                  state_ffnconv[0, :, 0, :], state_ffnconv[0, :, 1, :], tf=512)

    return (y_p.reshape(n_batch, seq, D_MODEL),
            y_s.reshape(n_dec, 1, D_MODEL),
            c_p[None],
            n_p.reshape(1, n_batch, N_HEADS, DK),
            m_p[:, :, 0, 0][None],
            sc_p[None],
            jnp.concatenate([_join_tail_blocks(fca_p), _join_tail_blocks(fcb_p)], axis=-1)[None],
            c_s[None],
            n_s.transpose(1, 0, 2)[None],
            m_s[:, :, 0].T[None],
            jnp.stack([state_shortconv[0, :, 1, :], cu_s], axis=1)[None],
            jnp.stack([state_ffnconv[0, :, 1, :], jnp.concatenate([upa_s, upb_s], axis=-1)], axis=1)[None])
```

```python
import functools

import jax
import jax.numpy as jnp
from jax import lax
from jax.experimental import pallas as pl
from jax.experimental.pallas import tpu as pltpu

D_MODEL = 2048
N_META = 16
N_HEADS = 8
DK = 128
DV = 256
D_CONV = D_MODEL
CONV_W = 3
D_FF = 5632
CHUNK = 128
EPS = 1e-6
K_SCALE = DK ** -0.5
NEG = -1e30
HEAD_COLS = 7 * DV + 2 * DK
V7X_VMEM_LIMIT = 56 * 1024 * 1024
HALO = 8
FFN_SUB = 64
F32 = jnp.float32
BF16 = jnp.bfloat16

_U, _CG, _BG, _GA, _V, _O, _GB, _Q, _K = (0, 256, 512, 768, 1024, 1280, 1536, 1792, 1920)


def _params(n_grid):
    return pltpu.CompilerParams(dimension_semantics=("arbitrary",) * n_grid,
                                vmem_limit_bytes=V7X_VMEM_LIMIT)


def _log_sigmoid(x):
    return jnp.minimum(x, 0.0) - jnp.log1p(jnp.exp(-jnp.abs(x)))


def _rms(x, g):
    return x * lax.rsqrt(jnp.mean(x * x, axis=-1, keepdims=True) + EPS) * g


def _prep_win_kernel(u_ref, cg_ref, bg_ref, v_ref, o_ref, q_ref, k_ref,
                     ga0_ref, ga1_ref, ga2_ref, gb0_ref, gb1_ref, gb2_ref, gif_ref, wp_ref, wg_ref):
    shift = 2 * N_HEADS
    ga = jnp.concatenate([ga0_ref[...], ga1_ref[...], ga2_ref[...]], axis=1)[:, shift:shift + DV]
    gb = jnp.concatenate([gb0_ref[...], gb1_ref[...], gb2_ref[...]], axis=1)[:, shift:shift + DV]
    wp_ref[:, _U:_U + DV] = u_ref[...].astype(BF16)
    wp_ref[:, _CG:_CG + DV] = cg_ref[...].astype(BF16)
    wp_ref[:, _BG:_BG + DV] = bg_ref[...].astype(BF16)
    wp_ref[:, _GA:_GA + DV] = ga.astype(BF16)
    wp_ref[:, _V:_V + DV] = v_ref[...].astype(BF16)
    wp_ref[:, _O:_O + DV] = o_ref[...].astype(BF16)
    wp_ref[:, _GB:_GB + DV] = gb.astype(BF16)
    wp_ref[:, _Q:_Q + DK] = q_ref[...].astype(BF16)
    wp_ref[:, _K:_K + DK] = k_ref[...].astype(BF16)
    lane = lax.broadcasted_iota(jnp.int32, gif_ref.shape, 1)
    wg_ref[...] = jnp.where(lane < shift, gif_ref[...], 0.0).astype(BF16)


def _prep_win(w, *, tr):
    wide = lambda base: pl.BlockSpec((tr, DV), lambda i, h: (i, base + h))
    narrow = lambda base: pl.BlockSpec((tr, DK), lambda i, h: (i, base + h))
    pair = lambda base, j: pl.BlockSpec((tr, 128), lambda i, h: (i, base + 2 * h + j))
    o_q = 3 * D_CONV
    o_k = o_q + N_HEADS * DK
    o_v = o_k + N_HEADS * DK
    o_o = o_v + N_HEADS * DV
    o_gif = o_o + N_HEADS * DV
    o_gb = o_gif + 2 * N_HEADS + D_CONV
    assert o_gif % 128 == 0 and (o_gb - 2 * N_HEADS) % 128 == 0
    g_a = o_gif // 128
    g_b = (o_gb - 2 * N_HEADS) // 128
    return pl.pallas_call(
        _prep_win_kernel,
        grid=(D_MODEL // tr, N_HEADS),
        in_specs=[wide(0), wide(D_CONV // DV), wide(2 * D_CONV // DV), wide(o_v // DV), wide(o_o // DV),
                  narrow(o_q // DK), narrow(o_k // DK),
                  pair(g_a, 0), pair(g_a, 1), pair(g_a, 2), pair(g_b, 0), pair(g_b, 1), pair(g_b, 2),
                  pl.BlockSpec((tr, 128), lambda i, h: (i, g_a))],
        out_specs=[pl.BlockSpec((tr, HEAD_COLS), lambda i, h: (i, h)),
                   pl.BlockSpec((tr, 128), lambda i, h: (i, 0))],
        out_shape=[jax.ShapeDtypeStruct((D_MODEL, N_HEADS * HEAD_COLS), BF16),
                   jax.ShapeDtypeStruct((D_MODEL, 128), BF16)],
        compiler_params=_params(2),
        name="prep_win",
    )(*([w] * 14))


def _norm_gates_seq_kernel(x_ref, g_ref, wg_ref, bif_ref, hn_ref, gates_ref, *, tm, n_valid):
    hn = _rms(x_ref[...], g_ref[...]).astype(BF16)
    hn_ref[...] = hn
    gif = jnp.dot(hn, wg_ref[...], preferred_element_type=F32) + bif_ref[...]
    row = lax.broadcasted_iota(jnp.int32, (CHUNK, CHUNK), 0)
    col = lax.broadcasted_iota(jnp.int32, (CHUNK, CHUNK), 1)
    upper = (row <= col).astype(F32)
    for c in range(tm // CHUNK):
        gt = gif[c * CHUNK:(c + 1) * CHUNK, :].T
        ig = gt[0:N_HEADS, :]
        lf = _log_sigmoid(gt[N_HEADS:2 * N_HEADS, :])
        if n_valid < CHUNK:
            tok = lax.broadcasted_iota(jnp.int32, (N_HEADS, CHUNK), 1)
            ig = jnp.where(tok < n_valid, ig, NEG)
            lf = jnp.where(tok < n_valid, lf, 0.0)
        b = jnp.dot(lf, upper, preferred_element_type=F32, precision=lax.Precision.HIGHEST)
        a = ig - b
        for j in range(N_HEADS):
            gates_ref[j, 0:1, c * CHUNK:(c + 1) * CHUNK] = a[j:j + 1, :]
            gates_ref[j, 1:2, c * CHUNK:(c + 1) * CHUNK] = b[j:j + 1, :]


def _norm_gates_seq(x, g, wg, bif, *, tm, n_valid):
    t = x.shape[0]
    return pl.pallas_call(
        functools.partial(_norm_gates_seq_kernel, tm=tm, n_valid=n_valid),
        grid=(t // tm,),
        in_specs=[pl.BlockSpec((tm, D_MODEL), lambda r: (r, 0)),
                  pl.BlockSpec((1, D_MODEL), lambda r: (0, 0)),
                  pl.BlockSpec((D_MODEL, 128), lambda r: (0, 0)),
                  pl.BlockSpec((1, 128), lambda r: (0, 0))],
        out_specs=[pl.BlockSpec((tm, D_MODEL), lambda r: (r, 0)),
                   pl.BlockSpec((N_HEADS, 2, tm), lambda r: (0, 0, r))],
        out_shape=[jax.ShapeDtypeStruct((t, D_MODEL), BF16),
                   jax.ShapeDtypeStruct((N_HEADS, 2, t), F32)],
        compiler_params=_params(1),
        name="norm_gates_seq",
    )(x, g, wg, bif)


def _norm_gates_step_kernel(x_ref, g_ref, wg_ref, bif_ref, hn_ref, gif_ref):
    hn = _rms(x_ref[...], g_ref[...]).astype(BF16)
    hn_ref[...] = hn
    gif_ref[...] = jnp.dot(hn, wg_ref[...], preferred_element_type=F32) + bif_ref[...]


def _norm_gates_step(x, g, wg, bif):
    t = x.shape[0]
    return pl.pallas_call(
        _norm_gates_step_kernel,
        grid=(1,),
        in_specs=[pl.BlockSpec((t, D_MODEL), lambda r: (0, 0)),
                  pl.BlockSpec((1, D_MODEL), lambda r: (0, 0)),
                  pl.BlockSpec((D_MODEL, 128), lambda r: (0, 0)),
                  pl.BlockSpec((1, 128), lambda r: (0, 0))],
        out_specs=[pl.BlockSpec((t, D_MODEL), lambda r: (0, 0)),
                   pl.BlockSpec((t, 128), lambda r: (0, 0))],
        out_shape=[jax.ShapeDtypeStruct((t, D_MODEL), BF16),
                   jax.ShapeDtypeStruct((t, 128), F32)],
        compiler_params=_params(1),
        name="norm_gates_step",
    )(x, g, wg, bif)


def _mixers_seq_kernel(hn_ref, w_ref, gates_ref, wsc_ref, c0_ref, n0_ref, m0_ref, sc0_ref,
                       mix_ref, cout_ref, nout_ref, mout_ref, scout_ref,
                       z_ref, cu_ref, c_ref, n_ref, m_ref, *, tm, tiles_per_batch, n_valid):
    rb = pl.program_id(1) % tiles_per_batch

    @pl.when(rb == 0)
    def _():
        c_ref[...] = c0_ref[...]
        n_ref[...] = n0_ref[...]
        m_ref[...] = m0_ref[...]
        cu_ref[0:HALO, :] = sc0_ref[...]

    z_ref[...] = jnp.dot(hn_ref[...], w_ref[...], preferred_element_type=F32)

    row = lax.broadcasted_iota(jnp.int32, (CHUNK, CHUNK), 0)
    col = lax.broadcasted_iota(jnp.int32, (CHUNK, CHUNK), 1)
    causal = col <= row
    eye = col == row
    w0 = wsc_ref[0:1, :]
    w1 = wsc_ref[1:2, :]
    w2 = wsc_ref[2:3, :]

    for c in range(tm // CHUNK):
        lo = c * CHUNK
        rows = slice(lo, lo + CHUNK)
        cu = z_ref[rows, _CG:_CG + DV] * z_ref[rows, _U:_U + DV]
        cu_ref[HALO + lo:HALO + lo + CHUNK, :] = cu
        conv = (cu * w2 + cu_ref[HALO - 1 + lo:HALO - 1 + lo + CHUNK, :] * w1
                + cu_ref[HALO - 2 + lo:HALO - 2 + lo + CHUNK, :] * w0)
        a_out = z_ref[rows, _BG:_BG + DV] * conv

        q = z_ref[rows, _Q:_Q + DK]
        k = z_ref[rows, _K:_K + DK] * K_SCALE
        v = z_ref[rows, _V:_V + DV]
        a_row = gates_ref[0:1, rows]
        b_row = gates_ref[1:2, rows]
        a_col = jnp.sum(jnp.where(eye, a_row, 0.0), axis=-1, keepdims=True)
        b_col = jnp.sum(jnp.where(eye, b_row, 0.0), axis=-1, keepdims=True)
        m_prev = m_ref[:, 0:1]
        dmat = jnp.where(causal, b_col + a_row, NEG)
        bm = b_col + m_prev
        m_t = jnp.maximum(bm, jnp.max(dmat, axis=-1, keepdims=True))
        w_ts = jnp.exp(dmat - m_t)
        qb = q.astype(BF16)
        kb = k.astype(BF16)
        s = lax.dot_general(qb, kb, (((1,), (1,)), ((), ())), preferred_element_type=F32) * w_ts
        decay = jnp.exp(bm - m_t)
        c_prev = c_ref[...]
        n_prev = n_ref[...]
        num = (decay * jnp.dot(qb, c_prev.astype(BF16), preferred_element_type=F32)
               + jnp.dot(s.astype(BF16), v.astype(BF16), preferred_element_type=F32))
        den = (decay * jnp.sum(q * n_prev, axis=-1, keepdims=True)
               + jnp.sum(s, axis=-1, keepdims=True))
        hb = num / jnp.maximum(jnp.abs(den), jnp.exp(-m_t))
        m_last = m_t[CHUNK - 1:CHUNK, :]
        b_last = b_col[CHUNK - 1:CHUNK, :]
        g_col = jnp.exp(b_last + a_col - m_last)
        carry = jnp.exp(b_last + m_prev - m_last)
        c_ref[...] = carry * c_prev + jnp.dot(k.T.astype(BF16), (g_col * v).astype(BF16),
                                              preferred_element_type=F32)
        n_ref[...] = carry * n_prev + jnp.sum(g_col * k, axis=0, keepdims=True)
        m_ref[...] = jnp.broadcast_to(m_last, (1, DK))
        b_out = jax.nn.sigmoid(z_ref[rows, _O:_O + DV]) * hb

        mix = (jax.nn.sigmoid(z_ref[rows, _GA:_GA + DV]) * a_out
               + jax.nn.sigmoid(z_ref[rows, _GB:_GB + DV]) * b_out)
        mix_ref[rows, :] = mix.astype(BF16)

    @pl.when(rb == tiles_per_batch - 1)
    def _():
        cout_ref[...] = c_ref[...]
        nout_ref[...] = n_ref[...]
        mout_ref[...] = m_ref[...]
        scout_ref[...] = cu_ref[HALO + n_valid - 2:HALO + n_valid, :]

    cu_ref[0:HALO, :] = cu_ref[tm:tm + HALO, :]


def _mixers_seq(hn, wp, gates, wsc, c0, n0, m0, sc0, *, n_batch, tm, n_valid):
    t = hn.shape[0]
    tpb = t // n_batch // tm
    return pl.pallas_call(
        functools.partial(_mixers_seq_kernel, tm=tm, tiles_per_batch=tpb, n_valid=n_valid),
        grid=(N_HEADS, t // tm),
        in_specs=[pl.BlockSpec((tm, D_MODEL), lambda h, r: (r, 0)),
                  pl.BlockSpec((D_MODEL, HEAD_COLS), lambda h, r: (0, h)),
                  pl.BlockSpec((None, 2, tm), lambda h, r: (h, 0, r)),
                  pl.BlockSpec((CONV_W, DV), lambda h, r: (0, h)),
                  pl.BlockSpec((None, DK, DV), lambda h, r: (h, 0, 0)),
                  pl.BlockSpec((None, 1, DK), lambda h, r: (h, 0, 0)),
                  pl.BlockSpec((None, 1, DK), lambda h, r: (h, 0, 0)),
                  pl.BlockSpec((HALO, DV), lambda h, r: (0, h))],
        out_specs=[pl.BlockSpec((tm, DV), lambda h, r: (r, h)),
                   pl.BlockSpec((None, None, DK, DV), lambda h, r: (r // tpb, h, 0, 0)),
                   pl.BlockSpec((None, None, 1, DK), lambda h, r: (r // tpb, h, 0, 0)),
                   pl.BlockSpec((None, None, 1, DK), lambda h, r: (r // tpb, h, 0, 0)),
                   pl.BlockSpec((None, 2, DV), lambda h, r: (r // tpb, 0, h))],
        out_shape=[jax.ShapeDtypeStruct((t, D_MODEL), BF16),
                   jax.ShapeDtypeStruct((n_batch, N_HEADS, DK, DV), F32),
                   jax.ShapeDtypeStruct((n_batch, N_HEADS, 1, DK), F32),
                   jax.ShapeDtypeStruct((n_batch, N_HEADS, 1, DK), F32),
                   jax.ShapeDtypeStruct((n_batch, 2, D_CONV), F32)],
        scratch_shapes=[pltpu.VMEM((tm, HEAD_COLS), F32),
                        pltpu.VMEM((tm + HALO, DV), F32),
                        pltpu.VMEM((DK, DV), F32),
                        pltpu.VMEM((1, DK), F32),
                        pltpu.VMEM((1, DK), F32)],
        compiler_params=_params(2),
        name="mixers_seq",
    )(hn, wp, gates, wsc, c0, n0, m0, sc0)


def _mixers_step_kernel(hn_ref, w_ref, gif_ref, mprev_ref, wsc_ref, s0_ref, s1_ref, c_ref, n_ref,
                        mix_ref, cout_ref, nout_ref, mout_ref, cuout_ref,
                        z_ref, aout_ref, qt_ref, kt_ref, hb_ref, *, bb, n_blocks):
    h = pl.program_id(0)
    j = pl.program_id(1)
    nb = bb * n_blocks

    @pl.when(j == 0)
    def _():
        z = jnp.dot(hn_ref[...], w_ref[...], preferred_element_type=F32)
        z_ref[...] = z
        cu = z[:, _CG:_CG + DV] * z[:, _U:_U + DV]
        conv = s0_ref[...] * wsc_ref[0:1, :] + s1_ref[...] * wsc_ref[1:2, :] + cu * wsc_ref[2:3, :]
        aout_ref[...] = z[:, _BG:_BG + DV] * conv
        cuout_ref[...] = cu
        qt = z[:, _Q:_Q + DK].T
        kt = (z[:, _K:_K + DK] * K_SCALE).T
        for jj in range(n_blocks):
            qt_ref[jj] = qt[:, jj * bb:(jj + 1) * bb]
            kt_ref[jj] = kt[:, jj * bb:(jj + 1) * bb]

    rows = pl.ds(pl.multiple_of(j * bb, bb), bb)
    lane = lax.broadcasted_iota(jnp.int32, (bb, 128), 1)
    gif = gif_ref[rows, :]
    ig = jnp.sum(jnp.where(lane == h, gif, 0.0), axis=-1, keepdims=True)
    lf = _log_sigmoid(jnp.sum(jnp.where(lane == h + N_HEADS, gif, 0.0), axis=-1, keepdims=True))
    m_prev = jnp.sum(jnp.where(lane == h, mprev_ref[rows, :], 0.0), axis=-1, keepdims=True)
    q = z_ref[rows, _Q:_Q + DK]
    k = z_ref[rows, _K:_K + DK] * K_SCALE
    v = z_ref[rows, _V:_V + DV]
    n_prev = n_ref[...]
    m_t = jnp.maximum(lf + m_prev, ig)
    w_in = jnp.exp(ig - m_t)
    decay = jnp.exp(lf + m_prev - m_t)
    s = jnp.sum(q * k, axis=-1, keepdims=True) * w_in
    den = decay * jnp.sum(q * n_prev, axis=-1, keepdims=True) + s
    inv = 1.0 / jnp.maximum(jnp.abs(den), jnp.exp(-m_t))
    nout_ref[...] = decay * n_prev + w_in * k
    mout_ref[...] = jnp.broadcast_to(m_t, (bb, DK))
    qt = qt_ref[j]
    kt = kt_ref[j]
    for i in range(bb):
        c_prev = c_ref[i]
        d_i = decay[i:i + 1, :]
        v_i = v[i:i + 1, :]
        qc = jnp.sum(qt[:, i:i + 1] * c_prev, axis=0, keepdims=True)
        hb_ref[i:i + 1, :] = (d_i * qc + s[i:i + 1, :] * v_i) * inv[i:i + 1, :]
        cout_ref[i] = d_i * c_prev + (w_in[i:i + 1, :] * kt[:, i:i + 1]) * v_i
    b_out = jax.nn.sigmoid(z_ref[rows, _O:_O + DV]) * hb_ref[...]
    mix = (jax.nn.sigmoid(z_ref[rows, _GA:_GA + DV]) * aout_ref[rows, :]
           + jax.nn.sigmoid(z_ref[rows, _GB:_GB + DV]) * b_out)
    mix_ref[...] = mix.astype(BF16)


def _mixers_step(hn, wp, gif, mprev, wsc, s0, s1, c, n, *, bb):
    nb = hn.shape[0]
    n_blocks = nb // bb
    return pl.pallas_call(
        functools.partial(_mixers_step_kernel, bb=bb, n_blocks=n_blocks),
        grid=(N_HEADS, n_blocks),
        in_specs=[pl.BlockSpec((nb, D_MODEL), lambda h, j: (0, 0)),
                  pl.BlockSpec((D_MODEL, HEAD_COLS), lambda h, j: (0, h)),
                  pl.BlockSpec((nb, 128), lambda h, j: (0, 0)),
                  pl.BlockSpec((nb, 128), lambda h, j: (0, 0)),
                  pl.BlockSpec((CONV_W, DV), lambda h, j: (0, h)),
                  pl.BlockSpec((nb, DV), lambda h, j: (0, h)),
                  pl.BlockSpec((nb, DV), lambda h, j: (0, h)),
                  pl.BlockSpec((bb, None, DK, DV), lambda h, j: (j, h, 0, 0)),
                  pl.BlockSpec((None, bb, DK), lambda h, j: (h, j, 0))],
        out_specs=[pl.BlockSpec((bb, DV), lambda h, j: (j, h)),
                   pl.BlockSpec((bb, None, DK, DV), lambda h, j: (j, h, 0, 0)),
                   pl.BlockSpec((None, bb, DK), lambda h, j: (h, j, 0)),
                   pl.BlockSpec((None, bb, DK), lambda h, j: (h, j, 0)),
                   pl.BlockSpec((nb, DV), lambda h, j: (0, h))],
        out_shape=[jax.ShapeDtypeStruct((nb, D_MODEL), BF16),
                   jax.ShapeDtypeStruct((nb, N_HEADS, DK, DV), F32),
                   jax.ShapeDtypeStruct((N_HEADS, nb, DK), F32),
                   jax.ShapeDtypeStruct((N_HEADS, nb, DK), F32),
                   jax.ShapeDtypeStruct((nb, D_CONV), F32)],
        scratch_shapes=[pltpu.VMEM((nb, HEAD_COLS), F32),
                        pltpu.VMEM((nb, DV), F32),
                        pltpu.VMEM((n_blocks, DK, bb), F32),
                        pltpu.VMEM((n_blocks, DK, bb), F32),
                        pltpu.VMEM((bb, DV), F32)],
        compiler_params=_params(2),
        name="mixers_step",
    )(hn, wp, gif, mprev, wsc, s0, s1, c, n)


def _outproj_norm_kernel(x_ref, mix_ref, wo_ref, g_ref, x1_ref, h2_ref):
    x1 = x_ref[...] + jnp.dot(mix_ref[...], wo_ref[...], preferred_element_type=F32)
    x1_ref[...] = x1
    h2_ref[...] = _rms(x1, g_ref[...]).astype(BF16)


def _outproj_norm(x, mix, wo, g, *, tm):
    t = x.shape[0]
    return pl.pallas_call(
        _outproj_norm_kernel,
        grid=(t // tm,),
        in_specs=[pl.BlockSpec((tm, D_MODEL), lambda r: (r, 0)),
                  pl.BlockSpec((tm, D_MODEL), lambda r: (r, 0)),
                  pl.BlockSpec((D_MODEL, D_MODEL), lambda r: (0, 0)),
                  pl.BlockSpec((1, D_MODEL), lambda r: (0, 0))],
        out_specs=[pl.BlockSpec((tm, D_MODEL), lambda r: (r, 0)),
                   pl.BlockSpec((tm, D_MODEL), lambda r: (r, 0))],
        out_shape=[jax.ShapeDtypeStruct((t, D_MODEL), F32),
                   jax.ShapeDtypeStruct((t, D_MODEL), BF16)],
        compiler_params=_params(1),
        name="outproj_norm",
    )(x, mix, wo, g)


def _ffn_seq_kernel(h2_ref, x1_ref, wua_ref, wub_ref, wca_ref, wcb_ref, wd_ref, g_ref, ha0_ref, hb0_ref,
                    y_ref, fca_ref, fcb_ref,
                    acc_ref, ua_ref, ub_ref, ha_ref, hb_ref, act_ref, *, tm, tiles_per_batch, n_valid, n_f):
    r = pl.program_id(0)
    f = pl.program_id(1)
    rb = r % tiles_per_batch

    @pl.when((r == 0) & (f == 0))
    def _():
        ha_ref[...] = jnp.zeros_like(ha_ref)
        hb_ref[...] = jnp.zeros_like(hb_ref)

    @pl.when(f == 0)
    def _():
        acc_ref[...] = jnp.zeros_like(acc_ref)

    def up_proj(wu_ref, u_ref, halo_ref, halo0_ref, tail_ref):
        u_ref[HALO:tm + HALO, :] = jnp.dot(h2_ref[...], wu_ref[...], preferred_element_type=F32)
        u_ref[0:HALO, :] = jnp.where(rb == 0, halo0_ref[...], halo_ref[f])
        halo_ref[f] = u_ref[tm:tm + HALO, :]
        tail_ref[f] = u_ref[HALO + n_valid - 2:HALO + n_valid, :]

    up_proj(wua_ref, ua_ref, ha_ref, ha0_ref, fca_ref)
    up_proj(wub_ref, ub_ref, hb_ref, hb0_ref, fcb_ref)

    def conv(u_ref, wc_ref, lo):
        return (u_ref[HALO + lo:HALO + lo + FFN_SUB, :] * wc_ref[2:3, :]
                + u_ref[HALO - 1 + lo:HALO - 1 + lo + FFN_SUB, :] * wc_ref[1:2, :]
                + u_ref[HALO - 2 + lo:HALO - 2 + lo + FFN_SUB, :] * wc_ref[0:1, :])

    for sb in range(tm // FFN_SUB):
        lo = sb * FFN_SUB
        up_a = conv(ua_ref, wca_ref, lo)
        up_b = conv(ub_ref, wcb_ref, lo)
        act_ref[lo:lo + FFN_SUB, :] = (up_b * jax.nn.sigmoid(up_b) * up_a).astype(BF16)
    acc_ref[...] += jnp.dot(act_ref[...], wd_ref[...], preferred_element_type=F32)

    @pl.when(f == n_f - 1)
    def _():
        y_ref[...] = _rms(x1_ref[...] + acc_ref[...], g_ref[...])


def _ffn_seq(h2, x1, wu, wc, wd, g, halo_a, halo_b, *, n_batch, tm, tf, n_valid):
    t = h2.shape[0]
    tpb = t // n_batch // tm
    n_f = D_FF // tf
    return pl.pallas_call(
        functools.partial(_ffn_seq_kernel, tm=tm, tiles_per_batch=tpb, n_valid=n_valid, n_f=n_f),
        grid=(t // tm, n_f),
        in_specs=[pl.BlockSpec((tm, D_MODEL), lambda r, f: (r, 0)),
                  pl.BlockSpec((tm, D_MODEL), lambda r, f: (r, 0)),
                  pl.BlockSpec((D_MODEL, tf), lambda r, f: (0, f)),
                  pl.BlockSpec((D_MODEL, tf), lambda r, f: (0, n_f + f)),
                  pl.BlockSpec((CONV_W, tf), lambda r, f: (0, f)),
                  pl.BlockSpec((CONV_W, tf), lambda r, f: (0, n_f + f)),
                  pl.BlockSpec((tf, D_MODEL), lambda r, f: (f, 0)),
                  pl.BlockSpec((1, D_MODEL), lambda r, f: (0, 0)),
                  pl.BlockSpec((HALO, tf), lambda r, f: (0, f)),
                  pl.BlockSpec((HALO, tf), lambda r, f: (0, f))],
        out_specs=[pl.BlockSpec((tm, D_MODEL), lambda r, f: (r, 0)),
                   pl.BlockSpec((None, n_f, 2, tf), lambda r, f: (r // tpb, 0, 0, 0)),
                   pl.BlockSpec((None, n_f, 2, tf), lambda r, f: (r // tpb, 0, 0, 0))],
        out_shape=[jax.ShapeDtypeStruct((t, D_MODEL), F32),
                   jax.ShapeDtypeStruct((n_batch, n_f, 2, tf), F32),
                   jax.ShapeDtypeStruct((n_batch, n_f, 2, tf), F32)],
        scratch_shapes=[pltpu.VMEM((tm, D_MODEL), F32),
                        pltpu.VMEM((tm + HALO, tf), F32),
                        pltpu.VMEM((tm + HALO, tf), F32),
                        pltpu.VMEM((n_f, HALO, tf), F32),
                        pltpu.VMEM((n_f, HALO, tf), F32),
                        pltpu.VMEM((tm, tf), BF16)],
        compiler_params=_params(2),
        name="ffn_seq",
    )(h2, x1, wu, wu, wc, wc, wd, g, halo_a, halo_b)


def _ffn_step_kernel(h2_ref, x1_ref, wua_ref, wub_ref, wca_ref, wcb_ref, wd_ref, g_ref,
                     s0a_ref, s1a_ref, s0b_ref, s1b_ref,
                     y_ref, upa_ref, upb_ref, acc_ref, *, n_f):
    f = pl.program_id(0)

    @pl.when(f == 0)
    def _():
        acc_ref[...] = jnp.zeros_like(acc_ref)

    def half(wu_ref, wc_ref, s0_ref, s1_ref, up_ref):
        up = jnp.dot(h2_ref[...], wu_ref[...], preferred_element_type=F32)
        up_ref[...] = up
        return s0_ref[...] * wc_ref[0:1, :] + s1_ref[...] * wc_ref[1:2, :] + up * wc_ref[2:3, :]

    up_a = half(wua_ref, wca_ref, s0a_ref, s1a_ref, upa_ref)
    up_b = half(wub_ref, wcb_ref, s0b_ref, s1b_ref, upb_ref)
    act = (up_b * jax.nn.sigmoid(up_b) * up_a).astype(BF16)
    acc_ref[...] += jnp.dot(act, wd_ref[...], preferred_element_type=F32)

    @pl.when(f == n_f - 1)
    def _():
        y_ref[...] = _rms(x1_ref[...] + acc_ref[...], g_ref[...])


def _ffn_step(h2, x1, wu, wc, wd, g, s0, s1, *, tf):
    nb = h2.shape[0]
    n_f = D_FF // tf
    lo = lambda f: (0, f)
    hi = lambda f: (0, n_f + f)
    return pl.pallas_call(
        functools.partial(_ffn_step_kernel, n_f=n_f),
        grid=(n_f,),
        in_specs=[pl.BlockSpec((nb, D_MODEL), lambda f: (0, 0)),
                  pl.BlockSpec((nb, D_MODEL), lambda f: (0, 0)),
                  pl.BlockSpec((D_MODEL, tf), lo),
                  pl.BlockSpec((D_MODEL, tf), hi),
                  pl.BlockSpec((CONV_W, tf), lo),
                  pl.BlockSpec((CONV_W, tf), hi),
                  pl.BlockSpec((tf, D_MODEL), lambda f: (f, 0)),
                  pl.BlockSpec((1, D_MODEL), lambda f: (0, 0)),
                  pl.BlockSpec((nb, tf), lo),
                  pl.BlockSpec((nb, tf), lo),
                  pl.BlockSpec((nb, tf), hi),
                  pl.BlockSpec((nb, tf), hi)],
        out_specs=[pl.BlockSpec((nb, D_MODEL), lambda f: (0, 0)),
                   pl.BlockSpec((nb, tf), lo),
                   pl.BlockSpec((nb, tf), lo)],
        out_shape=[jax.ShapeDtypeStruct((nb, D_MODEL), F32),
                   jax.ShapeDtypeStruct((nb, D_FF), F32),
                   jax.ShapeDtypeStruct((nb, D_FF), F32)],
        scratch_shapes=[pltpu.VMEM((nb, D_MODEL), F32)],
        compiler_params=_params(1),
        name="ffn_step",
    )(h2, x1, wu, wu, wc, wc, wd, g, s0, s1, s0, s1)


def _tail_to_halo(tail):
    return jnp.pad(tail, ((HALO - 2, 0), (0, 0)))


def _join_tail_blocks(t):
    b, n_f, _, tf = t.shape
    return t.transpose(0, 2, 1, 3).reshape(b, 2, n_f * tf)


def kernel(x_prompt, x_sample, state_mlstm_C, state_mlstm_n, state_mlstm_m, state_shortconv,
           state_ffnconv, meta_tokens, norm1_g, w_in, b_if, w_shortconv, w_out, norm2_g,
           w_up, w_ffconv, w_down, norm_f_g):
    assert w_in.shape[0] == 1, "single-layer trunk"
    n_batch, seq, _ = x_prompt.shape
    n_dec = x_sample.shape[0]

    wp, wg = _prep_win(w_in[0], tr=1024)
    bif = jnp.pad(b_if[0], (0, 128 - 2 * N_HEADS)).reshape(1, 128)
    wo = w_out[0].astype(BF16)
    wu = w_up[0].astype(BF16)
    wd = w_down[0].astype(BF16)
    wsc = w_shortconv[0]
    wfc = w_ffconv[0]
    g1 = norm1_g[0].reshape(1, D_MODEL)
    g2 = norm2_g[0].reshape(1, D_MODEL)
    gf = norm_f_g.reshape(1, D_MODEL)

    xm = jnp.pad(meta_tokens, ((0, CHUNK - N_META), (0, 0)))
    hn_m, gates_m = _norm_gates_seq(xm, g1, wg, bif, tm=CHUNK, n_valid=N_META)
    zc = jnp.zeros((N_HEADS, DK, DV), F32)
    zn = jnp.zeros((N_HEADS, 1, DK), F32)
    mix_m, c_m, n_m, m_m, sc_m = _mixers_seq(
        hn_m, wp, gates_m, wsc, zc, zn, zn, jnp.zeros((HALO, D_CONV), F32),
        n_batch=1, tm=CHUNK, n_valid=N_META)
    x1_m, h2_m = _outproj_norm(xm, mix_m, wo, g2, tm=CHUNK)
    zh = jnp.zeros((HALO, D_FF), F32)
    _, fca_m, fcb_m = _ffn_seq(h2_m, x1_m, wu, wfc, wd, gf, zh, zh,
                               n_batch=1, tm=CHUNK, tf=512, n_valid=N_META)
    fca_m = _join_tail_blocks(fca_m)
    fcb_m = _join_tail_blocks(fcb_m)

    xp = x_prompt.reshape(n_batch * seq, D_MODEL)
    tm = 512
    hn_p, gates_p = _norm_gates_seq(xp, g1, wg, bif, tm=tm, n_valid=CHUNK)
    mix_p, c_p, n_p, m_p, sc_p = _mixers_seq(
        hn_p, wp, gates_p, wsc, c_m[0], n_m[0], m_m[0], _tail_to_halo(sc_m[0]),
        n_batch=n_batch, tm=tm, n_valid=tm)
    x1_p, h2_p = _outproj_norm(xp, mix_p, wo, g2, tm=tm)
    y_p, fca_p, fcb_p = _ffn_seq(h2_p, x1_p, wu, wfc, wd, gf,
                                 _tail_to_halo(fca_m[0]), _tail_to_halo(fcb_m[0]),
                                 n_batch=n_batch, tm=tm, tf=512, n_valid=tm)

    xs = x_sample.reshape(n_dec, D_MODEL)
    hn_s, gif_s = _norm_gates_step(xs, g1, wg, bif)
    mprev = jnp.pad(state_mlstm_m[0], ((0, 0), (0, 128 - N_HEADS)))
    n_hb = state_mlstm_n[0].transpose(1, 0, 2)
    mix_s, c_s, n_s, m_s, cu_s = _mixers_step(
        hn_s, wp, gif_s, mprev, wsc, state_shortconv[0, :, 0, :], state_shortconv[0, :, 1, :],
        state_mlstm_C[0], n_hb, bb=32)
    x1_s, h2_s = _outproj_norm(xs, mix_s, wo, g2, tm=n_dec)
    y_s, upa_s, upb_s = _ffn_step(h2_s, x1_s, wu, wfc, wd, gf,
                                  state_ffnconv[0, :, 0, :], state_ffnconv[0, :, 1, :], tf=512)

    return (y_p.reshape(n_batch, seq, D_MODEL),
            y_s.reshape(n_dec, 1, D_MODEL),
            c_p[None],
            n_p.reshape(1, n_batch, N_HEADS, DK),
            m_p[:, :, 0, 0][None],
            sc_p[None],
            jnp.concatenate([_join_tail_blocks(fca_p), _join_tail_blocks(fcb_p)], axis=-1)[None],
            c_s[None],
            n_s.transpose(1, 0, 2)[None],
            m_s[:, :, 0].T[None],
            jnp.stack([state_shortconv[0, :, 1, :], cu_s], axis=1)[None],
            jnp.stack([state_ffnconv[0, :, 1, :], jnp.concatenate([upa_s, upb_s], axis=-1)], axis=1)[None])
```

```python
import functools

import jax
import jax.numpy as jnp
from jax import lax
from jax.experimental import pallas as pl
from jax.experimental.pallas import tpu as pltpu

D_MODEL = 2048
N_META = 16
N_HEADS = 8
DK = 128
DV = 256
D_CONV = D_MODEL
CONV_W = 3
D_FF = 5632
CHUNK = 128
EPS = 1e-6
K_SCALE = DK ** -0.5
NEG = -1e30
HEAD_COLS = 7 * DV + 2 * DK
V7X_VMEM_LIMIT = 56 * 1024 * 1024
HALO = 8
FFN_SUB = 64
F32 = jnp.float32
BF16 = jnp.bfloat16

_U, _CG, _BG, _GA, _V, _O, _GB, _Q, _K = (0, 256, 512, 768, 1024, 1280, 1536, 1792, 1920)


def _params(n_grid):
    return pltpu.CompilerParams(dimension_semantics=("arbitrary",) * n_grid,
                                vmem_limit_bytes=V7X_VMEM_LIMIT)


def _log_sigmoid(x):
    return jnp.minimum(x, 0.0) - jnp.log1p(jnp.exp(-jnp.abs(x)))


def _rms(x, g):
    return x * lax.rsqrt(jnp.mean(x * x, axis=-1, keepdims=True) + EPS) * g


def _prep_win_kernel(u_ref, cg_ref, bg_ref, ga_ref, v_ref, o_ref, gb_ref, q_ref, k_ref, gif_ref,
                     wp_ref, wg_ref):
    for ref, off, d in ((u_ref, _U, DV), (cg_ref, _CG, DV), (bg_ref, _BG, DV), (ga_ref, _GA, DV),
                        (v_ref, _V, DV), (o_ref, _O, DV), (gb_ref, _GB, DV), (q_ref, _Q, DK), (k_ref, _K, DK)):
        wp_ref[:, off:off + d] = ref[...].T.astype(BF16)
    gif = gif_ref[...].T
    lane = lax.broadcasted_iota(jnp.int32, gif.shape, 1)
    wg_ref[...] = jnp.where(lane < 2 * N_HEADS, gif, 0.0).astype(BF16)


def _prep_win(wt, *, tk):
    o_q = 3 * D_CONV
    o_k = o_q + N_HEADS * DK
    o_v = o_k + N_HEADS * DK
    o_o = o_v + N_HEADS * DV
    o_gif = o_o + N_HEADS * DV
    o_ga = o_gif + 2 * N_HEADS
    o_gb = o_ga + D_CONV
    assert o_ga % 8 == 0
    rows = lambda off, d: pl.BlockSpec((pl.Element(d), pl.Element(tk)),
                                       lambda i, h: (pl.multiple_of(off + d * h, 8), pl.multiple_of(i * tk, 128)))
    return pl.pallas_call(
        _prep_win_kernel,
        grid=(D_MODEL // tk, N_HEADS),
        in_specs=[rows(0, DV), rows(D_CONV, DV), rows(2 * D_CONV, DV), rows(o_ga, DV), rows(o_v, DV),
                  rows(o_o, DV), rows(o_gb, DV), rows(o_q, DK), rows(o_k, DK),
                  pl.BlockSpec((pl.Element(128), pl.Element(tk)),
                               lambda i, h: (o_gif, pl.multiple_of(i * tk, 128)))],
        out_specs=[pl.BlockSpec((tk, HEAD_COLS), lambda i, h: (i, h)),
                   pl.BlockSpec((tk, 128), lambda i, h: (i, 0))],
        out_shape=[jax.ShapeDtypeStruct((D_MODEL, N_HEADS * HEAD_COLS), BF16),
                   jax.ShapeDtypeStruct((D_MODEL, 128), BF16)],
        compiler_params=_params(2),
        name="prep_win",
    )(*([wt] * 10))


def _norm_gates_seq_kernel(x_ref, g_ref, wg_ref, bif_ref, hn_ref, gates_ref, *, tm, n_valid):
    hn = _rms(x_ref[...], g_ref[...]).astype(BF16)
    hn_ref[...] = hn
    gif = jnp.dot(hn, wg_ref[...], preferred_element_type=F32) + bif_ref[...]
    row = lax.broadcasted_iota(jnp.int32, (CHUNK, CHUNK), 0)
    col = lax.broadcasted_iota(jnp.int32, (CHUNK, CHUNK), 1)
    upper = (row <= col).astype(F32)
    for c in range(tm // CHUNK):
        gt = gif[c * CHUNK:(c + 1) * CHUNK, :].T
        ig = gt[0:N_HEADS, :]
        lf = _log_sigmoid(gt[N_HEADS:2 * N_HEADS, :])
        if n_valid < CHUNK:
            tok = lax.broadcasted_iota(jnp.int32, (N_HEADS, CHUNK), 1)
            ig = jnp.where(tok < n_valid, ig, NEG)
            lf = jnp.where(tok < n_valid, lf, 0.0)
        b = jnp.dot(lf, upper, preferred_element_type=F32, precision=lax.Precision.HIGHEST)
        a = ig - b
        for j in range(N_HEADS):
            gates_ref[j, 0:1, c * CHUNK:(c + 1) * CHUNK] = a[j:j + 1, :]
            gates_ref[j, 1:2, c * CHUNK:(c + 1) * CHUNK] = b[j:j + 1, :]


def _norm_gates_seq(x, g, wg, bif, *, tm, n_valid):
    t = x.shape[0]
    return pl.pallas_call(
        functools.partial(_norm_gates_seq_kernel, tm=tm, n_valid=n_valid),
        grid=(t // tm,),
        in_specs=[pl.BlockSpec((tm, D_MODEL), lambda r: (r, 0)),
                  pl.BlockSpec((1, D_MODEL), lambda r: (0, 0)),
                  pl.BlockSpec((D_MODEL, 128), lambda r: (0, 0)),
                  pl.BlockSpec((1, 128), lambda r: (0, 0))],
        out_specs=[pl.BlockSpec((tm, D_MODEL), lambda r: (r, 0)),
                   pl.BlockSpec((N_HEADS, 2, tm), lambda r: (0, 0, r))],
        out_shape=[jax.ShapeDtypeStruct((t, D_MODEL), BF16),
                   jax.ShapeDtypeStruct((N_HEADS, 2, t), F32)],
        compiler_params=_params(1),
        name="norm_gates_seq",
    )(x, g, wg, bif)


def _norm_gates_step_kernel(x_ref, g_ref, wg_ref, bif_ref, hn_ref, gif_ref):
    hn = _rms(x_ref[...], g_ref[...]).astype(BF16)
    hn_ref[...] = hn
    gif_ref[...] = jnp.dot(hn, wg_ref[...], preferred_element_type=F32) + bif_ref[...]


def _norm_gates_step(x, g, wg, bif):
    t = x.shape[0]
    return pl.pallas_call(
        _norm_gates_step_kernel,
        grid=(1,),
        in_specs=[pl.BlockSpec((t, D_MODEL), lambda r: (0, 0)),
                  pl.BlockSpec((1, D_MODEL), lambda r: (0, 0)),
                  pl.BlockSpec((D_MODEL, 128), lambda r: (0, 0)),
                  pl.BlockSpec((1, 128), lambda r: (0, 0))],
        out_specs=[pl.BlockSpec((t, D_MODEL), lambda r: (0, 0)),
                   pl.BlockSpec((t, 128), lambda r: (0, 0))],
        out_shape=[jax.ShapeDtypeStruct((t, D_MODEL), BF16),
                   jax.ShapeDtypeStruct((t, 128), F32)],
        compiler_params=_params(1),
        name="norm_gates_step",
    )(x, g, wg, bif)


def _mixers_seq_kernel(hn_ref, w_ref, gates_ref, wsc_ref, c0_ref, n0_ref, m0_ref, sc0_ref,
                       mix_ref, cout_ref, nout_ref, mout_ref, scout_ref,
                       z_ref, cu_ref, c_ref, n_ref, m_ref, *, tm, tiles_per_batch, n_valid):
    rb = pl.program_id(1) % tiles_per_batch

    @pl.when(rb == 0)
    def _():
        c_ref[...] = c0_ref[...]
        n_ref[...] = n0_ref[...]
        m_ref[...] = m0_ref[...]
        cu_ref[0:HALO, :] = sc0_ref[...]

    z_ref[...] = jnp.dot(hn_ref[...], w_ref[...], preferred_element_type=F32)

    row = lax.broadcasted_iota(jnp.int32, (CHUNK, CHUNK), 0)
    col = lax.broadcasted_iota(jnp.int32, (CHUNK, CHUNK), 1)
    causal = col <= row
    eye = col == row
    w0 = wsc_ref[0:1, :]
    w1 = wsc_ref[1:2, :]
    w2 = wsc_ref[2:3, :]

    for c in range(tm // CHUNK):
        lo = c * CHUNK
        rows = slice(lo, lo + CHUNK)
        cu = z_ref[rows, _CG:_CG + DV] * z_ref[rows, _U:_U + DV]
        cu_ref[HALO + lo:HALO + lo + CHUNK, :] = cu
        conv = (cu * w2 + cu_ref[HALO - 1 + lo:HALO - 1 + lo + CHUNK, :] * w1
                + cu_ref[HALO - 2 + lo:HALO - 2 + lo + CHUNK, :] * w0)
        a_out = z_ref[rows, _BG:_BG + DV] * conv

        q = z_ref[rows, _Q:_Q + DK]
        k = z_ref[rows, _K:_K + DK] * K_SCALE
        v = z_ref[rows, _V:_V + DV]
        a_row = gates_ref[0:1, rows]
        b_row = gates_ref[1:2, rows]
        a_col = jnp.sum(jnp.where(eye, a_row, 0.0), axis=-1, keepdims=True)
        b_col = jnp.sum(jnp.where(eye, b_row, 0.0), axis=-1, keepdims=True)
        m_prev = m_ref[:, 0:1]
        dmat = jnp.where(causal, b_col + a_row, NEG)
        bm = b_col + m_prev
        m_t = jnp.maximum(bm, jnp.max(dmat, axis=-1, keepdims=True))
        w_ts = jnp.exp(dmat - m_t)
        qb = q.astype(BF16)
        kb = k.astype(BF16)
        s = lax.dot_general(qb, kb, (((1,), (1,)), ((), ())), preferred_element_type=F32) * w_ts
        decay = jnp.exp(bm - m_t)
        c_prev = c_ref[...]
        n_prev = n_ref[...]
        num = (decay * jnp.dot(qb, c_prev.astype(BF16), preferred_element_type=F32)
               + jnp.dot(s.astype(BF16), v.astype(BF16), preferred_element_type=F32))
        den = (decay * jnp.sum(q * n_prev, axis=-1, keepdims=True)
               + jnp.sum(s, axis=-1, keepdims=True))
        hb = num / jnp.maximum(jnp.abs(den), jnp.exp(-m_t))
        m_last = m_t[CHUNK - 1:CHUNK, :]
        b_last = b_col[CHUNK - 1:CHUNK, :]
        g_col = jnp.exp(b_last + a_col - m_last)
        carry = jnp.exp(b_last + m_prev - m_last)
        c_ref[...] = carry * c_prev + jnp.dot(k.T.astype(BF16), (g_col * v).astype(BF16),
                                              preferred_element_type=F32)
        n_ref[...] = carry * n_prev + jnp.sum(g_col * k, axis=0, keepdims=True)
        m_ref[...] = jnp.broadcast_to(m_last, (1, DK))
        b_out = jax.nn.sigmoid(z_ref[rows, _O:_O + DV]) * hb

        mix = (jax.nn.sigmoid(z_ref[rows, _GA:_GA + DV]) * a_out
               + jax.nn.sigmoid(z_ref[rows, _GB:_GB + DV]) * b_out)
        mix_ref[rows, :] = mix.astype(BF16)

    @pl.when(rb == tiles_per_batch - 1)
    def _():
        cout_ref[...] = c_ref[...]
        nout_ref[...] = n_ref[...]
        mout_ref[...] = m_ref[...]
        scout_ref[...] = cu_ref[HALO + n_valid - 2:HALO + n_valid, :]

    cu_ref[0:HALO, :] = cu_ref[tm:tm + HALO, :]


def _mixers_seq(hn, wp, gates, wsc, c0, n0, m0, sc0, *, n_batch, tm, n_valid):
    t = hn.shape[0]
    tpb = t // n_batch // tm
    return pl.pallas_call(
        functools.partial(_mixers_seq_kernel, tm=tm, tiles_per_batch=tpb, n_valid=n_valid),
        grid=(N_HEADS, t // tm),
        in_specs=[pl.BlockSpec((tm, D_MODEL), lambda h, r: (r, 0)),
                  pl.BlockSpec((D_MODEL, HEAD_COLS), lambda h, r: (0, h)),
                  pl.BlockSpec((None, 2, tm), lambda h, r: (h, 0, r)),
                  pl.BlockSpec((CONV_W, DV), lambda h, r: (0, h)),
                  pl.BlockSpec((None, DK, DV), lambda h, r: (h, 0, 0)),
                  pl.BlockSpec((None, 1, DK), lambda h, r: (h, 0, 0)),
                  pl.BlockSpec((None, 1, DK), lambda h, r: (h, 0, 0)),
                  pl.BlockSpec((HALO, DV), lambda h, r: (0, h))],
        out_specs=[pl.BlockSpec((tm, DV), lambda h, r: (r, h)),
                   pl.BlockSpec((None, None, DK, DV), lambda h, r: (r // tpb, h, 0, 0)),
                   pl.BlockSpec((None, None, 1, DK), lambda h, r: (r // tpb, h, 0, 0)),
                   pl.BlockSpec((None, None, 1, DK), lambda h, r: (r // tpb, h, 0, 0)),
                   pl.BlockSpec((None, 2, DV), lambda h, r: (r // tpb, 0, h))],
        out_shape=[jax.ShapeDtypeStruct((t, D_MODEL), BF16),
                   jax.ShapeDtypeStruct((n_batch, N_HEADS, DK, DV), F32),
                   jax.ShapeDtypeStruct((n_batch, N_HEADS, 1, DK), F32),
                   jax.ShapeDtypeStruct((n_batch, N_HEADS, 1, DK), F32),
                   jax.ShapeDtypeStruct((n_batch, 2, D_CONV), F32)],
        scratch_shapes=[pltpu.VMEM((tm, HEAD_COLS), F32),
                        pltpu.VMEM((tm + HALO, DV), F32),
                        pltpu.VMEM((DK, DV), F32),
                        pltpu.VMEM((1, DK), F32),
                        pltpu.VMEM((1, DK), F32)],
        compiler_params=_params(2),
        name="mixers_seq",
    )(hn, wp, gates, wsc, c0, n0, m0, sc0)


def _mixers_step_kernel(hn_ref, w_ref, gif_ref, mprev_ref, wsc_ref, s0_ref, s1_ref, c_ref, n_ref,
                        mix_ref, cout_ref, nout_ref, mout_ref, cuout_ref,
                        z_ref, aout_ref, qt_ref, kt_ref, hb_ref, *, bb, n_blocks):
    h = pl.program_id(0)
    j = pl.program_id(1)
    nb = bb * n_blocks

    @pl.when(j == 0)
    def _():
        z = jnp.dot(hn_ref[...], w_ref[...], preferred_element_type=F32)
        z_ref[...] = z
        cu = z[:, _CG:_CG + DV] * z[:, _U:_U + DV]
        conv = s0_ref[...] * wsc_ref[0:1, :] + s1_ref[...] * wsc_ref[1:2, :] + cu * wsc_ref[2:3, :]
        aout_ref[...] = z[:, _BG:_BG + DV] * conv
        cuout_ref[...] = cu
        qt = z[:, _Q:_Q + DK].T
        kt = (z[:, _K:_K + DK] * K_SCALE).T
        for jj in range(n_blocks):
            qt_ref[jj] = qt[:, jj * bb:(jj + 1) * bb]
            kt_ref[jj] = kt[:, jj * bb:(jj + 1) * bb]

    rows = pl.ds(pl.multiple_of(j * bb, bb), bb)
    lane = lax.broadcasted_iota(jnp.int32, (bb, 128), 1)
    gif = gif_ref[rows, :]
    ig = jnp.sum(jnp.where(lane == h, gif, 0.0), axis=-1, keepdims=True)
    lf = _log_sigmoid(jnp.sum(jnp.where(lane == h + N_HEADS, gif, 0.0), axis=-1, keepdims=True))
    m_prev = jnp.sum(jnp.where(lane == h, mprev_ref[rows, :], 0.0), axis=-1, keepdims=True)
    q = z_ref[rows, _Q:_Q + DK]
    k = z_ref[rows, _K:_K + DK] * K_SCALE
    v = z_ref[rows, _V:_V + DV]
    n_prev = n_ref[...]
    m_t = jnp.maximum(lf + m_prev, ig)
    w_in = jnp.exp(ig - m_t)
    decay = jnp.exp(lf + m_prev - m_t)
    s = jnp.sum(q * k, axis=-1, keepdims=True) * w_in
    den = decay * jnp.sum(q * n_prev, axis=-1, keepdims=True) + s
    inv = 1.0 / jnp.maximum(jnp.abs(den), jnp.exp(-m_t))
    nout_ref[...] = decay * n_prev + w_in * k
    mout_ref[...] = jnp.broadcast_to(m_t, (bb, DK))
    qt = qt_ref[j]
    kt = kt_ref[j]
    for i in range(bb):
        c_prev = c_ref[i]
        d_i = decay[i:i + 1, :]
        v_i = v[i:i + 1, :]
        qc = jnp.sum(qt[:, i:i + 1] * c_prev, axis=0, keepdims=True)
        hb_ref[i:i + 1, :] = (d_i * qc + s[i:i + 1, :] * v_i) * inv[i:i + 1, :]
        cout_ref[i] = d_i * c_prev + (w_in[i:i + 1, :] * kt[:, i:i + 1]) * v_i
    b_out = jax.nn.sigmoid(z_ref[rows, _O:_O + DV]) * hb_ref[...]
    mix = (jax.nn.sigmoid(z_ref[rows, _GA:_GA + DV]) * aout_ref[rows, :]
           + jax.nn.sigmoid(z_ref[rows, _GB:_GB + DV]) * b_out)
    mix_ref[...] = mix.astype(BF16)


def _mixers_step(hn, wp, gif, mprev, wsc, s0, s1, c, n, *, bb):
    nb = hn.shape[0]
    n_blocks = nb // bb
    return pl.pallas_call(
        functools.partial(_mixers_step_kernel, bb=bb, n_blocks=n_blocks),
        grid=(N_HEADS, n_blocks),
        in_specs=[pl.BlockSpec((nb, D_MODEL), lambda h, j: (0, 0)),
                  pl.BlockSpec((D_MODEL, HEAD_COLS), lambda h, j: (0, h)),
                  pl.BlockSpec((nb, 128), lambda h, j: (0, 0)),
                  pl.BlockSpec((nb, 128), lambda h, j: (0, 0)),
                  pl.BlockSpec((CONV_W, DV), lambda h, j: (0, h)),
                  pl.BlockSpec((nb, DV), lambda h, j: (0, h)),
                  pl.BlockSpec((nb, DV), lambda h, j: (0, h)),
                  pl.BlockSpec((bb, None, DK, DV), lambda h, j: (j, h, 0, 0)),
                  pl.BlockSpec((None, bb, DK), lambda h, j: (h, j, 0))],
        out_specs=[pl.BlockSpec((bb, DV), lambda h, j: (j, h)),
                   pl.BlockSpec((bb, None, DK, DV), lambda h, j: (j, h, 0, 0)),
                   pl.BlockSpec((None, bb, DK), lambda h, j: (h, j, 0)),
                   pl.BlockSpec((None, bb, DK), lambda h, j: (h, j, 0)),
                   pl.BlockSpec((nb, DV), lambda h, j: (0, h))],
        out_shape=[jax.ShapeDtypeStruct((nb, D_MODEL), BF16),
                   jax.ShapeDtypeStruct((nb, N_HEADS, DK, DV), F32),
                   jax.ShapeDtypeStruct((N_HEADS, nb, DK), F32),
                   jax.ShapeDtypeStruct((N_HEADS, nb, DK), F32),
                   jax.ShapeDtypeStruct((nb, D_CONV), F32)],
        scratch_shapes=[pltpu.VMEM((nb, HEAD_COLS), F32),
                        pltpu.VMEM((nb, DV), F32),
                        pltpu.VMEM((n_blocks, DK, bb), F32),
                        pltpu.VMEM((n_blocks, DK, bb), F32),
                        pltpu.VMEM((bb, DV), F32)],
        compiler_params=_params(2),
        name="mixers_step",
    )(hn, wp, gif, mprev, wsc, s0, s1, c, n)


def _outproj_norm_kernel(x_ref, mix_ref, wo_ref, g_ref, x1_ref, h2_ref):
    x1 = x_ref[...] + jnp.dot(mix_ref[...], wo_ref[...], preferred_element_type=F32)
    x1_ref[...] = x1
    h2_ref[...] = _rms(x1, g_ref[...]).astype(BF16)


def _outproj_norm(x, mix, wo, g, *, tm):
    t = x.shape[0]
    return pl.pallas_call(
        _outproj_norm_kernel,
        grid=(t // tm,),
        in_specs=[pl.BlockSpec((tm, D_MODEL), lambda r: (r, 0)),
                  pl.BlockSpec((tm, D_MODEL), lambda r: (r, 0)),
                  pl.BlockSpec((D_MODEL, D_MODEL), lambda r: (0, 0)),
                  pl.BlockSpec((1, D_MODEL), lambda r: (0, 0))],
        out_specs=[pl.BlockSpec((tm, D_MODEL), lambda r: (r, 0)),
                   pl.BlockSpec((tm, D_MODEL), lambda r: (r, 0))],
        out_shape=[jax.ShapeDtypeStruct((t, D_MODEL), F32),
                   jax.ShapeDtypeStruct((t, D_MODEL), BF16)],
        compiler_params=_params(1),
        name="outproj_norm",
    )(x, mix, wo, g)


def _ffn_seq_kernel(h2_ref, x1_ref, wua_ref, wub_ref, wca_ref, wcb_ref, wd_ref, g_ref, ha0_ref, hb0_ref,
                    y_ref, fca_ref, fcb_ref,
                    acc_ref, ua_ref, ub_ref, ha_ref, hb_ref, act_ref, *, tm, tiles_per_batch, n_valid, n_f):
    r = pl.program_id(0)
    f = pl.program_id(1)
    rb = r % tiles_per_batch

    @pl.when((r == 0) & (f == 0))
    def _():
        ha_ref[...] = jnp.zeros_like(ha_ref)
        hb_ref[...] = jnp.zeros_like(hb_ref)

    @pl.when(f == 0)
    def _():
        acc_ref[...] = jnp.zeros_like(acc_ref)

    def up_proj(wu_ref, u_ref, halo_ref, halo0_ref, tail_ref):
        u_ref[HALO:tm + HALO, :] = jnp.dot(h2_ref[...], wu_ref[...], preferred_element_type=F32)
        u_ref[0:HALO, :] = jnp.where(rb == 0, halo0_ref[...], halo_ref[f])
        halo_ref[f] = u_ref[tm:tm + HALO, :]
        tail_ref[f] = u_ref[HALO + n_valid - 2:HALO + n_valid, :]

    up_proj(wua_ref, ua_ref, ha_ref, ha0_ref, fca_ref)
    up_proj(wub_ref, ub_ref, hb_ref, hb0_ref, fcb_ref)

    def conv(u_ref, wc_ref, lo):
        return (u_ref[HALO + lo:HALO + lo + FFN_SUB, :] * wc_ref[2:3, :]
                + u_ref[HALO - 1 + lo:HALO - 1 + lo + FFN_SUB, :] * wc_ref[1:2, :]
                + u_ref[HALO - 2 + lo:HALO - 2 + lo + FFN_SUB, :] * wc_ref[0:1, :])

    for sb in range(tm // FFN_SUB):
        lo = sb * FFN_SUB
        up_a = conv(ua_ref, wca_ref, lo)
        up_b = conv(ub_ref, wcb_ref, lo)
        act_ref[lo:lo + FFN_SUB, :] = (up_b * jax.nn.sigmoid(up_b) * up_a).astype(BF16)
    acc_ref[...] += jnp.dot(act_ref[...], wd_ref[...], preferred_element_type=F32)

    @pl.when(f == n_f - 1)
    def _():
        y_ref[...] = _rms(x1_ref[...] + acc_ref[...], g_ref[...])


def _ffn_seq(h2, x1, wu, wc, wd, g, halo_a, halo_b, *, n_batch, tm, tf, n_valid):
    t = h2.shape[0]
    tpb = t // n_batch // tm
    n_f = D_FF // tf
    return pl.pallas_call(
        functools.partial(_ffn_seq_kernel, tm=tm, tiles_per_batch=tpb, n_valid=n_valid, n_f=n_f),
        grid=(t // tm, n_f),
        in_specs=[pl.BlockSpec((tm, D_MODEL), lambda r, f: (r, 0)),
                  pl.BlockSpec((tm, D_MODEL), lambda r, f: (r, 0)),
                  pl.BlockSpec((D_MODEL, tf), lambda r, f: (0, f)),
                  pl.BlockSpec((D_MODEL, tf), lambda r, f: (0, n_f + f)),
                  pl.BlockSpec((CONV_W, tf), lambda r, f: (0, f)),
                  pl.BlockSpec((CONV_W, tf), lambda r, f: (0, n_f + f)),
                  pl.BlockSpec((tf, D_MODEL), lambda r, f: (f, 0)),
                  pl.BlockSpec((1, D_MODEL), lambda r, f: (0, 0)),
                  pl.BlockSpec((HALO, tf), lambda r, f: (0, f)),
                  pl.BlockSpec((HALO, tf), lambda r, f: (0, f))],
        out_specs=[pl.BlockSpec((tm, D_MODEL), lambda r, f: (r, 0)),
                   pl.BlockSpec((None, n_f, 2, tf), lambda r, f: (r // tpb, 0, 0, 0)),
                   pl.BlockSpec((None, n_f, 2, tf), lambda r, f: (r // tpb, 0, 0, 0))],
        out_shape=[jax.ShapeDtypeStruct((t, D_MODEL), F32),
                   jax.ShapeDtypeStruct((n_batch, n_f, 2, tf), F32),
                   jax.ShapeDtypeStruct((n_batch, n_f, 2, tf), F32)],
        scratch_shapes=[pltpu.VMEM((tm, D_MODEL), F32),
                        pltpu.VMEM((tm + HALO, tf), F32),
                        pltpu.VMEM((tm + HALO, tf), F32),
                        pltpu.VMEM((n_f, HALO, tf), F32),
                        pltpu.VMEM((n_f, HALO, tf), F32),
                        pltpu.VMEM((tm, tf), BF16)],
        compiler_params=_params(2),
        name="ffn_seq",
    )(h2, x1, wu, wu, wc, wc, wd, g, halo_a, halo_b)


def _ffn_step_kernel(h2_ref, x1_ref, wua_ref, wub_ref, wca_ref, wcb_ref, wd_ref, g_ref,
                     s0a_ref, s1a_ref, s0b_ref, s1b_ref,
                     y_ref, upa_ref, upb_ref, acc_ref, *, n_f):
    f = pl.program_id(0)

    @pl.when(f == 0)
    def _():
        acc_ref[...] = jnp.zeros_like(acc_ref)

    def half(wu_ref, wc_ref, s0_ref, s1_ref, up_ref):
        up = jnp.dot(h2_ref[...], wu_ref[...], preferred_element_type=F32)
        up_ref[...] = up
        return s0_ref[...] * wc_ref[0:1, :] + s1_ref[...] * wc_ref[1:2, :] + up * wc_ref[2:3, :]

    up_a = half(wua_ref, wca_ref, s0a_ref, s1a_ref, upa_ref)
    up_b = half(wub_ref, wcb_ref, s0b_ref, s1b_ref, upb_ref)
    act = (up_b * jax.nn.sigmoid(up_b) * up_a).astype(BF16)
    acc_ref[...] += jnp.dot(act, wd_ref[...], preferred_element_type=F32)

    @pl.when(f == n_f - 1)
    def _():
        y_ref[...] = _rms(x1_ref[...] + acc_ref[...], g_ref[...])


def _ffn_step(h2, x1, wu, wc, wd, g, s0, s1, *, tf):
    nb = h2.shape[0]
    n_f = D_FF // tf
    lo = lambda f: (0, f)
    hi = lambda f: (0, n_f + f)
    return pl.pallas_call(
        functools.partial(_ffn_step_kernel, n_f=n_f),
        grid=(n_f,),
        in_specs=[pl.BlockSpec((nb, D_MODEL), lambda f: (0, 0)),
                  pl.BlockSpec((nb, D_MODEL), lambda f: (0, 0)),
                  pl.BlockSpec((D_MODEL, tf), lo),
                  pl.BlockSpec((D_MODEL, tf), hi),
                  pl.BlockSpec((CONV_W, tf), lo),
                  pl.BlockSpec((CONV_W, tf), hi),
                  pl.BlockSpec((tf, D_MODEL), lambda f: (f, 0)),
                  pl.BlockSpec((1, D_MODEL), lambda f: (0, 0)),
                  pl.BlockSpec((nb, tf), lo),
                  pl.BlockSpec((nb, tf), lo),
                  pl.BlockSpec((nb, tf), hi),
                  pl.BlockSpec((nb, tf), hi)],
        out_specs=[pl.BlockSpec((nb, D_MODEL), lambda f: (0, 0)),
                   pl.BlockSpec((nb, tf), lo),
                   pl.BlockSpec((nb, tf), lo)],
        out_shape=[jax.ShapeDtypeStruct((nb, D_MODEL), F32),
                   jax.ShapeDtypeStruct((nb, D_FF), F32),
                   jax.ShapeDtypeStruct((nb, D_FF), F32)],
        scratch_shapes=[pltpu.VMEM((nb, D_MODEL), F32)],
        compiler_params=_params(1),
        name="ffn_step",
    )(h2, x1, wu, wu, wc, wc, wd, g, s0, s1, s0, s1)


def _tail_to_halo(tail):
    return jnp.pad(tail, ((HALO - 2, 0), (0, 0)))


def _join_tail_blocks(t):
    b, n_f, _, tf = t.shape
    return t.transpose(0, 2, 1, 3).reshape(b, 2, n_f * tf)


def kernel(x_prompt, x_sample, state_mlstm_C, state_mlstm_n, state_mlstm_m, state_shortconv,
           state_ffnconv, meta_tokens, norm1_g, w_in, b_if, w_shortconv, w_out, norm2_g,
           w_up, w_ffconv, w_down, norm_f_g):
    assert w_in.shape[0] == 1, "single-layer trunk"
    n_batch, seq, _ = x_prompt.shape
    n_dec = x_sample.shape[0]

    wp, wg = _prep_win(w_in[0].T, tk=1024)
    bif = jnp.pad(b_if[0], (0, 128 - 2 * N_HEADS)).reshape(1, 128)
    wo = w_out[0].astype(BF16)
    wu = w_up[0].astype(BF16)
    wd = w_down[0].astype(BF16)
    wsc = w_shortconv[0]
    wfc = w_ffconv[0]
    g1 = norm1_g[0].reshape(1, D_MODEL)
    g2 = norm2_g[0].reshape(1, D_MODEL)
    gf = norm_f_g.reshape(1, D_MODEL)

    xm = jnp.pad(meta_tokens, ((0, CHUNK - N_META), (0, 0)))
    hn_m, gates_m = _norm_gates_seq(xm, g1, wg, bif, tm=CHUNK, n_valid=N_META)
    zc = jnp.zeros((N_HEADS, DK, DV), F32)
    zn = jnp.zeros((N_HEADS, 1, DK), F32)
    mix_m, c_m, n_m, m_m, sc_m = _mixers_seq(
        hn_m, wp, gates_m, wsc, zc, zn, zn, jnp.zeros((HALO, D_CONV), F32),
        n_batch=1, tm=CHUNK, n_valid=N_META)
    x1_m, h2_m = _outproj_norm(xm, mix_m, wo, g2, tm=CHUNK)
    zh = jnp.zeros((HALO, D_FF), F32)
    _, fca_m, fcb_m = _ffn_seq(h2_m, x1_m, wu, wfc, wd, gf, zh, zh,
                               n_batch=1, tm=CHUNK, tf=512, n_valid=N_META)
    fca_m = _join_tail_blocks(fca_m)
    fcb_m = _join_tail_blocks(fcb_m)

    xp = x_prompt.reshape(n_batch * seq, D_MODEL)
    tm = 512
    hn_p, gates_p = _norm_gates_seq(xp, g1, wg, bif, tm=tm, n_valid=CHUNK)
    mix_p, c_p, n_p, m_p, sc_p = _mixers_seq(
        hn_p, wp, gates_p, wsc, c_m[0], n_m[0], m_m[0], _tail_to_halo(sc_m[0]),
        n_batch=n_batch, tm=tm, n_valid=tm)
    x1_p, h2_p = _outproj_norm(xp, mix_p, wo, g2, tm=tm)
    y_p, fca_p, fcb_p = _ffn_seq(h2_p, x1_p, wu, wfc, wd, gf,
                                 _tail_to_halo(fca_m[0]), _tail_to_halo(fcb_m[0]),
                                 n_batch=n_batch, tm=tm, tf=512, n_valid=tm)

    xs = x_sample.reshape(n_dec, D_MODEL)
    hn_s, gif_s = _norm_gates_step(xs, g1, wg, bif)
    mprev = jnp.pad(state_mlstm_m[0], ((0, 0), (0, 128 - N_HEADS)))
    n_hb = state_mlstm_n[0].transpose(1, 0, 2)
    mix_s, c_s, n_s, m_s, cu_s = _mixers_step(
        hn_s, wp, gif_s, mprev, wsc, state_shortconv[0, :, 0, :], state_shortconv[0, :, 1, :],
        state_mlstm_C[0], n_hb, bb=32)
    x1_s, h2_s = _outproj_norm(xs, mix_s, wo, g2, tm=n_dec)
    y_s, upa_s, upb_s = _ffn_step(h2_s, x1_s, wu, wfc, wd, gf,
                                  state_ffnconv[0, :, 0, :], state_ffnconv[0, :, 1, :], tf=512)

    return (y_p.reshape(n_batch, seq, D_MODEL),
            y_s.reshape(n_dec, 1, D_MODEL),
            c_p[None],
            n_p.reshape(1, n_batch, N_HEADS, DK),
            m_p[:, :, 0, 0][None],
            sc_p[None],
            jnp.concatenate([_join_tail_blocks(fca_p), _join_tail_blocks(fcb_p)], axis=-1)[None],
            c_s[None],
            n_s.transpose(1, 0, 2)[None],
            m_s[:, :, 0].T[None],
            jnp.stack([state_shortconv[0, :, 1, :], cu_s], axis=1)[None],
            jnp.stack([state_ffnconv[0, :, 1, :], jnp.concatenate([upa_s, upb_s], axis=-1)], axis=1)[None])
```

```python
import functools

import jax
import jax.numpy as jnp
from jax import lax
from jax.experimental import pallas as pl
from jax.experimental.pallas import tpu as pltpu

D_MODEL = 2048
N_META = 16
N_HEADS = 8
DK = 128
DV = 256
D_CONV = D_MODEL
CONV_W = 3
D_FF = 5632
CHUNK = 128
EPS = 1e-6
K_SCALE = DK ** -0.5
NEG = -1e30
HEAD_COLS = 7 * DV + 2 * DK
V7X_VMEM_LIMIT = 56 * 1024 * 1024
HALO = 8
FFN_SUB = 64
F32 = jnp.float32
BF16 = jnp.bfloat16

_Q, _K, _V, _U, _CG, _BG, _GA, _O, _GB = (0, 128, 256, 512, 768, 1024, 1280, 1536, 1792)


def _params(n_grid):
    return pltpu.CompilerParams(dimension_semantics=("arbitrary",) * n_grid,
                                vmem_limit_bytes=V7X_VMEM_LIMIT)


def _log_sigmoid(x):
    return jnp.minimum(x, 0.0) - jnp.log1p(jnp.exp(-jnp.abs(x)))


def _rms(x, g):
    return x * lax.rsqrt(jnp.mean(x * x, axis=-1, keepdims=True) + EPS) * g


def _prep_win_kernel(u_ref, cg_ref, bg_ref, ga_ref, v_ref, o_ref, gb_ref, q_ref, k_ref, gif_ref,
                     wp_ref, wg_ref):
    for ref, off, d in ((u_ref, _U, DV), (cg_ref, _CG, DV), (bg_ref, _BG, DV), (ga_ref, _GA, DV),
                        (v_ref, _V, DV), (o_ref, _O, DV), (gb_ref, _GB, DV), (q_ref, _Q, DK), (k_ref, _K, DK)):
        wp_ref[:, off:off + d] = ref[...].T.astype(BF16)
    gif = gif_ref[...].T
    lane = lax.broadcasted_iota(jnp.int32, gif.shape, 1)
    wg_ref[...] = jnp.where(lane < 2 * N_HEADS, gif, 0.0).astype(BF16)


def _prep_win(wt, *, tk):
    o_q = 3 * D_CONV
    o_k = o_q + N_HEADS * DK
    o_v = o_k + N_HEADS * DK
    o_o = o_v + N_HEADS * DV
    o_gif = o_o + N_HEADS * DV
    o_ga = o_gif + 2 * N_HEADS
    o_gb = o_ga + D_CONV
    assert o_ga % 8 == 0
    rows = lambda off, d: pl.BlockSpec((pl.Element(d), pl.Element(tk)),
                                       lambda i, h: (pl.multiple_of(off + d * h, 8), pl.multiple_of(i * tk, 128)))
    return pl.pallas_call(
        _prep_win_kernel,
        grid=(D_MODEL // tk, N_HEADS),
        in_specs=[rows(0, DV), rows(D_CONV, DV), rows(2 * D_CONV, DV), rows(o_ga, DV), rows(o_v, DV),
                  rows(o_o, DV), rows(o_gb, DV), rows(o_q, DK), rows(o_k, DK),
                  pl.BlockSpec((pl.Element(128), pl.Element(tk)),
                               lambda i, h: (o_gif, pl.multiple_of(i * tk, 128)))],
        out_specs=[pl.BlockSpec((tk, HEAD_COLS), lambda i, h: (i, h)),
                   pl.BlockSpec((tk, 128), lambda i, h: (i, 0))],
        out_shape=[jax.ShapeDtypeStruct((D_MODEL, N_HEADS * HEAD_COLS), BF16),
                   jax.ShapeDtypeStruct((D_MODEL, 128), BF16)],
        compiler_params=_params(2),
        name="prep_win",
    )(*([wt] * 10))


def _norm_gates_seq_kernel(x_ref, g_ref, wg_ref, bif_ref, hn_ref, gates_ref, *, tm, n_valid):
    hn = _rms(x_ref[...], g_ref[...]).astype(BF16)
    hn_ref[...] = hn
    gif = jnp.dot(hn, wg_ref[...], preferred_element_type=F32) + bif_ref[...]
    row = lax.broadcasted_iota(jnp.int32, (CHUNK, CHUNK), 0)
    col = lax.broadcasted_iota(jnp.int32, (CHUNK, CHUNK), 1)
    upper = (row <= col).astype(F32)
    for c in range(tm // CHUNK):
        gt = gif[c * CHUNK:(c + 1) * CHUNK, :].T
        ig = gt[0:N_HEADS, :]
        lf = _log_sigmoid(gt[N_HEADS:2 * N_HEADS, :])
        if n_valid < CHUNK:
            tok = lax.broadcasted_iota(jnp.int32, (N_HEADS, CHUNK), 1)
            ig = jnp.where(tok < n_valid, ig, NEG)
            lf = jnp.where(tok < n_valid, lf, 0.0)
        b = jnp.dot(lf, upper, preferred_element_type=F32, precision=lax.Precision.HIGHEST)
        a = ig - b
        for j in range(N_HEADS):
            gates_ref[j, 0:1, c * CHUNK:(c + 1) * CHUNK] = a[j:j + 1, :]
            gates_ref[j, 1:2, c * CHUNK:(c + 1) * CHUNK] = b[j:j + 1, :]


def _norm_gates_seq(x, g, wg, bif, *, tm, n_valid):
    t = x.shape[0]
    return pl.pallas_call(
        functools.partial(_norm_gates_seq_kernel, tm=tm, n_valid=n_valid),
        grid=(t // tm,),
        in_specs=[pl.BlockSpec((tm, D_MODEL), lambda r: (r, 0)),
                  pl.BlockSpec((1, D_MODEL), lambda r: (0, 0)),
                  pl.BlockSpec((D_MODEL, 128), lambda r: (0, 0)),
                  pl.BlockSpec((1, 128), lambda r: (0, 0))],
        out_specs=[pl.BlockSpec((tm, D_MODEL), lambda r: (r, 0)),
                   pl.BlockSpec((N_HEADS, 2, tm), lambda r: (0, 0, r))],
        out_shape=[jax.ShapeDtypeStruct((t, D_MODEL), BF16),
                   jax.ShapeDtypeStruct((N_HEADS, 2, t), F32)],
        compiler_params=_params(1),
        name="norm_gates_seq",
    )(x, g, wg, bif)


def _norm_gates_step_kernel(x_ref, g_ref, wg_ref, bif_ref, hn_ref, gif_ref):
    hn = _rms(x_ref[...], g_ref[...]).astype(BF16)
    hn_ref[...] = hn
    gif_ref[...] = jnp.dot(hn, wg_ref[...], preferred_element_type=F32) + bif_ref[...]


def _norm_gates_step(x, g, wg, bif):
    t = x.shape[0]
    return pl.pallas_call(
        _norm_gates_step_kernel,
        grid=(1,),
        in_specs=[pl.BlockSpec((t, D_MODEL), lambda r: (0, 0)),
                  pl.BlockSpec((1, D_MODEL), lambda r: (0, 0)),
                  pl.BlockSpec((D_MODEL, 128), lambda r: (0, 0)),
                  pl.BlockSpec((1, 128), lambda r: (0, 0))],
        out_specs=[pl.BlockSpec((t, D_MODEL), lambda r: (0, 0)),
                   pl.BlockSpec((t, 128), lambda r: (0, 0))],
        out_shape=[jax.ShapeDtypeStruct((t, D_MODEL), BF16),
                   jax.ShapeDtypeStruct((t, 128), F32)],
        compiler_params=_params(1),
        name="norm_gates_step",
    )(x, g, wg, bif)


def _mixers_seq_kernel(hn_ref, w_ref, gates_ref, wsc_ref, c0_ref, n0_ref, m0_ref, sc0_ref,
                       mix_ref, cout_ref, nout_ref, mout_ref, scout_ref,
                       z_ref, cu_ref, c_ref, n_ref, m_ref, dmat_ref, qk_ref, sv_ref, kv_ref, hb_ref, amix_ref,
                       *, tm, tiles_per_batch, n_valid):
    rb = pl.program_id(1) % tiles_per_batch
    nc = tm // CHUNK

    @pl.when(rb == 0)
    def _():
        c_ref[...] = c0_ref[...]
        n_ref[...] = n0_ref[...]
        m_ref[...] = m0_ref[...]
        cu_ref[0:HALO, :] = sc0_ref[...]

    row = lax.broadcasted_iota(jnp.int32, (CHUNK, CHUNK), 0)
    col = lax.broadcasted_iota(jnp.int32, (CHUNK, CHUNK), 1)
    causal = col <= row
    eye = col == row
    st = [dict() for _ in range(nc)]
    rows_of = lambda c: slice(c * CHUNK, (c + 1) * CHUNK)
    k_of = lambda rows: z_ref[rows, _K:_K + DK] * K_SCALE

    def proj(j):
        cols = slice(j * DV, (j + 1) * DV)
        z_ref[:, cols] = jnp.dot(hn_ref[...], w_ref[:, cols], preferred_element_type=F32)

    def gate_terms(c):
        rows, d = rows_of(c), st[c]
        a_row = gates_ref[0:1, rows]
        b_row = gates_ref[1:2, rows]
        d["a_col"] = jnp.sum(jnp.where(eye, a_row, 0.0), axis=-1, keepdims=True)
        d["b_col"] = jnp.sum(jnp.where(eye, b_row, 0.0), axis=-1, keepdims=True)
        dmat = jnp.where(causal, d["b_col"] + a_row, NEG)
        dmat_ref[rows, :] = dmat
        d["mloc"] = jnp.max(dmat, axis=-1, keepdims=True)
        qk_ref[rows, :] = lax.dot_general(z_ref[rows, _Q:_Q + DK].astype(BF16), k_of(rows).astype(BF16),
                                          (((1,), (1,)), ((), ())), preferred_element_type=F32)

    def stabilisers():
        m_prev = m_ref[:, 0:1]
        for c in range(nc):
            d = st[c]
            d["m_prev"] = m_prev
            d["bm"] = d["b_col"] + m_prev
            d["m_t"] = jnp.maximum(d["bm"], d["mloc"])
            d["m_last"] = d["m_t"][CHUNK - 1:CHUNK, :]
            m_prev = d["m_last"]
        m_ref[...] = jnp.broadcast_to(m_prev, (1, DK))

    def intra(c):
        rows, d = rows_of(c), st[c]
        k = k_of(rows)
        v = z_ref[rows, _V:_V + DV]
        s = qk_ref[rows, :] * jnp.exp(dmat_ref[rows, :] - d["m_t"])
        sv_ref[rows, :] = jnp.dot(s.astype(BF16), v.astype(BF16), preferred_element_type=F32)
        d["ssum"] = jnp.sum(s, axis=-1, keepdims=True)
        d["decay"] = jnp.exp(d["bm"] - d["m_t"])
        d["hden"] = jnp.exp(-d["m_t"])
        b_last = d["b_col"][CHUNK - 1:CHUNK, :]
        g_col = jnp.exp(b_last + d["a_col"] - d["m_last"])
        d["carry"] = jnp.exp(b_last + d["m_prev"] - d["m_last"])
        kv_ref[c] = jnp.dot(k.T.astype(BF16), (g_col * v).astype(BF16), preferred_element_type=F32)
        d["kn"] = jnp.sum(g_col * k, axis=0, keepdims=True)

    def inter(c):
        rows, d = rows_of(c), st[c]
        q = z_ref[rows, _Q:_Q + DK]
        c_prev = c_ref[...]
        n_prev = n_ref[...]
        num = (d["decay"] * jnp.dot(q.astype(BF16), c_prev.astype(BF16), preferred_element_type=F32)
               + sv_ref[rows, :])
        den = d["decay"] * jnp.sum(q * n_prev, axis=-1, keepdims=True) + d["ssum"]
        hb_ref[rows, :] = num / jnp.maximum(jnp.abs(den), d["hden"])
        c_ref[...] = d["carry"] * c_prev + kv_ref[c]
        n_ref[...] = d["carry"] * n_prev + d["kn"]

    def conv_mix(c):
        lo = c * CHUNK
        rows = rows_of(c)
        cu = z_ref[rows, _CG:_CG + DV] * z_ref[rows, _U:_U + DV]
        cu_ref[HALO + lo:HALO + lo + CHUNK, :] = cu
        conv = (cu * wsc_ref[2:3, :] + cu_ref[HALO - 1 + lo:HALO - 1 + lo + CHUNK, :] * wsc_ref[1:2, :]
                + cu_ref[HALO - 2 + lo:HALO - 2 + lo + CHUNK, :] * wsc_ref[0:1, :])
        amix_ref[rows, :] = jax.nn.sigmoid(z_ref[rows, _GA:_GA + DV]) * (z_ref[rows, _BG:_BG + DV] * conv)

    def merge(c):
        rows = rows_of(c)
        b_out = jax.nn.sigmoid(z_ref[rows, _O:_O + DV]) * hb_ref[rows, :]
        mix = amix_ref[rows, :] + jax.nn.sigmoid(z_ref[rows, _GB:_GB + DV]) * b_out
        mix_ref[rows, :] = mix.astype(BF16)

    n_proj = HEAD_COLS // DV
    assert (_Q // DV, _V // DV) == (0, 1) and _GB // DV == n_proj - 1 and max(_U, _CG, _BG, _GA) // DV < n_proj - 2
    vec = ([functools.partial(gate_terms, c) for c in range(nc)] + [stabilisers]
           + [functools.partial(intra, c) for c in range(nc)]
           + [functools.partial(inter, c) for c in range(nc)]
           + [functools.partial(conv_mix, c) for c in range(nc)])
    proj(0)
    proj(1)
    done = 0
    for i, j in enumerate(range(2, n_proj)):
        proj(j)
        upto = (i + 1) * len(vec) // (n_proj - 2)
        for fn in vec[done:upto]:
            fn()
        done = upto
    for c in range(nc):
        merge(c)

    @pl.when(rb == tiles_per_batch - 1)
    def _():
        cout_ref[...] = c_ref[...]
        nout_ref[...] = n_ref[...]
        mout_ref[...] = m_ref[...]
        scout_ref[...] = cu_ref[HALO + n_valid - 2:HALO + n_valid, :]

    cu_ref[0:HALO, :] = cu_ref[tm:tm + HALO, :]


def _mixers_seq(hn, wp, gates, wsc, c0, n0, m0, sc0, *, n_batch, tm, n_valid):
    t = hn.shape[0]
    tpb = t // n_batch // tm
    return pl.pallas_call(
        functools.partial(_mixers_seq_kernel, tm=tm, tiles_per_batch=tpb, n_valid=n_valid),
        grid=(N_HEADS, t // tm),
        in_specs=[pl.BlockSpec((tm, D_MODEL), lambda h, r: (r, 0)),
                  pl.BlockSpec((D_MODEL, HEAD_COLS), lambda h, r: (0, h)),
                  pl.BlockSpec((None, 2, tm), lambda h, r: (h, 0, r)),
                  pl.BlockSpec((CONV_W, DV), lambda h, r: (0, h)),
                  pl.BlockSpec((None, DK, DV), lambda h, r: (h, 0, 0)),
                  pl.BlockSpec((None, 1, DK), lambda h, r: (h, 0, 0)),
                  pl.BlockSpec((None, 1, DK), lambda h, r: (h, 0, 0)),
                  pl.BlockSpec((HALO, DV), lambda h, r: (0, h))],
        out_specs=[pl.BlockSpec((tm, DV), lambda h, r: (r, h)),
                   pl.BlockSpec((None, None, DK, DV), lambda h, r: (r // tpb, h, 0, 0)),
                   pl.BlockSpec((None, None, 1, DK), lambda h, r: (r // tpb, h, 0, 0)),
                   pl.BlockSpec((None, None, 1, DK), lambda h, r: (r // tpb, h, 0, 0)),
                   pl.BlockSpec((None, 2, DV), lambda h, r: (r // tpb, 0, h))],
        out_shape=[jax.ShapeDtypeStruct((t, D_MODEL), BF16),
                   jax.ShapeDtypeStruct((n_batch, N_HEADS, DK, DV), F32),
                   jax.ShapeDtypeStruct((n_batch, N_HEADS, 1, DK), F32),
                   jax.ShapeDtypeStruct((n_batch, N_HEADS, 1, DK), F32),
                   jax.ShapeDtypeStruct((n_batch, 2, D_CONV), F32)],
        scratch_shapes=[pltpu.VMEM((tm, HEAD_COLS), F32),
                        pltpu.VMEM((tm + HALO, DV), F32),
                        pltpu.VMEM((DK, DV), F32),
                        pltpu.VMEM((1, DK), F32),
                        pltpu.VMEM((1, DK), F32),
                        pltpu.VMEM((tm, CHUNK), F32),
                        pltpu.VMEM((tm, CHUNK), F32),
                        pltpu.VMEM((tm, DV), F32),
                        pltpu.VMEM((tm // CHUNK, DK, DV), F32),
                        pltpu.VMEM((tm, DV), F32),
                        pltpu.VMEM((tm, DV), F32)],
        compiler_params=_params(2),
        name="mixers_seq",
    )(hn, wp, gates, wsc, c0, n0, m0, sc0)


def _mixers_step_kernel(hn_ref, w_ref, gif_ref, mprev_ref, wsc_ref, s0_ref, s1_ref, c_ref, n_ref,
                        mix_ref, cout_ref, nout_ref, mout_ref, cuout_ref,
                        z_ref, aout_ref, qt_ref, kt_ref, hb_ref, *, bb, n_blocks):
    h = pl.program_id(0)
    j = pl.program_id(1)
    nb = bb * n_blocks

    @pl.when(j == 0)
    def _():
        z = jnp.dot(hn_ref[...], w_ref[...], preferred_element_type=F32)
        z_ref[...] = z
        cu = z[:, _CG:_CG + DV] * z[:, _U:_U + DV]
        conv = s0_ref[...] * wsc_ref[0:1, :] + s1_ref[...] * wsc_ref[1:2, :] + cu * wsc_ref[2:3, :]
        aout_ref[...] = z[:, _BG:_BG + DV] * conv
        cuout_ref[...] = cu
        qt = z[:, _Q:_Q + DK].T
        kt = (z[:, _K:_K + DK] * K_SCALE).T
        for jj in range(n_blocks):
            qt_ref[jj] = qt[:, jj * bb:(jj + 1) * bb]
            kt_ref[jj] = kt[:, jj * bb:(jj + 1) * bb]

    rows = pl.ds(pl.multiple_of(j * bb, bb), bb)
    lane = lax.broadcasted_iota(jnp.int32, (bb, 128), 1)
    gif = gif_ref[rows, :]
    ig = jnp.sum(jnp.where(lane == h, gif, 0.0), axis=-1, keepdims=True)
    lf = _log_sigmoid(jnp.sum(jnp.where(lane == h + N_HEADS, gif, 0.0), axis=-1, keepdims=True))
    m_prev = jnp.sum(jnp.where(lane == h, mprev_ref[rows, :], 0.0), axis=-1, keepdims=True)
    q = z_ref[rows, _Q:_Q + DK]
    k = z_ref[rows, _K:_K + DK] * K_SCALE
    v = z_ref[rows, _V:_V + DV]
    n_prev = n_ref[...]
    m_t = jnp.maximum(lf + m_prev, ig)
    w_in = jnp.exp(ig - m_t)
    decay = jnp.exp(lf + m_prev - m_t)
    s = jnp.sum(q * k, axis=-1, keepdims=True) * w_in
    den = decay * jnp.sum(q * n_prev, axis=-1, keepdims=True) + s
    inv = 1.0 / jnp.maximum(jnp.abs(den), jnp.exp(-m_t))
    nout_ref[...] = decay * n_prev + w_in * k
    mout_ref[...] = jnp.broadcast_to(m_t, (bb, DK))
    qt = qt_ref[j]
    kt = kt_ref[j]
    for i in range(bb):
        c_prev = c_ref[i]
        d_i = decay[i:i + 1, :]
        v_i = v[i:i + 1, :]
        qc = jnp.sum(qt[:, i:i + 1] * c_prev, axis=0, keepdims=True)
        hb_ref[i:i + 1, :] = (d_i * qc + s[i:i + 1, :] * v_i) * inv[i:i + 1, :]
        cout_ref[i] = d_i * c_prev + (w_in[i:i + 1, :] * kt[:, i:i + 1]) * v_i
    b_out = jax.nn.sigmoid(z_ref[rows, _O:_O + DV]) * hb_ref[...]
    mix = (jax.nn.sigmoid(z_ref[rows, _GA:_GA + DV]) * aout_ref[rows, :]
           + jax.nn.sigmoid(z_ref[rows, _GB:_GB + DV]) * b_out)
    mix_ref[...] = mix.astype(BF16)


def _mixers_step(hn, wp, gif, mprev, wsc, s0, s1, c, n, *, bb):
    nb = hn.shape[0]
    n_blocks = nb // bb
    return pl.pallas_call(
        functools.partial(_mixers_step_kernel, bb=bb, n_blocks=n_blocks),
        grid=(N_HEADS, n_blocks),
        in_specs=[pl.BlockSpec((nb, D_MODEL), lambda h, j: (0, 0)),
                  pl.BlockSpec((D_MODEL, HEAD_COLS), lambda h, j: (0, h)),
                  pl.BlockSpec((nb, 128), lambda h, j: (0, 0)),
                  pl.BlockSpec((nb, 128), lambda h, j: (0, 0)),
                  pl.BlockSpec((CONV_W, DV), lambda h, j: (0, h)),
                  pl.BlockSpec((nb, DV), lambda h, j: (0, h)),
                  pl.BlockSpec((nb, DV), lambda h, j: (0, h)),
                  pl.BlockSpec((bb, None, DK, DV), lambda h, j: (j, h, 0, 0)),
                  pl.BlockSpec((None, bb, DK), lambda h, j: (h, j, 0))],
        out_specs=[pl.BlockSpec((bb, DV), lambda h, j: (j, h)),
                   pl.BlockSpec((bb, None, DK, DV), lambda h, j: (j, h, 0, 0)),
                   pl.BlockSpec((None, bb, DK), lambda h, j: (h, j, 0)),
                   pl.BlockSpec((None, bb, DK), lambda h, j: (h, j, 0)),
                   pl.BlockSpec((nb, DV), lambda h, j: (0, h))],
        out_shape=[jax.ShapeDtypeStruct((nb, D_MODEL), BF16),
                   jax.ShapeDtypeStruct((nb, N_HEADS, DK, DV), F32),
                   jax.ShapeDtypeStruct((N_HEADS, nb, DK), F32),
                   jax.ShapeDtypeStruct((N_HEADS, nb, DK), F32),
                   jax.ShapeDtypeStruct((nb, D_CONV), F32)],
        scratch_shapes=[pltpu.VMEM((nb, HEAD_COLS), F32),
                        pltpu.VMEM((nb, DV), F32),
                        pltpu.VMEM((n_blocks, DK, bb), F32),
                        pltpu.VMEM((n_blocks, DK, bb), F32),
                        pltpu.VMEM((bb, DV), F32)],
        compiler_params=_params(2),
        name="mixers_step",
    )(hn, wp, gif, mprev, wsc, s0, s1, c, n)


def _outproj_norm_kernel(x_ref, mix_ref, wo_ref, g_ref, x1_ref, h2_ref):
    x1 = x_ref[...] + jnp.dot(mix_ref[...], wo_ref[...], preferred_element_type=F32)
    x1_ref[...] = x1
    h2_ref[...] = _rms(x1, g_ref[...]).astype(BF16)


def _outproj_norm(x, mix, wo, g, *, tm):
    t = x.shape[0]
    return pl.pallas_call(
        _outproj_norm_kernel,
        grid=(t // tm,),
        in_specs=[pl.BlockSpec((tm, D_MODEL), lambda r: (r, 0)),
                  pl.BlockSpec((tm, D_MODEL), lambda r: (r, 0)),
                  pl.BlockSpec((D_MODEL, D_MODEL), lambda r: (0, 0)),
                  pl.BlockSpec((1, D_MODEL), lambda r: (0, 0))],
        out_specs=[pl.BlockSpec((tm, D_MODEL), lambda r: (r, 0)),
                   pl.BlockSpec((tm, D_MODEL), lambda r: (r, 0))],
        out_shape=[jax.ShapeDtypeStruct((t, D_MODEL), F32),
                   jax.ShapeDtypeStruct((t, D_MODEL), BF16)],
        compiler_params=_params(1),
        name="outproj_norm",
    )(x, mix, wo, g)


def _ffn_seq_kernel(h2_ref, x1_ref, wua_ref, wub_ref, wca_ref, wcb_ref, wd_ref, g_ref, ha0_ref, hb0_ref,
                    y_ref, fca_ref, fcb_ref,
                    acc_ref, ua_ref, ub_ref, ha_ref, hb_ref, act_ref, *, tm, tiles_per_batch, n_valid, n_f):
    r = pl.program_id(0)
    f = pl.program_id(1)
    rb = r % tiles_per_batch

    @pl.when((r == 0) & (f == 0))
    def _():
        ha_ref[...] = jnp.zeros_like(ha_ref)
        hb_ref[...] = jnp.zeros_like(hb_ref)

    @pl.when(f == 0)
    def _():
        acc_ref[...] = jnp.zeros_like(acc_ref)

    def up_proj(wu_ref, u_ref, halo_ref, halo0_ref, tail_ref):
        u_ref[HALO:tm + HALO, :] = jnp.dot(h2_ref[...], wu_ref[...], preferred_element_type=F32)
        u_ref[0:HALO, :] = jnp.where(rb == 0, halo0_ref[...], halo_ref[f])
        halo_ref[f] = u_ref[tm:tm + HALO, :]
        tail_ref[f] = u_ref[HALO + n_valid - 2:HALO + n_valid, :]

    up_proj(wua_ref, ua_ref, ha_ref, ha0_ref, fca_ref)
    up_proj(wub_ref, ub_ref, hb_ref, hb0_ref, fcb_ref)

    def conv(u_ref, wc_ref, lo):
        return (u_ref[HALO + lo:HALO + lo + FFN_SUB, :] * wc_ref[2:3, :]
                + u_ref[HALO - 1 + lo:HALO - 1 + lo + FFN_SUB, :] * wc_ref[1:2, :]
                + u_ref[HALO - 2 + lo:HALO - 2 + lo + FFN_SUB, :] * wc_ref[0:1, :])

    for sb in range(tm // FFN_SUB):
        lo = sb * FFN_SUB
        up_a = conv(ua_ref, wca_ref, lo)
        up_b = conv(ub_ref, wcb_ref, lo)
        act_ref[lo:lo + FFN_SUB, :] = (up_b * jax.nn.sigmoid(up_b) * up_a).astype(BF16)
    acc_ref[...] += jnp.dot(act_ref[...], wd_ref[...], preferred_element_type=F32)

    @pl.when(f == n_f - 1)
    def _():
        y_ref[...] = _rms(x1_ref[...] + acc_ref[...], g_ref[...])


def _ffn_seq(h2, x1, wu, wc, wd, g, halo_a, halo_b, *, n_batch, tm, tf, n_valid):
    t = h2.shape[0]
    tpb = t // n_batch // tm
    n_f = D_FF // tf
    return pl.pallas_call(
        functools.partial(_ffn_seq_kernel, tm=tm, tiles_per_batch=tpb, n_valid=n_valid, n_f=n_f),
        grid=(t // tm, n_f),
        in_specs=[pl.BlockSpec((tm, D_MODEL), lambda r, f: (r, 0)),
                  pl.BlockSpec((tm, D_MODEL), lambda r, f: (r, 0)),
                  pl.BlockSpec((D_MODEL, tf), lambda r, f: (0, f)),
                  pl.BlockSpec((D_MODEL, tf), lambda r, f: (0, n_f + f)),
                  pl.BlockSpec((CONV_W, tf), lambda r, f: (0, f)),
                  pl.BlockSpec((CONV_W, tf), lambda r, f: (0, n_f + f)),
                  pl.BlockSpec((tf, D_MODEL), lambda r, f: (f, 0)),
                  pl.BlockSpec((1, D_MODEL), lambda r, f: (0, 0)),
                  pl.BlockSpec((HALO, tf), lambda r, f: (0, f)),
                  pl.BlockSpec((HALO, tf), lambda r, f: (0, f))],
        out_specs=[pl.BlockSpec((tm, D_MODEL), lambda r, f: (r, 0)),
                   pl.BlockSpec((None, n_f, 2, tf), lambda r, f: (r // tpb, 0, 0, 0)),
                   pl.BlockSpec((None, n_f, 2, tf), lambda r, f: (r // tpb, 0, 0, 0))],
        out_shape=[jax.ShapeDtypeStruct((t, D_MODEL), F32),
                   jax.ShapeDtypeStruct((n_batch, n_f, 2, tf), F32),
                   jax.ShapeDtypeStruct((n_batch, n_f, 2, tf), F32)],
        scratch_shapes=[pltpu.VMEM((tm, D_MODEL), F32),
                        pltpu.VMEM((tm + HALO, tf), F32),
                        pltpu.VMEM((tm + HALO, tf), F32),
                        pltpu.VMEM((n_f, HALO, tf), F32),
                        pltpu.VMEM((n_f, HALO, tf), F32),
                        pltpu.VMEM((tm, tf), BF16)],
        compiler_params=_params(2),
        name="ffn_seq",
    )(h2, x1, wu, wu, wc, wc, wd, g, halo_a, halo_b)


def _ffn_step_kernel(h2_ref, x1_ref, wua_ref, wub_ref, wca_ref, wcb_ref, wd_ref, g_ref,
                     s0a_ref, s1a_ref, s0b_ref, s1b_ref,
                     y_ref, upa_ref, upb_ref, acc_ref, *, n_f):
    f = pl.program_id(0)

    @pl.when(f == 0)
    def _():
        acc_ref[...] = jnp.zeros_like(acc_ref)

    def half(wu_ref, wc_ref, s0_ref, s1_ref, up_ref):
        up = jnp.dot(h2_ref[...], wu_ref[...], preferred_element_type=F32)
        up_ref[...] = up
        return s0_ref[...] * wc_ref[0:1, :] + s1_ref[...] * wc_ref[1:2, :] + up * wc_ref[2:3, :]

    up_a = half(wua_ref, wca_ref, s0a_ref, s1a_ref, upa_ref)
    up_b = half(wub_ref, wcb_ref, s0b_ref, s1b_ref, upb_ref)
    act = (up_b * jax.nn.sigmoid(up_b) * up_a).astype(BF16)
    acc_ref[...] += jnp.dot(act, wd_ref[...], preferred_element_type=F32)

    @pl.when(f == n_f - 1)
    def _():
        y_ref[...] = _rms(x1_ref[...] + acc_ref[...], g_ref[...])


def _ffn_step(h2, x1, wu, wc, wd, g, s0, s1, *, tf):
    nb = h2.shape[0]
    n_f = D_FF // tf
    lo = lambda f: (0, f)
    hi = lambda f: (0, n_f + f)
    return pl.pallas_call(
        functools.partial(_ffn_step_kernel, n_f=n_f),
        grid=(n_f,),
        in_specs=[pl.BlockSpec((nb, D_MODEL), lambda f: (0, 0)),
                  pl.BlockSpec((nb, D_MODEL), lambda f: (0, 0)),
                  pl.BlockSpec((D_MODEL, tf), lo),
                  pl.BlockSpec((D_MODEL, tf), hi),
                  pl.BlockSpec((CONV_W, tf), lo),
                  pl.BlockSpec((CONV_W, tf), hi),
                  pl.BlockSpec((tf, D_MODEL), lambda f: (f, 0)),
                  pl.BlockSpec((1, D_MODEL), lambda f: (0, 0)),
                  pl.BlockSpec((nb, tf), lo),
                  pl.BlockSpec((nb, tf), lo),
                  pl.BlockSpec((nb, tf), hi),
                  pl.BlockSpec((nb, tf), hi)],
        out_specs=[pl.BlockSpec((nb, D_MODEL), lambda f: (0, 0)),
                   pl.BlockSpec((nb, tf), lo),
                   pl.BlockSpec((nb, tf), lo)],
        out_shape=[jax.ShapeDtypeStruct((nb, D_MODEL), F32),
                   jax.ShapeDtypeStruct((nb, D_FF), F32),
                   jax.ShapeDtypeStruct((nb, D_FF), F32)],
        scratch_shapes=[pltpu.VMEM((nb, D_MODEL), F32)],
        compiler_params=_params(1),
        name="ffn_step",
    )(h2, x1, wu, wu, wc, wc, wd, g, s0, s1, s0, s1)


def _tail_to_halo(tail):
    return jnp.pad(tail, ((HALO - 2, 0), (0, 0)))


def _join_tail_blocks(t):
    b, n_f, _, tf = t.shape
    return t.transpose(0, 2, 1, 3).reshape(b, 2, n_f * tf)


def kernel(x_prompt, x_sample, state_mlstm_C, state_mlstm_n, state_mlstm_m, state_shortconv,
           state_ffnconv, meta_tokens, norm1_g, w_in, b_if, w_shortconv, w_out, norm2_g,
           w_up, w_ffconv, w_down, norm_f_g):
    assert w_in.shape[0] == 1, "single-layer trunk"
    n_batch, seq, _ = x_prompt.shape
    n_dec = x_sample.shape[0]

    wp, wg = _prep_win(w_in[0].T, tk=1024)
    bif = jnp.pad(b_if[0], (0, 128 - 2 * N_HEADS)).reshape(1, 128)
    wo = w_out[0].astype(BF16)
    wu = w_up[0].astype(BF16)
    wd = w_down[0].astype(BF16)
    wsc = w_shortconv[0]
    wfc = w_ffconv[0]
    g1 = norm1_g[0].reshape(1, D_MODEL)
    g2 = norm2_g[0].reshape(1, D_MODEL)
    gf = norm_f_g.reshape(1, D_MODEL)

    xm = jnp.pad(meta_tokens, ((0, CHUNK - N_META), (0, 0)))
    hn_m, gates_m = _norm_gates_seq(xm, g1, wg, bif, tm=CHUNK, n_valid=N_META)
    zc = jnp.zeros((N_HEADS, DK, DV), F32)
    zn = jnp.zeros((N_HEADS, 1, DK), F32)
    mix_m, c_m, n_m, m_m, sc_m = _mixers_seq(
        hn_m, wp, gates_m, wsc, zc, zn, zn, jnp.zeros((HALO, D_CONV), F32),
        n_batch=1, tm=CHUNK, n_valid=N_META)
    x1_m, h2_m = _outproj_norm(xm, mix_m, wo, g2, tm=CHUNK)
    zh = jnp.zeros((HALO, D_FF), F32)
    _, fca_m, fcb_m = _ffn_seq(h2_m, x1_m, wu, wfc, wd, gf, zh, zh,
                               n_batch=1, tm=CHUNK, tf=512, n_valid=N_META)
    fca_m = _join_tail_blocks(fca_m)
    fcb_m = _join_tail_blocks(fcb_m)

    xp = x_prompt.reshape(n_batch * seq, D_MODEL)
    tm = 512
    hn_p, gates_p = _norm_gates_seq(xp, g1, wg, bif, tm=tm, n_valid=CHUNK)
    mix_p, c_p, n_p, m_p, sc_p = _mixers_seq(
        hn_p, wp, gates_p, wsc, c_m[0], n_m[0], m_m[0], _tail_to_halo(sc_m[0]),
        n_batch=n_batch, tm=tm, n_valid=tm)
    x1_p, h2_p = _outproj_norm(xp, mix_p, wo, g2, tm=tm)
    y_p, fca_p, fcb_p = _ffn_seq(h2_p, x1_p, wu, wfc, wd, gf,
                                 _tail_to_halo(fca_m[0]), _tail_to_halo(fcb_m[0]),
                                 n_batch=n_batch, tm=tm, tf=512, n_valid=tm)

    xs = x_sample.reshape(n_dec, D_MODEL)
    hn_s, gif_s = _norm_gates_step(xs, g1, wg, bif)
    mprev = jnp.pad(state_mlstm_m[0], ((0, 0), (0, 128 - N_HEADS)))
    n_hb = state_mlstm_n[0].transpose(1, 0, 2)
    mix_s, c_s, n_s, m_s, cu_s = _mixers_step(
        hn_s, wp, gif_s, mprev, wsc, state_shortconv[0, :, 0, :], state_shortconv[0, :, 1, :],
        state_mlstm_C[0], n_hb, bb=32)
    x1_s, h2_s = _outproj_norm(xs, mix_s, wo, g2, tm=n_dec)
    y_s, upa_s, upb_s = _ffn_step(h2_s, x1_s, wu, wfc, wd, gf,
                                  state_ffnconv[0, :, 0, :], state_ffnconv[0, :, 1, :], tf=512)

    return (y_p.reshape(n_batch, seq, D_MODEL),
            y_s.reshape(n_dec, 1, D_MODEL),
            c_p[None],
            n_p.reshape(1, n_batch, N_HEADS, DK),
            m_p[:, :, 0, 0][None],
            sc_p[None],
            jnp.concatenate([_join_tail_blocks(fca_p), _join_tail_blocks(fcb_p)], axis=-1)[None],
            c_s[None],
            n_s.transpose(1, 0, 2)[None],
            m_s[:, :, 0].T[None],
            jnp.stack([state_shortconv[0, :, 1, :], cu_s], axis=1)[None],
            jnp.stack([state_ffnconv[0, :, 1, :], jnp.concatenate([upa_s, upb_s], axis=-1)], axis=1)[None])
```

```python
import functools

import jax
import jax.numpy as jnp
from jax import lax
from jax.experimental import pallas as pl
from jax.experimental.pallas import tpu as pltpu

D_MODEL = 2048
N_META = 16
N_HEADS = 8
DK = 128
DV = 256
D_CONV = D_MODEL
CONV_W = 3
D_FF = 5632
CHUNK = 128
EPS = 1e-6
K_SCALE = DK ** -0.5
NEG = -1e30
HEAD_COLS = 7 * DV + 2 * DK
V7X_VMEM_LIMIT = 56 * 1024 * 1024
HALO = 8
FFN_SUB = 64
F32 = jnp.float32
BF16 = jnp.bfloat16

_Q, _K, _V, _O, _GB, _U, _CG, _BG, _GA = (0, 128, 256, 512, 768, 1024, 1280, 1536, 1792)


def _params(n_grid):
    return pltpu.CompilerParams(dimension_semantics=("arbitrary",) * n_grid,
                                vmem_limit_bytes=V7X_VMEM_LIMIT)


def _log_sigmoid(x):
    return jnp.minimum(x, 0.0) - jnp.log1p(jnp.exp(-jnp.abs(x)))


def _rms(x, g):
    return x * lax.rsqrt(jnp.mean(x * x, axis=-1, keepdims=True) + EPS) * g


def _prep_win_kernel(u_ref, cg_ref, bg_ref, ga_ref, v_ref, o_ref, gb_ref, q_ref, k_ref, gif_ref,
                     wp_ref, wg_ref):
    for ref, off, d in ((u_ref, _U, DV), (cg_ref, _CG, DV), (bg_ref, _BG, DV), (ga_ref, _GA, DV),
                        (v_ref, _V, DV), (o_ref, _O, DV), (gb_ref, _GB, DV), (q_ref, _Q, DK), (k_ref, _K, DK)):
        wp_ref[:, off:off + d] = ref[...].T.astype(BF16)
    gif = gif_ref[...].T
    lane = lax.broadcasted_iota(jnp.int32, gif.shape, 1)
    wg_ref[...] = jnp.where(lane < 2 * N_HEADS, gif, 0.0).astype(BF16)


def _prep_win(wt, *, tk):
    o_q = 3 * D_CONV
    o_k = o_q + N_HEADS * DK
    o_v = o_k + N_HEADS * DK
    o_o = o_v + N_HEADS * DV
    o_gif = o_o + N_HEADS * DV
    o_ga = o_gif + 2 * N_HEADS
    o_gb = o_ga + D_CONV
    assert o_ga % 8 == 0
    rows = lambda off, d: pl.BlockSpec((pl.Element(d), pl.Element(tk)),
                                       lambda i, h: (pl.multiple_of(off + d * h, 8), pl.multiple_of(i * tk, 128)))
    return pl.pallas_call(
        _prep_win_kernel,
        grid=(D_MODEL // tk, N_HEADS),
        in_specs=[rows(0, DV), rows(D_CONV, DV), rows(2 * D_CONV, DV), rows(o_ga, DV), rows(o_v, DV),
                  rows(o_o, DV), rows(o_gb, DV), rows(o_q, DK), rows(o_k, DK),
                  pl.BlockSpec((pl.Element(128), pl.Element(tk)),
                               lambda i, h: (o_gif, pl.multiple_of(i * tk, 128)))],
        out_specs=[pl.BlockSpec((tk, HEAD_COLS), lambda i, h: (i, h)),
                   pl.BlockSpec((tk, 128), lambda i, h: (i, 0))],
        out_shape=[jax.ShapeDtypeStruct((D_MODEL, N_HEADS * HEAD_COLS), BF16),
                   jax.ShapeDtypeStruct((D_MODEL, 128), BF16)],
        compiler_params=_params(2),
        name="prep_win",
    )(*([wt] * 10))


def _norm_gates_seq_kernel(x_ref, g_ref, wg_ref, bif_ref, hn_ref, gates_ref, *, tm, n_valid):
    hn = _rms(x_ref[...], g_ref[...]).astype(BF16)
    hn_ref[...] = hn
    gif = jnp.dot(hn, wg_ref[...], preferred_element_type=F32) + bif_ref[...]
    row = lax.broadcasted_iota(jnp.int32, (CHUNK, CHUNK), 0)
    col = lax.broadcasted_iota(jnp.int32, (CHUNK, CHUNK), 1)
    upper = (row <= col).astype(F32)
    for c in range(tm // CHUNK):
        gt = gif[c * CHUNK:(c + 1) * CHUNK, :].T
        ig = gt[0:N_HEADS, :]
        lf = _log_sigmoid(gt[N_HEADS:2 * N_HEADS, :])
        if n_valid < CHUNK:
            tok = lax.broadcasted_iota(jnp.int32, (N_HEADS, CHUNK), 1)
            ig = jnp.where(tok < n_valid, ig, NEG)
            lf = jnp.where(tok < n_valid, lf, 0.0)
        b = jnp.dot(lf, upper, preferred_element_type=F32, precision=lax.Precision.HIGHEST)
        a = ig - b
        for j in range(N_HEADS):
            gates_ref[j, 0:1, c * CHUNK:(c + 1) * CHUNK] = a[j:j + 1, :]
            gates_ref[j, 1:2, c * CHUNK:(c + 1) * CHUNK] = b[j:j + 1, :]


def _norm_gates_seq(x, g, wg, bif, *, tm, n_valid):
    t = x.shape[0]
    return pl.pallas_call(
        functools.partial(_norm_gates_seq_kernel, tm=tm, n_valid=n_valid),
        grid=(t // tm,),
        in_specs=[pl.BlockSpec((tm, D_MODEL), lambda r: (r, 0)),
                  pl.BlockSpec((1, D_MODEL), lambda r: (0, 0)),
                  pl.BlockSpec((D_MODEL, 128), lambda r: (0, 0)),
                  pl.BlockSpec((1, 128), lambda r: (0, 0))],
        out_specs=[pl.BlockSpec((tm, D_MODEL), lambda r: (r, 0)),
                   pl.BlockSpec((N_HEADS, 2, tm), lambda r: (0, 0, r))],
        out_shape=[jax.ShapeDtypeStruct((t, D_MODEL), BF16),
                   jax.ShapeDtypeStruct((N_HEADS, 2, t), F32)],
        compiler_params=_params(1),
        name="norm_gates_seq",
    )(x, g, wg, bif)


def _norm_gates_step_kernel(x_ref, g_ref, wg_ref, bif_ref, hn_ref, gif_ref):
    hn = _rms(x_ref[...], g_ref[...]).astype(BF16)
    hn_ref[...] = hn
    gif_ref[...] = jnp.dot(hn, wg_ref[...], preferred_element_type=F32) + bif_ref[...]


def _norm_gates_step(x, g, wg, bif):
    t = x.shape[0]
    return pl.pallas_call(
        _norm_gates_step_kernel,
        grid=(1,),
        in_specs=[pl.BlockSpec((t, D_MODEL), lambda r: (0, 0)),
                  pl.BlockSpec((1, D_MODEL), lambda r: (0, 0)),
                  pl.BlockSpec((D_MODEL, 128), lambda r: (0, 0)),
                  pl.BlockSpec((1, 128), lambda r: (0, 0))],
        out_specs=[pl.BlockSpec((t, D_MODEL), lambda r: (0, 0)),
                   pl.BlockSpec((t, 128), lambda r: (0, 0))],
        out_shape=[jax.ShapeDtypeStruct((t, D_MODEL), BF16),
                   jax.ShapeDtypeStruct((t, 128), F32)],
        compiler_params=_params(1),
        name="norm_gates_step",
    )(x, g, wg, bif)


def _mixers_kernel(*refs, tm, tiles_per_batch, n_valid, step_bb):
    n_seq = (8, 5, 11)
    n_step = (7, 5, 5) if step_bb else (0, 0, 0)
    cuts = [0]
    for a, b in zip(n_seq, n_step):
        cuts += [cuts[-1] + a, cuts[-1] + a + b]
    seq_in, step_in, seq_out, step_out, seq_scr, step_scr = (refs[i:j] for i, j in zip(cuts[:-1], cuts[1:]))
    hn_ref, w_ref, gates_ref, wsc_ref, c0_ref, n0_ref, m0_ref, sc0_ref = seq_in
    mix_ref, cout_ref, nout_ref, mout_ref, scout_ref = seq_out
    z_ref, cu_ref, c_ref, n_ref, m_ref, dmat_ref, qk_ref, sv_ref, kv_ref, bmix_ref, amix_ref = seq_scr
    head = pl.program_id(0)
    tile = pl.program_id(1)
    rb = tile % tiles_per_batch
    nc = tm // CHUNK

    @pl.when(rb == 0)
    def _():
        c_ref[...] = c0_ref[...]
        n_ref[...] = n0_ref[...]
        m_ref[...] = m0_ref[...]
        cu_ref[0:HALO, :] = sc0_ref[...]

    row = lax.broadcasted_iota(jnp.int32, (CHUNK, CHUNK), 0)
    col = lax.broadcasted_iota(jnp.int32, (CHUNK, CHUNK), 1)
    causal = col <= row
    eye = col == row
    st = [dict() for _ in range(nc)]
    rows_of = lambda c: slice(c * CHUNK, (c + 1) * CHUNK)
    k_of = lambda rows: z_ref[rows, _K:_K + DK] * K_SCALE

    def proj(j):
        cols = slice(j * DV, (j + 1) * DV)
        z_ref[:, cols] = jnp.dot(hn_ref[...], w_ref[:, cols], preferred_element_type=F32)

    def gate_terms(c):
        rows, d = rows_of(c), st[c]
        a_row = gates_ref[0:1, rows]
        b_row = gates_ref[1:2, rows]
        d["a_col"] = jnp.sum(jnp.where(eye, a_row, 0.0), axis=-1, keepdims=True)
        d["b_col"] = jnp.sum(jnp.where(eye, b_row, 0.0), axis=-1, keepdims=True)
        dmat = jnp.where(causal, d["b_col"] + a_row, NEG)
        dmat_ref[rows, :] = dmat
        d["mloc"] = jnp.max(dmat, axis=-1, keepdims=True)
        qk_ref[rows, :] = lax.dot_general(z_ref[rows, _Q:_Q + DK].astype(BF16), k_of(rows).astype(BF16),
                                          (((1,), (1,)), ((), ())), preferred_element_type=F32)

    def stabilisers():
        m_prev = m_ref[:, 0:1]
        for c in range(nc):
            d = st[c]
            d["m_prev"] = m_prev
            d["bm"] = d["b_col"] + m_prev
            d["m_t"] = jnp.maximum(d["bm"], d["mloc"])
            d["m_last"] = d["m_t"][CHUNK - 1:CHUNK, :]
            m_prev = d["m_last"]
        m_ref[...] = jnp.broadcast_to(m_prev, (1, DK))

    def intra(c):
        rows, d = rows_of(c), st[c]
        k = k_of(rows)
        v = z_ref[rows, _V:_V + DV]
        s = qk_ref[rows, :] * jnp.exp(dmat_ref[rows, :] - d["m_t"])
        sv_ref[rows, :] = jnp.dot(s.astype(BF16), v.astype(BF16), preferred_element_type=F32)
        d["ssum"] = jnp.sum(s, axis=-1, keepdims=True)
        d["decay"] = jnp.exp(d["bm"] - d["m_t"])
        d["hden"] = jnp.exp(-d["m_t"])
        b_last = d["b_col"][CHUNK - 1:CHUNK, :]
        g_col = jnp.exp(b_last + d["a_col"] - d["m_last"])
        d["carry"] = jnp.exp(b_last + d["m_prev"] - d["m_last"])
        kv_ref[c] = jnp.dot(k.T.astype(BF16), (g_col * v).astype(BF16), preferred_element_type=F32)
        d["kn"] = jnp.sum(g_col * k, axis=0, keepdims=True)

    def inter(c):
        rows, d = rows_of(c), st[c]
        q = z_ref[rows, _Q:_Q + DK]
        c_prev = c_ref[...]
        n_prev = n_ref[...]
        num = (d["decay"] * jnp.dot(q.astype(BF16), c_prev.astype(BF16), preferred_element_type=F32)
               + sv_ref[rows, :])
        den = d["decay"] * jnp.sum(q * n_prev, axis=-1, keepdims=True) + d["ssum"]
        hb = num / jnp.maximum(jnp.abs(den), d["hden"])
        b_out = jax.nn.sigmoid(z_ref[rows, _O:_O + DV]) * hb
        bmix_ref[rows, :] = jax.nn.sigmoid(z_ref[rows, _GB:_GB + DV]) * b_out
        c_ref[...] = d["carry"] * c_prev + kv_ref[c]
        n_ref[...] = d["carry"] * n_prev + d["kn"]

    def conv(c):
        lo = c * CHUNK
        rows = rows_of(c)
        cu = z_ref[rows, _CG:_CG + DV] * z_ref[rows, _U:_U + DV]
        cu_ref[HALO + lo:HALO + lo + CHUNK, :] = cu
        taps = (cu * wsc_ref[2:3, :] + cu_ref[HALO - 1 + lo:HALO - 1 + lo + CHUNK, :] * wsc_ref[1:2, :]
                + cu_ref[HALO - 2 + lo:HALO - 2 + lo + CHUNK, :] * wsc_ref[0:1, :])
        amix_ref[rows, :] = z_ref[rows, _BG:_BG + DV] * taps

    def merge(c):
        rows = rows_of(c)
        mix = jax.nn.sigmoid(z_ref[rows, _GA:_GA + DV]) * amix_ref[rows, :] + bmix_ref[rows, :]
        mix_ref[rows, :] = mix.astype(BF16)

    piece = lambda off: off // DV
    assert (piece(_Q), piece(_V), piece(_O), piece(_GB)) == (0, 1, 2, 3)
    assert sorted(map(piece, (_U, _CG, _BG))) == [4, 5, 6] and piece(_GA) == HEAD_COLS // DV - 1
    each = lambda fn: [functools.partial(fn, c) for c in range(nc)]
    half = (nc + 1) // 2
    after = {0: [],
             1: each(gate_terms)[:half],
             2: each(gate_terms)[half:] + [stabilisers],
             3: each(intra)[:half],
             4: each(intra)[half:] + each(inter)[:1],
             5: each(inter)[1:1 + half],
             6: each(inter)[1 + half:] + each(conv),
             7: each(merge)}

    if step_bb:
        bb = step_bb
        hns_ref, gifs_ref, mprevs_ref, s0_ref, s1_ref, cs_ref, ns_ref = step_in
        mixs_ref, csout_ref, nsout_ref, msout_ref, cuouts_ref = step_out
        zs_ref, aouts_ref, qts_ref, kts_ref, hbs_ref = step_scr

        @pl.when(tile == 0)
        def _():
            z = jnp.dot(hns_ref[...], w_ref[...], preferred_element_type=F32)
            zs_ref[...] = z
            cu = z[:, _CG:_CG + DV] * z[:, _U:_U + DV]
            taps = s0_ref[...] * wsc_ref[0:1, :] + s1_ref[...] * wsc_ref[1:2, :] + cu * wsc_ref[2:3, :]
            aouts_ref[...] = z[:, _BG:_BG + DV] * taps
            cuouts_ref[...] = cu
            qt = z[:, _Q:_Q + DK].T
            kt = (z[:, _K:_K + DK] * K_SCALE).T
            for jj in range(hns_ref.shape[0] // bb):
                qts_ref[jj] = qt[:, jj * bb:(jj + 1) * bb]
                kts_ref[jj] = kt[:, jj * bb:(jj + 1) * bb]

        srows = pl.ds(pl.multiple_of(tile * bb, bb), bb)
        sd = {}

        def step_gates():
            lane = lax.broadcasted_iota(jnp.int32, (bb, 128), 1)
            pick = lambda x, idx: jnp.sum(jnp.where(lane == idx, x, 0.0), axis=-1, keepdims=True)
            gif = gifs_ref[srows, :]
            ig = pick(gif, head)
            lf = _log_sigmoid(pick(gif, head + N_HEADS))
            m_prev = pick(mprevs_ref[srows, :], head)
            q = zs_ref[srows, _Q:_Q + DK]
            k = zs_ref[srows, _K:_K + DK] * K_SCALE
            n_prev = ns_ref[...]
            m_t = jnp.maximum(lf + m_prev, ig)
            sd["w"] = jnp.exp(ig - m_t)
            sd["decay"] = jnp.exp(lf + m_prev - m_t)
            sd["s"] = jnp.sum(q * k, axis=-1, keepdims=True) * sd["w"]
            den = sd["decay"] * jnp.sum(q * n_prev, axis=-1, keepdims=True) + sd["s"]
            sd["inv"] = 1.0 / jnp.maximum(jnp.abs(den), jnp.exp(-m_t))
            nsout_ref[...] = sd["decay"] * n_prev + sd["w"] * k
            msout_ref[...] = jnp.broadcast_to(m_t, (bb, DK))

        def step_rows(lo, hi):
            qt = qts_ref[tile]
            kt = kts_ref[tile]
            v = zs_ref[srows, _V:_V + DV]
            for i in range(lo, hi):
                c_prev = cs_ref[i]
                d_i = sd["decay"][i:i + 1, :]
                v_i = v[i:i + 1, :]
                qc = jnp.sum(qt[:, i:i + 1] * c_prev, axis=0, keepdims=True)
                hbs_ref[i:i + 1, :] = (d_i * qc + sd["s"][i:i + 1, :] * v_i) * sd["inv"][i:i + 1, :]
                csout_ref[i] = d_i * c_prev + (sd["w"][i:i + 1, :] * kt[:, i:i + 1]) * v_i

        def step_merge():
            b_out = jax.nn.sigmoid(zs_ref[srows, _O:_O + DV]) * hbs_ref[...]
            mixs_ref[...] = (jax.nn.sigmoid(zs_ref[srows, _GA:_GA + DV]) * aouts_ref[srows, :]
                             + jax.nn.sigmoid(zs_ref[srows, _GB:_GB + DV]) * b_out)

        after[0] += [step_gates, functools.partial(step_rows, 0, bb // 2)]
        after[5] += [functools.partial(step_rows, bb // 2, bb)]
        after[6] = [step_merge] + after[6]

    for j in range(HEAD_COLS // DV):
        proj(j)
        for fn in after.get(j, []):
            fn()

    @pl.when(rb == tiles_per_batch - 1)
    def _():
        cout_ref[...] = c_ref[...]
        nout_ref[...] = n_ref[...]
        mout_ref[...] = m_ref[...]
        scout_ref[...] = cu_ref[HALO + n_valid - 2:HALO + n_valid, :]

    cu_ref[0:HALO, :] = cu_ref[tm:tm + HALO, :]


def _mixers(hn, wp, gates, wsc, c0, n0, m0, sc0, step=None, *, n_batch, tm, n_valid):
    t = hn.shape[0]
    n_tiles = t // tm
    tpb = n_tiles // n_batch
    in_specs = [pl.BlockSpec((tm, D_MODEL), lambda h, r: (r, 0)),
                pl.BlockSpec((D_MODEL, HEAD_COLS), lambda h, r: (0, h)),
                pl.BlockSpec((None, 2, tm), lambda h, r: (h, 0, r)),
                pl.BlockSpec((CONV_W, DV), lambda h, r: (0, h)),
                pl.BlockSpec((None, DK, DV), lambda h, r: (h, 0, 0)),
                pl.BlockSpec((None, 1, DK), lambda h, r: (h, 0, 0)),
                pl.BlockSpec((None, 1, DK), lambda h, r: (h, 0, 0)),
                pl.BlockSpec((HALO, DV), lambda h, r: (0, h))]
    out_specs = [pl.BlockSpec((tm, DV), lambda h, r: (r, h)),
                 pl.BlockSpec((None, None, DK, DV), lambda h, r: (r // tpb, h, 0, 0)),
                 pl.BlockSpec((None, None, 1, DK), lambda h, r: (r // tpb, h, 0, 0)),
                 pl.BlockSpec((None, None, 1, DK), lambda h, r: (r // tpb, h, 0, 0)),
                 pl.BlockSpec((None, 2, DV), lambda h, r: (r // tpb, 0, h))]
    out_shape = [jax.ShapeDtypeStruct((t, D_MODEL), BF16),
                 jax.ShapeDtypeStruct((n_batch, N_HEADS, DK, DV), F32),
                 jax.ShapeDtypeStruct((n_batch, N_HEADS, 1, DK), F32),
                 jax.ShapeDtypeStruct((n_batch, N_HEADS, 1, DK), F32),
                 jax.ShapeDtypeStruct((n_batch, 2, D_CONV), F32)]
    scratch = [pltpu.VMEM((tm, HEAD_COLS), F32),
               pltpu.VMEM((tm + HALO, DV), F32),
               pltpu.VMEM((DK, DV), F32),
               pltpu.VMEM((1, DK), F32),
               pltpu.VMEM((1, DK), F32),
               pltpu.VMEM((tm, CHUNK), F32),
               pltpu.VMEM((tm, CHUNK), F32),
               pltpu.VMEM((tm, DV), F32),
               pltpu.VMEM((tm // CHUNK, DK, DV), F32),
               pltpu.VMEM((tm, DV), F32),
               pltpu.VMEM((tm, DV), F32)]
    step_bb = 0
    if step is not None:
        nb = step[0].shape[0]
        assert nb % n_tiles == 0
        step_bb = nb // n_tiles
        in_specs += [pl.BlockSpec((nb, D_MODEL), lambda h, r: (0, 0)),
                     pl.BlockSpec((nb, 128), lambda h, r: (0, 0)),
                     pl.BlockSpec((nb, 128), lambda h, r: (0, 0)),
                     pl.BlockSpec((nb, DV), lambda h, r: (0, h)),
                     pl.BlockSpec((nb, DV), lambda h, r: (0, h)),
                     pl.BlockSpec((step_bb, None, DK, DV), lambda h, r: (r, h, 0, 0)),
                     pl.BlockSpec((None, step_bb, DK), lambda h, r: (h, r, 0))]
        out_specs += [pl.BlockSpec((step_bb, DV), lambda h, r: (r, h)),
                      pl.BlockSpec((step_bb, None, DK, DV), lambda h, r: (r, h, 0, 0)),
                      pl.BlockSpec((None, step_bb, DK), lambda h, r: (h, r, 0)),
                      pl.BlockSpec((None, step_bb, DK), lambda h, r: (h, r, 0)),
                      pl.BlockSpec((nb, DV), lambda h, r: (0, h))]
        out_shape += [jax.ShapeDtypeStruct((nb, D_MODEL), F32),
                      jax.ShapeDtypeStruct((nb, N_HEADS, DK, DV), F32),
                      jax.ShapeDtypeStruct((N_HEADS, nb, DK), F32),
                      jax.ShapeDtypeStruct((N_HEADS, nb, DK), F32),
                      jax.ShapeDtypeStruct((nb, D_CONV), F32)]
        scratch += [pltpu.VMEM((nb, HEAD_COLS), F32),
                    pltpu.VMEM((nb, DV), F32),
                    pltpu.VMEM((n_tiles, DK, step_bb), F32),
                    pltpu.VMEM((n_tiles, DK, step_bb), F32),
                    pltpu.VMEM((step_bb, DV), F32)]
    return pl.pallas_call(
        functools.partial(_mixers_kernel, tm=tm, tiles_per_batch=tpb, n_valid=n_valid, step_bb=step_bb),
        grid=(N_HEADS, n_tiles),
        in_specs=in_specs,
        out_specs=out_specs,
        out_shape=out_shape,
        scratch_shapes=scratch,
        compiler_params=_params(2),
        name="mixers",
    )(hn, wp, gates, wsc, c0, n0, m0, sc0, *(step or ()))


def _outproj_norm_kernel(x_ref, mix_ref, wo_ref, g_ref, x1_ref, h2_ref):
    x1 = x_ref[...] + jnp.dot(mix_ref[...], wo_ref[...], preferred_element_type=F32)
    x1_ref[...] = x1
    h2_ref[...] = _rms(x1, g_ref[...]).astype(BF16)


def _outproj_norm(x, mix, wo, g, *, tm):
    t = x.shape[0]
    return pl.pallas_call(
        _outproj_norm_kernel,
        grid=(t // tm,),
        in_specs=[pl.BlockSpec((tm, D_MODEL), lambda r: (r, 0)),
                  pl.BlockSpec((tm, D_MODEL), lambda r: (r, 0)),
                  pl.BlockSpec((D_MODEL, D_MODEL), lambda r: (0, 0)),
                  pl.BlockSpec((1, D_MODEL), lambda r: (0, 0))],
        out_specs=[pl.BlockSpec((tm, D_MODEL), lambda r: (r, 0)),
                   pl.BlockSpec((tm, D_MODEL), lambda r: (r, 0))],
        out_shape=[jax.ShapeDtypeStruct((t, D_MODEL), F32),
                   jax.ShapeDtypeStruct((t, D_MODEL), BF16)],
        compiler_params=_params(1),
        name="outproj_norm",
    )(x, mix, wo, g)


def _ffn_seq_kernel(h2_ref, x1_ref, wua_ref, wub_ref, wca_ref, wcb_ref, wd_ref, g_ref, ha0_ref, hb0_ref,
                    y_ref, fca_ref, fcb_ref,
                    acc_ref, ua_ref, ub_ref, ha_ref, hb_ref, act_ref, *, tm, tiles_per_batch, n_valid, n_f):
    r = pl.program_id(0)
    f = pl.program_id(1)
    rb = r % tiles_per_batch

    @pl.when((r == 0) & (f == 0))
    def _():
        ha_ref[...] = jnp.zeros_like(ha_ref)
        hb_ref[...] = jnp.zeros_like(hb_ref)

    @pl.when(f == 0)
    def _():
        acc_ref[...] = jnp.zeros_like(acc_ref)

    def up_proj(wu_ref, u_ref, halo_ref, halo0_ref, tail_ref):
        u_ref[HALO:tm + HALO, :] = jnp.dot(h2_ref[...], wu_ref[...], preferred_element_type=F32)
        u_ref[0:HALO, :] = jnp.where(rb == 0, halo0_ref[...], halo_ref[f])
        halo_ref[f] = u_ref[tm:tm + HALO, :]
        tail_ref[f] = u_ref[HALO + n_valid - 2:HALO + n_valid, :]

    up_proj(wua_ref, ua_ref, ha_ref, ha0_ref, fca_ref)
    up_proj(wub_ref, ub_ref, hb_ref, hb0_ref, fcb_ref)

    def conv(u_ref, wc_ref, lo):
        return (u_ref[HALO + lo:HALO + lo + FFN_SUB, :] * wc_ref[2:3, :]
                + u_ref[HALO - 1 + lo:HALO - 1 + lo + FFN_SUB, :] * wc_ref[1:2, :]
                + u_ref[HALO - 2 + lo:HALO - 2 + lo + FFN_SUB, :] * wc_ref[0:1, :])

    for sb in range(tm // FFN_SUB):
        lo = sb * FFN_SUB
        up_a = conv(ua_ref, wca_ref, lo)
        up_b = conv(ub_ref, wcb_ref, lo)
        act_ref[lo:lo + FFN_SUB, :] = (up_b * jax.nn.sigmoid(up_b) * up_a).astype(BF16)
    acc_ref[...] += jnp.dot(act_ref[...], wd_ref[...], preferred_element_type=F32)

    @pl.when(f == n_f - 1)
    def _():
        y_ref[...] = _rms(x1_ref[...] + acc_ref[...], g_ref[...])


def _ffn_seq(h2, x1, wu, wc, wd, g, halo_a, halo_b, *, n_batch, tm, tf, n_valid):
    t = h2.shape[0]
    tpb = t // n_batch // tm
    n_f = D_FF // tf
    return pl.pallas_call(
        functools.partial(_ffn_seq_kernel, tm=tm, tiles_per_batch=tpb, n_valid=n_valid, n_f=n_f),
        grid=(t // tm, n_f),
        in_specs=[pl.BlockSpec((tm, D_MODEL), lambda r, f: (r, 0)),
                  pl.BlockSpec((tm, D_MODEL), lambda r, f: (r, 0)),
                  pl.BlockSpec((D_MODEL, tf), lambda r, f: (0, f)),
                  pl.BlockSpec((D_MODEL, tf), lambda r, f: (0, n_f + f)),
                  pl.BlockSpec((CONV_W, tf), lambda r, f: (0, f)),
                  pl.BlockSpec((CONV_W, tf), lambda r, f: (0, n_f + f)),
                  pl.BlockSpec((tf, D_MODEL), lambda r, f: (f, 0)),
                  pl.BlockSpec((1, D_MODEL), lambda r, f: (0, 0)),
                  pl.BlockSpec((HALO, tf), lambda r, f: (0, f)),
                  pl.BlockSpec((HALO, tf), lambda r, f: (0, f))],
        out_specs=[pl.BlockSpec((tm, D_MODEL), lambda r, f: (r, 0)),
                   pl.BlockSpec((None, n_f, 2, tf), lambda r, f: (r // tpb, 0, 0, 0)),
                   pl.BlockSpec((None, n_f, 2, tf), lambda r, f: (r // tpb, 0, 0, 0))],
        out_shape=[jax.ShapeDtypeStruct((t, D_MODEL), F32),
                   jax.ShapeDtypeStruct((n_batch, n_f, 2, tf), F32),
                   jax.ShapeDtypeStruct((n_batch, n_f, 2, tf), F32)],
        scratch_shapes=[pltpu.VMEM((tm, D_MODEL), F32),
                        pltpu.VMEM((tm + HALO, tf), F32),
                        pltpu.VMEM((tm + HALO, tf), F32),
                        pltpu.VMEM((n_f, HALO, tf), F32),
                        pltpu.VMEM((n_f, HALO, tf), F32),
                        pltpu.VMEM((tm, tf), BF16)],
        compiler_params=_params(2),
        name="ffn_seq",
    )(h2, x1, wu, wu, wc, wc, wd, g, halo_a, halo_b)


def _ffn_step_kernel(h2_ref, x1_ref, wua_ref, wub_ref, wca_ref, wcb_ref, wd_ref, g_ref,
                     s0a_ref, s1a_ref, s0b_ref, s1b_ref,
                     y_ref, upa_ref, upb_ref, acc_ref, *, n_f):
    f = pl.program_id(0)

    @pl.when(f == 0)
    def _():
        acc_ref[...] = jnp.zeros_like(acc_ref)

    def half(wu_ref, wc_ref, s0_ref, s1_ref, up_ref):
        up = jnp.dot(h2_ref[...], wu_ref[...], preferred_element_type=F32)
        up_ref[...] = up
        return s0_ref[...] * wc_ref[0:1, :] + s1_ref[...] * wc_ref[1:2, :] + up * wc_ref[2:3, :]

    up_a = half(wua_ref, wca_ref, s0a_ref, s1a_ref, upa_ref)
    up_b = half(wub_ref, wcb_ref, s0b_ref, s1b_ref, upb_ref)
    act = (up_b * jax.nn.sigmoid(up_b) * up_a).astype(BF16)
    acc_ref[...] += jnp.dot(act, wd_ref[...], preferred_element_type=F32)

    @pl.when(f == n_f - 1)
    def _():
        y_ref[...] = _rms(x1_ref[...] + acc_ref[...], g_ref[...])


def _ffn_step(h2, x1, wu, wc, wd, g, s0, s1, *, tf):
    nb = h2.shape[0]
    n_f = D_FF // tf
    lo = lambda f: (0, f)
    hi = lambda f: (0, n_f + f)
    return pl.pallas_call(
        functools.partial(_ffn_step_kernel, n_f=n_f),
        grid=(n_f,),
        in_specs=[pl.BlockSpec((nb, D_MODEL), lambda f: (0, 0)),
                  pl.BlockSpec((nb, D_MODEL), lambda f: (0, 0)),
                  pl.BlockSpec((D_MODEL, tf), lo),
                  pl.BlockSpec((D_MODEL, tf), hi),
                  pl.BlockSpec((CONV_W, tf), lo),
                  pl.BlockSpec((CONV_W, tf), hi),
                  pl.BlockSpec((tf, D_MODEL), lambda f: (f, 0)),
                  pl.BlockSpec((1, D_MODEL), lambda f: (0, 0)),
                  pl.BlockSpec((nb, tf), lo),
                  pl.BlockSpec((nb, tf), lo),
                  pl.BlockSpec((nb, tf), hi),
                  pl.BlockSpec((nb, tf), hi)],
        out_specs=[pl.BlockSpec((nb, D_MODEL), lambda f: (0, 0)),
                   pl.BlockSpec((nb, tf), lo),
                   pl.BlockSpec((nb, tf), lo)],
        out_shape=[jax.ShapeDtypeStruct((nb, D_MODEL), F32),
                   jax.ShapeDtypeStruct((nb, D_FF), F32),
                   jax.ShapeDtypeStruct((nb, D_FF), F32)],
        scratch_shapes=[pltpu.VMEM((nb, D_MODEL), F32)],
        compiler_params=_params(1),
        name="ffn_step",
    )(h2, x1, wu, wu, wc, wc, wd, g, s0, s1, s0, s1)


def _tail_to_halo(tail):
    return jnp.pad(tail, ((HALO - 2, 0), (0, 0)))


def _join_tail_blocks(t):
    b, n_f, _, tf = t.shape
    return t.transpose(0, 2, 1, 3).reshape(b, 2, n_f * tf)


def kernel(x_prompt, x_sample, state_mlstm_C, state_mlstm_n, state_mlstm_m, state_shortconv,
           state_ffnconv, meta_tokens, norm1_g, w_in, b_if, w_shortconv, w_out, norm2_g,
           w_up, w_ffconv, w_down, norm_f_g):
    assert w_in.shape[0] == 1, "single-layer trunk"
    n_batch, seq, _ = x_prompt.shape
    n_dec = x_sample.shape[0]

    wp, wg = _prep_win(w_in[0].T, tk=1024)
    bif = jnp.pad(b_if[0], (0, 128 - 2 * N_HEADS)).reshape(1, 128)
    wo = w_out[0].astype(BF16)
    wu = w_up[0].astype(BF16)
    wd = w_down[0].astype(BF16)
    wsc = w_shortconv[0]
    wfc = w_ffconv[0]
    g1 = norm1_g[0].reshape(1, D_MODEL)
    g2 = norm2_g[0].reshape(1, D_MODEL)
    gf = norm_f_g.reshape(1, D_MODEL)

    xm = jnp.pad(meta_tokens, ((0, CHUNK - N_META), (0, 0)))
    hn_m, gates_m = _norm_gates_seq(xm, g1, wg, bif, tm=CHUNK, n_valid=N_META)
    zc = jnp.zeros((N_HEADS, DK, DV), F32)
    zn = jnp.zeros((N_HEADS, 1, DK), F32)
    mix_m, c_m, n_m, m_m, sc_m = _mixers(
        hn_m, wp, gates_m, wsc, zc, zn, zn, jnp.zeros((HALO, D_CONV), F32),
        n_batch=1, tm=CHUNK, n_valid=N_META)
    x1_m, h2_m = _outproj_norm(xm, mix_m, wo, g2, tm=CHUNK)
    zh = jnp.zeros((HALO, D_FF), F32)
    _, fca_m, fcb_m = _ffn_seq(h2_m, x1_m, wu, wfc, wd, gf, zh, zh,
                               n_batch=1, tm=CHUNK, tf=512, n_valid=N_META)
    fca_m = _join_tail_blocks(fca_m)
    fcb_m = _join_tail_blocks(fcb_m)

    xp = x_prompt.reshape(n_batch * seq, D_MODEL)
    tm = 512
    hn_p, gates_p = _norm_gates_seq(xp, g1, wg, bif, tm=tm, n_valid=CHUNK)
    xs = x_sample.reshape(n_dec, D_MODEL)
    hn_s, gif_s = _norm_gates_step(xs, g1, wg, bif)
    mprev = jnp.pad(state_mlstm_m[0], ((0, 0), (0, 128 - N_HEADS)))
    n_hb = state_mlstm_n[0].transpose(1, 0, 2)
    mix_p, c_p, n_p, m_p, sc_p, mix_s, c_s, n_s, m_s, cu_s = _mixers(
        hn_p, wp, gates_p, wsc, c_m[0], n_m[0], m_m[0], _tail_to_halo(sc_m[0]),
        (hn_s, gif_s, mprev, state_shortconv[0, :, 0, :], state_shortconv[0, :, 1, :],
         state_mlstm_C[0], n_hb),
        n_batch=n_batch, tm=tm, n_valid=tm)
    x1_p, h2_p = _outproj_norm(xp, mix_p, wo, g2, tm=tm)
    y_p, fca_p, fcb_p = _ffn_seq(h2_p, x1_p, wu, wfc, wd, gf,
                                 _tail_to_halo(fca_m[0]), _tail_to_halo(fcb_m[0]),
                                 n_batch=n_batch, tm=tm, tf=512, n_valid=tm)

    x1_s, h2_s = _outproj_norm(xs, mix_s.astype(BF16), wo, g2, tm=n_dec)
    y_s, upa_s, upb_s = _ffn_step(h2_s, x1_s, wu, wfc, wd, gf,
                                  state_ffnconv[0, :, 0, :], state_ffnconv[0, :, 1, :], tf=512)

    return (y_p.reshape(n_batch, seq, D_MODEL),
            y_s.reshape(n_dec, 1, D_MODEL),
            c_p[None],
            n_p.reshape(1, n_batch, N_HEADS, DK),
            m_p[:, :, 0, 0][None],
            sc_p[None],
            jnp.concatenate([_join_tail_blocks(fca_p), _join_tail_blocks(fcb_p)], axis=-1)[None],
            c_s[None],
            n_s.transpose(1, 0, 2)[None],
            m_s[:, :, 0].T[None],
            jnp.stack([state_shortconv[0, :, 1, :], cu_s], axis=1)[None],
            jnp.stack([state_ffnconv[0, :, 1, :], jnp.concatenate([upa_s, upb_s], axis=-1)], axis=1)[None])
```

```python
import functools

import jax
import jax.numpy as jnp
from jax import lax
from jax.experimental import pallas as pl
from jax.experimental.pallas import tpu as pltpu

D_MODEL = 2048
N_META = 16
N_HEADS = 8
DK = 128
DV = 256
D_CONV = D_MODEL
CONV_W = 3
D_FF = 5632
CHUNK = 128
EPS = 1e-6
K_SCALE = DK ** -0.5
NEG = -1e30
HEAD_COLS = 7 * DV + 2 * DK
V7X_VMEM_LIMIT = 56 * 1024 * 1024
HALO = 8
FFN_SUB = 64
FFN_TF = 512
F32 = jnp.float32
BF16 = jnp.bfloat16

_Q, _K, _V, _O, _GB, _U, _CG, _BG, _GA = (0, 128, 256, 512, 768, 1024, 1280, 1536, 1792)


def _params(n_grid):
    return pltpu.CompilerParams(dimension_semantics=("arbitrary",) * n_grid,
                                vmem_limit_bytes=V7X_VMEM_LIMIT)


def _log_sigmoid(x):
    return jnp.minimum(x, 0.0) - jnp.log1p(jnp.exp(-jnp.abs(x)))


def _rms(x, g):
    return x * lax.rsqrt(jnp.mean(x * x, axis=-1, keepdims=True) + EPS) * g


def _prep_win_kernel(u_ref, cg_ref, bg_ref, ga_ref, v_ref, o_ref, gb_ref, q_ref, k_ref, gif_ref,
                     wp_ref, wg_ref):
    for ref, off, d in ((u_ref, _U, DV), (cg_ref, _CG, DV), (bg_ref, _BG, DV), (ga_ref, _GA, DV),
                        (v_ref, _V, DV), (o_ref, _O, DV), (gb_ref, _GB, DV), (q_ref, _Q, DK), (k_ref, _K, DK)):
        wp_ref[:, off:off + d] = ref[...].T.astype(BF16)
    gif = gif_ref[...].T
    lane = lax.broadcasted_iota(jnp.int32, gif.shape, 1)
    wg_ref[...] = jnp.where(lane < 2 * N_HEADS, gif, 0.0).astype(BF16)


def _prep_win(wt, *, tk):
    o_q = 3 * D_CONV
    o_k = o_q + N_HEADS * DK
    o_v = o_k + N_HEADS * DK
    o_o = o_v + N_HEADS * DV
    o_gif = o_o + N_HEADS * DV
    o_ga = o_gif + 2 * N_HEADS
    o_gb = o_ga + D_CONV
    assert o_ga % 8 == 0
    rows = lambda off, d: pl.BlockSpec((pl.Element(d), pl.Element(tk)),
                                       lambda i, h: (pl.multiple_of(off + d * h, 8), pl.multiple_of(i * tk, 128)))
    return pl.pallas_call(
        _prep_win_kernel,
        grid=(D_MODEL // tk, N_HEADS),
        in_specs=[rows(0, DV), rows(D_CONV, DV), rows(2 * D_CONV, DV), rows(o_ga, DV), rows(o_v, DV),
                  rows(o_o, DV), rows(o_gb, DV), rows(o_q, DK), rows(o_k, DK),
                  pl.BlockSpec((pl.Element(128), pl.Element(tk)),
                               lambda i, h: (o_gif, pl.multiple_of(i * tk, 128)))],
        out_specs=[pl.BlockSpec((tk, HEAD_COLS), lambda i, h: (i, h)),
                   pl.BlockSpec((tk, 128), lambda i, h: (i, 0))],
        out_shape=[jax.ShapeDtypeStruct((D_MODEL, N_HEADS * HEAD_COLS), BF16),
                   jax.ShapeDtypeStruct((D_MODEL, 128), BF16)],
        compiler_params=_params(2),
        name="prep_win",
    )(*([wt] * 10))


def _norm_gates_seq_kernel(x_ref, g_ref, wg_ref, bif_ref, hn_ref, gates_ref, *, tm, n_valid):
    hn = _rms(x_ref[...], g_ref[...]).astype(BF16)
    hn_ref[...] = hn
    gif = jnp.dot(hn, wg_ref[...], preferred_element_type=F32) + bif_ref[...]
    row = lax.broadcasted_iota(jnp.int32, (CHUNK, CHUNK), 0)
    col = lax.broadcasted_iota(jnp.int32, (CHUNK, CHUNK), 1)
    upper = (row <= col).astype(F32)
    for c in range(tm // CHUNK):
        gt = gif[c * CHUNK:(c + 1) * CHUNK, :].T
        ig = gt[0:N_HEADS, :]
        lf = _log_sigmoid(gt[N_HEADS:2 * N_HEADS, :])
        if n_valid < CHUNK:
            tok = lax.broadcasted_iota(jnp.int32, (N_HEADS, CHUNK), 1)
            ig = jnp.where(tok < n_valid, ig, NEG)
            lf = jnp.where(tok < n_valid, lf, 0.0)
        b = jnp.dot(lf, upper, preferred_element_type=F32, precision=lax.Precision.HIGHEST)
        a = ig - b
        for j in range(N_HEADS):
            gates_ref[j, 0:1, c * CHUNK:(c + 1) * CHUNK] = a[j:j + 1, :]
            gates_ref[j, 1:2, c * CHUNK:(c + 1) * CHUNK] = b[j:j + 1, :]


def _norm_gates_seq(x, g, wg, bif, *, tm, n_valid):
    t = x.shape[0]
    return pl.pallas_call(
        functools.partial(_norm_gates_seq_kernel, tm=tm, n_valid=n_valid),
        grid=(t // tm,),
        in_specs=[pl.BlockSpec((tm, D_MODEL), lambda r: (r, 0)),
                  pl.BlockSpec((1, D_MODEL), lambda r: (0, 0)),
                  pl.BlockSpec((D_MODEL, 128), lambda r: (0, 0)),
                  pl.BlockSpec((1, 128), lambda r: (0, 0))],
        out_specs=[pl.BlockSpec((tm, D_MODEL), lambda r: (r, 0)),
                   pl.BlockSpec((N_HEADS, 2, tm), lambda r: (0, 0, r))],
        out_shape=[jax.ShapeDtypeStruct((t, D_MODEL), BF16),
                   jax.ShapeDtypeStruct((N_HEADS, 2, t), F32)],
        compiler_params=_params(1),
        name="norm_gates_seq",
    )(x, g, wg, bif)


def _norm_gates_step_kernel(x_ref, g_ref, wg_ref, bif_ref, hn_ref, gif_ref):
    hn = _rms(x_ref[...], g_ref[...]).astype(BF16)
    hn_ref[...] = hn
    gif_ref[...] = jnp.dot(hn, wg_ref[...], preferred_element_type=F32) + bif_ref[...]


def _norm_gates_step(x, g, wg, bif):
    t = x.shape[0]
    return pl.pallas_call(
        _norm_gates_step_kernel,
        grid=(1,),
        in_specs=[pl.BlockSpec((t, D_MODEL), lambda r: (0, 0)),
                  pl.BlockSpec((1, D_MODEL), lambda r: (0, 0)),
                  pl.BlockSpec((D_MODEL, 128), lambda r: (0, 0)),
                  pl.BlockSpec((1, 128), lambda r: (0, 0))],
        out_specs=[pl.BlockSpec((t, D_MODEL), lambda r: (0, 0)),
                   pl.BlockSpec((t, 128), lambda r: (0, 0))],
        out_shape=[jax.ShapeDtypeStruct((t, D_MODEL), BF16),
                   jax.ShapeDtypeStruct((t, 128), F32)],
        compiler_params=_params(1),
        name="norm_gates_step",
    )(x, g, wg, bif)


def _mixers_kernel(*refs, tm, tiles_per_batch, n_valid, step_bb):
    n_seq = (8, 5, 11)
    n_step = (7, 5, 5) if step_bb else (0, 0, 0)
    cuts = [0]
    for a, b in zip(n_seq, n_step):
        cuts += [cuts[-1] + a, cuts[-1] + a + b]
    seq_in, step_in, seq_out, step_out, seq_scr, step_scr = (refs[i:j] for i, j in zip(cuts[:-1], cuts[1:]))
    hn_ref, w_ref, gates_ref, wsc_ref, c0_ref, n0_ref, m0_ref, sc0_ref = seq_in
    mix_ref, cout_ref, nout_ref, mout_ref, scout_ref = seq_out
    z_ref, cu_ref, c_ref, n_ref, m_ref, dmat_ref, qk_ref, sv_ref, kv_ref, bmix_ref, amix_ref = seq_scr
    head = pl.program_id(0)
    tile = pl.program_id(1)
    rb = tile % tiles_per_batch
    nc = tm // CHUNK

    @pl.when(rb == 0)
    def _():
        c_ref[...] = c0_ref[...]
        n_ref[...] = n0_ref[...]
        m_ref[...] = m0_ref[...]
        cu_ref[0:HALO, :] = sc0_ref[...]

    row = lax.broadcasted_iota(jnp.int32, (CHUNK, CHUNK), 0)
    col = lax.broadcasted_iota(jnp.int32, (CHUNK, CHUNK), 1)
    causal = col <= row
    eye = col == row
    st = [dict() for _ in range(nc)]
    rows_of = lambda c: slice(c * CHUNK, (c + 1) * CHUNK)
    k_of = lambda rows: z_ref[rows, _K:_K + DK] * K_SCALE

    def proj(j):
        cols = slice(j * DV, (j + 1) * DV)
        z_ref[:, cols] = jnp.dot(hn_ref[...], w_ref[:, cols], preferred_element_type=F32)

    def gate_terms(c):
        rows, d = rows_of(c), st[c]
        a_row = gates_ref[0:1, rows]
        b_row = gates_ref[1:2, rows]
        d["a_col"] = jnp.sum(jnp.where(eye, a_row, 0.0), axis=-1, keepdims=True)
        d["b_col"] = jnp.sum(jnp.where(eye, b_row, 0.0), axis=-1, keepdims=True)
        dmat = jnp.where(causal, d["b_col"] + a_row, NEG)
        dmat_ref[rows, :] = dmat
        d["mloc"] = jnp.max(dmat, axis=-1, keepdims=True)
        qk_ref[rows, :] = lax.dot_general(z_ref[rows, _Q:_Q + DK].astype(BF16), k_of(rows).astype(BF16),
                                          (((1,), (1,)), ((), ())), preferred_element_type=F32)

    def stabilisers():
        m_prev = m_ref[:, 0:1]
        for c in range(nc):
            d = st[c]
            d["m_prev"] = m_prev
            d["bm"] = d["b_col"] + m_prev
            d["m_t"] = jnp.maximum(d["bm"], d["mloc"])
            d["m_last"] = d["m_t"][CHUNK - 1:CHUNK, :]
            m_prev = d["m_last"]
        m_ref[...] = jnp.broadcast_to(m_prev, (1, DK))

    def intra(c):
        rows, d = rows_of(c), st[c]
        k = k_of(rows)
        v = z_ref[rows, _V:_V + DV]
        s = qk_ref[rows, :] * jnp.exp(dmat_ref[rows, :] - d["m_t"])
        sv_ref[rows, :] = jnp.dot(s.astype(BF16), v.astype(BF16), preferred_element_type=F32)
        d["ssum"] = jnp.sum(s, axis=-1, keepdims=True)
        d["decay"] = jnp.exp(d["bm"] - d["m_t"])
        d["hden"] = jnp.exp(-d["m_t"])
        b_last = d["b_col"][CHUNK - 1:CHUNK, :]
        g_col = jnp.exp(b_last + d["a_col"] - d["m_last"])
        d["carry"] = jnp.exp(b_last + d["m_prev"] - d["m_last"])
        kv_ref[c] = jnp.dot(k.T.astype(BF16), (g_col * v).astype(BF16), preferred_element_type=F32)
        d["kn"] = jnp.sum(g_col * k, axis=0, keepdims=True)

    def inter(c):
        rows, d = rows_of(c), st[c]
        q = z_ref[rows, _Q:_Q + DK]
        c_prev = c_ref[...]
        n_prev = n_ref[...]
        num = (d["decay"] * jnp.dot(q.astype(BF16), c_prev.astype(BF16), preferred_element_type=F32)
               + sv_ref[rows, :])
        den = d["decay"] * jnp.sum(q * n_prev, axis=-1, keepdims=True) + d["ssum"]
        hb = num / jnp.maximum(jnp.abs(den), d["hden"])
        b_out = jax.nn.sigmoid(z_ref[rows, _O:_O + DV]) * hb
        bmix_ref[rows, :] = jax.nn.sigmoid(z_ref[rows, _GB:_GB + DV]) * b_out
        c_ref[...] = d["carry"] * c_prev + kv_ref[c]
        n_ref[...] = d["carry"] * n_prev + d["kn"]

    def conv(c):
        lo = c * CHUNK
        rows = rows_of(c)
        cu = z_ref[rows, _CG:_CG + DV] * z_ref[rows, _U:_U + DV]
        cu_ref[HALO + lo:HALO + lo + CHUNK, :] = cu
        taps = (cu * wsc_ref[2:3, :] + cu_ref[HALO - 1 + lo:HALO - 1 + lo + CHUNK, :] * wsc_ref[1:2, :]
                + cu_ref[HALO - 2 + lo:HALO - 2 + lo + CHUNK, :] * wsc_ref[0:1, :])
        amix_ref[rows, :] = z_ref[rows, _BG:_BG + DV] * taps

    def merge(c):
        rows = rows_of(c)
        mix = jax.nn.sigmoid(z_ref[rows, _GA:_GA + DV]) * amix_ref[rows, :] + bmix_ref[rows, :]
        mix_ref[rows, :] = mix.astype(BF16)

    piece = lambda off: off // DV
    assert (piece(_Q), piece(_V), piece(_O), piece(_GB)) == (0, 1, 2, 3)
    assert sorted(map(piece, (_U, _CG, _BG))) == [4, 5, 6] and piece(_GA) == HEAD_COLS // DV - 1
    each = lambda fn: [functools.partial(fn, c) for c in range(nc)]
    half = (nc + 1) // 2
    after = {0: [],
             1: each(gate_terms)[:half],
             2: each(gate_terms)[half:] + [stabilisers],
             3: each(intra)[:half],
             4: each(intra)[half:] + each(inter)[:1],
             5: each(inter)[1:1 + half],
             6: each(inter)[1 + half:] + each(conv),
             7: each(merge)}

    if step_bb:
        bb = step_bb
        hns_ref, gifs_ref, mprevs_ref, s0_ref, s1_ref, cs_ref, ns_ref = step_in
        mixs_ref, csout_ref, nsout_ref, msout_ref, cuouts_ref = step_out
        zs_ref, aouts_ref, qts_ref, kts_ref, hbs_ref = step_scr

        @pl.when(tile == 0)
        def _():
            z = jnp.dot(hns_ref[...], w_ref[...], preferred_element_type=F32)
            zs_ref[...] = z
            cu = z[:, _CG:_CG + DV] * z[:, _U:_U + DV]
            taps = s0_ref[...] * wsc_ref[0:1, :] + s1_ref[...] * wsc_ref[1:2, :] + cu * wsc_ref[2:3, :]
            aouts_ref[...] = z[:, _BG:_BG + DV] * taps
            cuouts_ref[...] = cu
            qt = z[:, _Q:_Q + DK].T
            kt = (z[:, _K:_K + DK] * K_SCALE).T
            for jj in range(hns_ref.shape[0] // bb):
                qts_ref[jj] = qt[:, jj * bb:(jj + 1) * bb]
                kts_ref[jj] = kt[:, jj * bb:(jj + 1) * bb]

        srows = pl.ds(pl.multiple_of(tile * bb, bb), bb)
        sd = {}

        def step_gates():
            lane = lax.broadcasted_iota(jnp.int32, (bb, 128), 1)
            pick = lambda x, idx: jnp.sum(jnp.where(lane == idx, x, 0.0), axis=-1, keepdims=True)
            gif = gifs_ref[srows, :]
            ig = pick(gif, head)
            lf = _log_sigmoid(pick(gif, head + N_HEADS))
            m_prev = pick(mprevs_ref[srows, :], head)
            q = zs_ref[srows, _Q:_Q + DK]
            k = zs_ref[srows, _K:_K + DK] * K_SCALE
            n_prev = ns_ref[...]
            m_t = jnp.maximum(lf + m_prev, ig)
            sd["w"] = jnp.exp(ig - m_t)
            sd["decay"] = jnp.exp(lf + m_prev - m_t)
            sd["s"] = jnp.sum(q * k, axis=-1, keepdims=True) * sd["w"]
            den = sd["decay"] * jnp.sum(q * n_prev, axis=-1, keepdims=True) + sd["s"]
            sd["inv"] = 1.0 / jnp.maximum(jnp.abs(den), jnp.exp(-m_t))
            nsout_ref[...] = sd["decay"] * n_prev + sd["w"] * k
            msout_ref[...] = jnp.broadcast_to(m_t, (bb, DK))

        def step_rows(lo, hi):
            qt = qts_ref[tile]
            kt = kts_ref[tile]
            v = zs_ref[srows, _V:_V + DV]
            for i in range(lo, hi):
                c_prev = cs_ref[i]
                d_i = sd["decay"][i:i + 1, :]
                v_i = v[i:i + 1, :]
                qc = jnp.sum(qt[:, i:i + 1] * c_prev, axis=0, keepdims=True)
                hbs_ref[i:i + 1, :] = (d_i * qc + sd["s"][i:i + 1, :] * v_i) * sd["inv"][i:i + 1, :]
                csout_ref[i] = d_i * c_prev + (sd["w"][i:i + 1, :] * kt[:, i:i + 1]) * v_i

        def step_merge():
            b_out = jax.nn.sigmoid(zs_ref[srows, _O:_O + DV]) * hbs_ref[...]
            mixs_ref[...] = (jax.nn.sigmoid(zs_ref[srows, _GA:_GA + DV]) * aouts_ref[srows, :]
                             + jax.nn.sigmoid(zs_ref[srows, _GB:_GB + DV]) * b_out)

        after[0] += [step_gates, functools.partial(step_rows, 0, bb // 2)]
        after[5] += [functools.partial(step_rows, bb // 2, bb)]
        after[6] = [step_merge] + after[6]

    for j in range(HEAD_COLS // DV):
        proj(j)
        for fn in after.get(j, []):
            fn()

    @pl.when(rb == tiles_per_batch - 1)
    def _():
        cout_ref[...] = c_ref[...]
        nout_ref[...] = n_ref[...]
        mout_ref[...] = m_ref[...]
        scout_ref[...] = cu_ref[HALO + n_valid - 2:HALO + n_valid, :]

    cu_ref[0:HALO, :] = cu_ref[tm:tm + HALO, :]


def _mixers(hn, wp, gates, wsc, c0, n0, m0, sc0, step=None, *, n_batch, tm, n_valid):
    t = hn.shape[0]
    n_tiles = t // tm
    tpb = n_tiles // n_batch
    in_specs = [pl.BlockSpec((tm, D_MODEL), lambda h, r: (r, 0)),
                pl.BlockSpec((D_MODEL, HEAD_COLS), lambda h, r: (0, h)),
                pl.BlockSpec((None, 2, tm), lambda h, r: (h, 0, r)),
                pl.BlockSpec((CONV_W, DV), lambda h, r: (0, h)),
                pl.BlockSpec((None, DK, DV), lambda h, r: (h, 0, 0)),
                pl.BlockSpec((None, 1, DK), lambda h, r: (h, 0, 0)),
                pl.BlockSpec((None, 1, DK), lambda h, r: (h, 0, 0)),
                pl.BlockSpec((HALO, DV), lambda h, r: (0, h))]
    out_specs = [pl.BlockSpec((tm, DV), lambda h, r: (r, h)),
                 pl.BlockSpec((None, None, DK, DV), lambda h, r: (r // tpb, h, 0, 0)),
                 pl.BlockSpec((None, None, 1, DK), lambda h, r: (r // tpb, h, 0, 0)),
                 pl.BlockSpec((None, None, 1, DK), lambda h, r: (r // tpb, h, 0, 0)),
                 pl.BlockSpec((None, 2, DV), lambda h, r: (r // tpb, 0, h))]
    out_shape = [jax.ShapeDtypeStruct((t, D_MODEL), BF16),
                 jax.ShapeDtypeStruct((n_batch, N_HEADS, DK, DV), F32),
                 jax.ShapeDtypeStruct((n_batch, N_HEADS, 1, DK), F32),
                 jax.ShapeDtypeStruct((n_batch, N_HEADS, 1, DK), F32),
                 jax.ShapeDtypeStruct((n_batch, 2, D_CONV), F32)]
    scratch = [pltpu.VMEM((tm, HEAD_COLS), F32),
               pltpu.VMEM((tm + HALO, DV), F32),
               pltpu.VMEM((DK, DV), F32),
               pltpu.VMEM((1, DK), F32),
               pltpu.VMEM((1, DK), F32),
               pltpu.VMEM((tm, CHUNK), F32),
               pltpu.VMEM((tm, CHUNK), F32),
               pltpu.VMEM((tm, DV), F32),
               pltpu.VMEM((tm // CHUNK, DK, DV), F32),
               pltpu.VMEM((tm, DV), F32),
               pltpu.VMEM((tm, DV), F32)]
    step_bb = 0
    if step is not None:
        nb = step[0].shape[0]
        assert nb % n_tiles == 0
        step_bb = nb // n_tiles
        in_specs += [pl.BlockSpec((nb, D_MODEL), lambda h, r: (0, 0)),
                     pl.BlockSpec((nb, 128), lambda h, r: (0, 0)),
                     pl.BlockSpec((nb, 128), lambda h, r: (0, 0)),
                     pl.BlockSpec((nb, DV), lambda h, r: (0, h)),
                     pl.BlockSpec((nb, DV), lambda h, r: (0, h)),
                     pl.BlockSpec((step_bb, None, DK, DV), lambda h, r: (r, h, 0, 0)),
                     pl.BlockSpec((None, step_bb, DK), lambda h, r: (h, r, 0))]
        out_specs += [pl.BlockSpec((step_bb, DV), lambda h, r: (r, h)),
                      pl.BlockSpec((step_bb, None, DK, DV), lambda h, r: (r, h, 0, 0)),
                      pl.BlockSpec((None, step_bb, DK), lambda h, r: (h, r, 0)),
                      pl.BlockSpec((None, step_bb, DK), lambda h, r: (h, r, 0)),
                      pl.BlockSpec((nb, DV), lambda h, r: (0, h))]
        out_shape += [jax.ShapeDtypeStruct((nb, D_MODEL), F32),
                      jax.ShapeDtypeStruct((nb, N_HEADS, DK, DV), F32),
                      jax.ShapeDtypeStruct((N_HEADS, nb, DK), F32),
                      jax.ShapeDtypeStruct((N_HEADS, nb, DK), F32),
                      jax.ShapeDtypeStruct((nb, D_CONV), F32)]
        scratch += [pltpu.VMEM((nb, HEAD_COLS), F32),
                    pltpu.VMEM((nb, DV), F32),
                    pltpu.VMEM((n_tiles, DK, step_bb), F32),
                    pltpu.VMEM((n_tiles, DK, step_bb), F32),
                    pltpu.VMEM((step_bb, DV), F32)]
    return pl.pallas_call(
        functools.partial(_mixers_kernel, tm=tm, tiles_per_batch=tpb, n_valid=n_valid, step_bb=step_bb),
        grid=(N_HEADS, n_tiles),
        in_specs=in_specs,
        out_specs=out_specs,
        out_shape=out_shape,
        scratch_shapes=scratch,
        compiler_params=_params(2),
        name="mixers",
    )(hn, wp, gates, wsc, c0, n0, m0, sc0, *(step or ()))


def _mixer_residual(x_ref, mix_ref, wo_ref, g2_ref, acc_ref, h2_ref):
    mix = mix_ref[...]
    if mix.dtype != BF16:
        mix = mix.astype(BF16)
    acc_ref[...] = x_ref[...] + jnp.dot(mix, wo_ref[...], preferred_element_type=F32)
    rows = acc_ref.shape[0]
    for lo in range(0, rows, FFN_SUB):
        h2_ref[lo:lo + FFN_SUB, :] = _rms(acc_ref[lo:lo + FFN_SUB, :], g2_ref[...]).astype(BF16)


def _final_norm(acc_ref, g_ref, y_ref):
    rows = acc_ref.shape[0]
    for lo in range(0, rows, FFN_SUB):
        y_ref[lo:lo + FFN_SUB, :] = _rms(acc_ref[lo:lo + FFN_SUB, :], g_ref[...])


def _ffn_seq_kernel(x_ref, mix_ref, wo_ref, g2_ref, wua_ref, wub_ref, wca_ref, wcb_ref, wd_ref, g_ref,
                    ha0_ref, hb0_ref,
                    y_ref, fca_ref, fcb_ref,
                    acc_ref, h2_ref, ua_ref, ub_ref, ha_ref, hb_ref, act_ref,
                    *, tm, tiles_per_batch, n_valid, n_f):
    r = pl.program_id(0)
    f = pl.program_id(1)
    rb = r % tiles_per_batch

    @pl.when((r == 0) & (f == 0))
    def _():
        ha_ref[...] = jnp.zeros_like(ha_ref)
        hb_ref[...] = jnp.zeros_like(hb_ref)

    @pl.when(f == 0)
    def _():
        _mixer_residual(x_ref, mix_ref, wo_ref, g2_ref, acc_ref, h2_ref)

    def up_proj(wu_ref, u_ref, halo_ref, halo0_ref, tail_ref):
        u_ref[HALO:tm + HALO, :] = jnp.dot(h2_ref[...], wu_ref[...], preferred_element_type=F32)
        u_ref[0:HALO, :] = jnp.where(rb == 0, halo0_ref[...], halo_ref[f])
        halo_ref[f] = u_ref[tm:tm + HALO, :]
        tail_ref[f] = u_ref[HALO + n_valid - 2:HALO + n_valid, :]

    up_proj(wua_ref, ua_ref, ha_ref, ha0_ref, fca_ref)
    up_proj(wub_ref, ub_ref, hb_ref, hb0_ref, fcb_ref)

    def conv(u_ref, wc_ref, lo):
        return (u_ref[HALO + lo:HALO + lo + FFN_SUB, :] * wc_ref[2:3, :]
                + u_ref[HALO - 1 + lo:HALO - 1 + lo + FFN_SUB, :] * wc_ref[1:2, :]
                + u_ref[HALO - 2 + lo:HALO - 2 + lo + FFN_SUB, :] * wc_ref[0:1, :])

    for sb in range(tm // FFN_SUB):
        lo = sb * FFN_SUB
        up_a = conv(ua_ref, wca_ref, lo)
        up_b = conv(ub_ref, wcb_ref, lo)
        act_ref[lo:lo + FFN_SUB, :] = (up_b * jax.nn.sigmoid(up_b) * up_a).astype(BF16)
    acc_ref[...] += jnp.dot(act_ref[...], wd_ref[...], preferred_element_type=F32)

    @pl.when(f == n_f - 1)
    def _():
        _final_norm(acc_ref, g_ref, y_ref)


def _ffn_seq(x, mix, wo, g2, wu, wc, wd, g, halo_a, halo_b, *, n_batch, tm, tf, n_valid):
    t = x.shape[0]
    tpb = t // n_batch // tm
    n_f = D_FF // tf
    return pl.pallas_call(
        functools.partial(_ffn_seq_kernel, tm=tm, tiles_per_batch=tpb, n_valid=n_valid, n_f=n_f),
        grid=(t // tm, n_f),
        in_specs=[pl.BlockSpec((tm, D_MODEL), lambda r, f: (r, 0)),
                  pl.BlockSpec((tm, D_MODEL), lambda r, f: (r, 0)),
                  pl.BlockSpec((D_MODEL, D_MODEL), lambda r, f: (0, 0), pipeline_mode=pl.Buffered(1)),
                  pl.BlockSpec((1, D_MODEL), lambda r, f: (0, 0)),
                  pl.BlockSpec((D_MODEL, tf), lambda r, f: (0, f)),
                  pl.BlockSpec((D_MODEL, tf), lambda r, f: (0, n_f + f)),
                  pl.BlockSpec((CONV_W, tf), lambda r, f: (0, f)),
                  pl.BlockSpec((CONV_W, tf), lambda r, f: (0, n_f + f)),
                  pl.BlockSpec((tf, D_MODEL), lambda r, f: (f, 0)),
                  pl.BlockSpec((1, D_MODEL), lambda r, f: (0, 0)),
                  pl.BlockSpec((HALO, tf), lambda r, f: (0, f)),
                  pl.BlockSpec((HALO, tf), lambda r, f: (0, f))],
        out_specs=[pl.BlockSpec((tm, D_MODEL), lambda r, f: (r, 0)),
                   pl.BlockSpec((None, n_f, 2, tf), lambda r, f: (r // tpb, 0, 0, 0)),
                   pl.BlockSpec((None, n_f, 2, tf), lambda r, f: (r // tpb, 0, 0, 0))],
        out_shape=[jax.ShapeDtypeStruct((t, D_MODEL), F32),
                   jax.ShapeDtypeStruct((n_batch, n_f, 2, tf), F32),
                   jax.ShapeDtypeStruct((n_batch, n_f, 2, tf), F32)],
        scratch_shapes=[pltpu.VMEM((tm, D_MODEL), F32),
                        pltpu.VMEM((tm, D_MODEL), BF16),
                        pltpu.VMEM((tm + HALO, tf), F32),
                        pltpu.VMEM((tm + HALO, tf), F32),
                        pltpu.VMEM((n_f, HALO, tf), F32),
                        pltpu.VMEM((n_f, HALO, tf), F32),
                        pltpu.VMEM((tm, tf), BF16)],
        compiler_params=_params(2),
        name="ffn_seq",
    )(x, mix, wo, g2, wu, wu, wc, wc, wd, g, halo_a, halo_b)


def _ffn_step_kernel(x_ref, mix_ref, wo_ref, g2_ref, wua_ref, wub_ref, wca_ref, wcb_ref, wd_ref, g_ref,
                     s0a_ref, s1a_ref, s0b_ref, s1b_ref,
                     y_ref, upa_ref, upb_ref, acc_ref, h2_ref, *, n_f):
    f = pl.program_id(0)

    @pl.when(f == 0)
    def _():
        _mixer_residual(x_ref, mix_ref, wo_ref, g2_ref, acc_ref, h2_ref)

    def half(wu_ref, wc_ref, s0_ref, s1_ref, up_ref):
        up = jnp.dot(h2_ref[...], wu_ref[...], preferred_element_type=F32)
        up_ref[...] = up
        return s0_ref[...] * wc_ref[0:1, :] + s1_ref[...] * wc_ref[1:2, :] + up * wc_ref[2:3, :]

    up_a = half(wua_ref, wca_ref, s0a_ref, s1a_ref, upa_ref)
    up_b = half(wub_ref, wcb_ref, s0b_ref, s1b_ref, upb_ref)
    act = (up_b * jax.nn.sigmoid(up_b) * up_a).astype(BF16)
    acc_ref[...] += jnp.dot(act, wd_ref[...], preferred_element_type=F32)

    @pl.when(f == n_f - 1)
    def _():
        _final_norm(acc_ref, g_ref, y_ref)


def _ffn_step(x, mix, wo, g2, wu, wc, wd, g, s0, s1, *, tf):
    nb = x.shape[0]
    n_f = D_FF // tf
    lo = lambda f: (0, f)
    hi = lambda f: (0, n_f + f)
    return pl.pallas_call(
        functools.partial(_ffn_step_kernel, n_f=n_f),
        grid=(n_f,),
        in_specs=[pl.BlockSpec((nb, D_MODEL), lambda f: (0, 0)),
                  pl.BlockSpec((nb, D_MODEL), lambda f: (0, 0)),
                  pl.BlockSpec((D_MODEL, D_MODEL), lambda f: (0, 0), pipeline_mode=pl.Buffered(1)),
                  pl.BlockSpec((1, D_MODEL), lambda f: (0, 0)),
                  pl.BlockSpec((D_MODEL, tf), lo),
                  pl.BlockSpec((D_MODEL, tf), hi),
                  pl.BlockSpec((CONV_W, tf), lo),
                  pl.BlockSpec((CONV_W, tf), hi),
                  pl.BlockSpec((tf, D_MODEL), lambda f: (f, 0)),
                  pl.BlockSpec((1, D_MODEL), lambda f: (0, 0)),
                  pl.BlockSpec((nb, tf), lo),
                  pl.BlockSpec((nb, tf), lo),
                  pl.BlockSpec((nb, tf), hi),
                  pl.BlockSpec((nb, tf), hi)],
        out_specs=[pl.BlockSpec((nb, D_MODEL), lambda f: (0, 0)),
                   pl.BlockSpec((nb, tf), lo),
                   pl.BlockSpec((nb, tf), lo)],
        out_shape=[jax.ShapeDtypeStruct((nb, D_MODEL), F32),
                   jax.ShapeDtypeStruct((nb, D_FF), F32),
                   jax.ShapeDtypeStruct((nb, D_FF), F32)],
        scratch_shapes=[pltpu.VMEM((nb, D_MODEL), F32),
                        pltpu.VMEM((nb, D_MODEL), BF16)],
        compiler_params=_params(1),
        name="ffn_step",
    )(x, mix, wo, g2, wu, wu, wc, wc, wd, g, s0, s1, s0, s1)


def _tail_to_halo(tail):
    return jnp.pad(tail, ((HALO - 2, 0), (0, 0)))


def _join_tail_blocks(t):
    b, n_f, _, tf = t.shape
    return t.transpose(0, 2, 1, 3).reshape(b, 2, n_f * tf)


def kernel(x_prompt, x_sample, state_mlstm_C, state_mlstm_n, state_mlstm_m, state_shortconv,
           state_ffnconv, meta_tokens, norm1_g, w_in, b_if, w_shortconv, w_out, norm2_g,
           w_up, w_ffconv, w_down, norm_f_g):
    assert w_in.shape[0] == 1, "single-layer trunk"
    n_batch, seq, _ = x_prompt.shape
    n_dec = x_sample.shape[0]

    wp, wg = _prep_win(w_in[0].T, tk=1024)
    bif = jnp.pad(b_if[0], (0, 128 - 2 * N_HEADS)).reshape(1, 128)
    wo = w_out[0].astype(BF16)
    wu = w_up[0].astype(BF16)
    wd = w_down[0].astype(BF16)
    wsc = w_shortconv[0]
    wfc = w_ffconv[0]
    g1 = norm1_g[0].reshape(1, D_MODEL)
    g2 = norm2_g[0].reshape(1, D_MODEL)
    gf = norm_f_g.reshape(1, D_MODEL)

    xm = jnp.pad(meta_tokens, ((0, CHUNK - N_META), (0, 0)))
    hn_m, gates_m = _norm_gates_seq(xm, g1, wg, bif, tm=CHUNK, n_valid=N_META)
    zc = jnp.zeros((N_HEADS, DK, DV), F32)
    zn = jnp.zeros((N_HEADS, 1, DK), F32)
    mix_m, c_m, n_m, m_m, sc_m = _mixers(
        hn_m, wp, gates_m, wsc, zc, zn, zn, jnp.zeros((HALO, D_CONV), F32),
        n_batch=1, tm=CHUNK, n_valid=N_META)
    zh = jnp.zeros((HALO, D_FF), F32)
    _, fca_m, fcb_m = _ffn_seq(xm, mix_m, wo, g2, wu, wfc, wd, gf, zh, zh,
                               n_batch=1, tm=CHUNK, tf=FFN_TF, n_valid=N_META)
    fca_m = _join_tail_blocks(fca_m)
    fcb_m = _join_tail_blocks(fcb_m)

    xp = x_prompt.reshape(n_batch * seq, D_MODEL)
    tm = 512
    hn_p, gates_p = _norm_gates_seq(xp, g1, wg, bif, tm=tm, n_valid=CHUNK)
    xs = x_sample.reshape(n_dec, D_MODEL)
    hn_s, gif_s = _norm_gates_step(xs, g1, wg, bif)
    mprev = jnp.pad(state_mlstm_m[0], ((0, 0), (0, 128 - N_HEADS)))
    n_hb = state_mlstm_n[0].transpose(1, 0, 2)
    mix_p, c_p, n_p, m_p, sc_p, mix_s, c_s, n_s, m_s, cu_s = _mixers(
        hn_p, wp, gates_p, wsc, c_m[0], n_m[0], m_m[0], _tail_to_halo(sc_m[0]),
        (hn_s, gif_s, mprev, state_shortconv[0, :, 0, :], state_shortconv[0, :, 1, :],
         state_mlstm_C[0], n_hb),
        n_batch=n_batch, tm=tm, n_valid=tm)
    y_p, fca_p, fcb_p = _ffn_seq(xp, mix_p, wo, g2, wu, wfc, wd, gf,
                                 _tail_to_halo(fca_m[0]), _tail_to_halo(fcb_m[0]),
                                 n_batch=n_batch, tm=tm, tf=FFN_TF, n_valid=tm)

    y_s, upa_s, upb_s = _ffn_step(xs, mix_s, wo, g2, wu, wfc, wd, gf,
                                  state_ffnconv[0, :, 0, :], state_ffnconv[0, :, 1, :], tf=FFN_TF)

    return (y_p.reshape(n_batch, seq, D_MODEL),
            y_s.reshape(n_dec, 1, D_MODEL),
            c_p[None],
            n_p.reshape(1, n_batch, N_HEADS, DK),
            m_p[:, :, 0, 0][None],
            sc_p[None],
            jnp.concatenate([_join_tail_blocks(fca_p), _join_tail_blocks(fcb_p)], axis=-1)[None],
            c_s[None],
            n_s.transpose(1, 0, 2)[None],
            m_s[:, :, 0].T[None],
            jnp.stack([state_shortconv[0, :, 1, :], cu_s], axis=1)[None],
            jnp.stack([state_ffnconv[0, :, 1, :], jnp.concatenate([upa_s, upb_s], axis=-1)], axis=1)[None])
```

```python
import functools

import jax
import jax.numpy as jnp
from jax import lax
from jax.experimental import pallas as pl
from jax.experimental.pallas import tpu as pltpu

D_MODEL = 2048
N_META = 16
N_HEADS = 8
DK = 128
DV = 256
D_CONV = D_MODEL
CONV_W = 3
D_FF = 5632
CHUNK = 128
EPS = 1e-6
K_SCALE = DK ** -0.5
NEG = -1e30
HEAD_COLS = 7 * DV + 2 * DK
V7X_VMEM_LIMIT = 56 * 1024 * 1024
HALO = 8
FFN_SUB = 64
FFN_TF = 512
F32 = jnp.float32
BF16 = jnp.bfloat16

_Q, _K, _V, _O, _GB, _U, _CG, _BG, _GA = (0, 128, 256, 512, 768, 1024, 1280, 1536, 1792)


def _params(n_grid):
    return pltpu.CompilerParams(dimension_semantics=("arbitrary",) * n_grid,
                                vmem_limit_bytes=V7X_VMEM_LIMIT)


def _log_sigmoid(x):
    return jnp.minimum(x, 0.0) - jnp.log1p(jnp.exp(-jnp.abs(x)))


def _rms(x, g):
    return x * lax.rsqrt(jnp.mean(x * x, axis=-1, keepdims=True) + EPS) * g


def _prep_win_kernel(u_ref, cg_ref, bg_ref, ga_ref, v_ref, o_ref, gb_ref, q_ref, k_ref, gif_ref,
                     wp_ref, wg_ref):
    for ref, off, d in ((u_ref, _U, DV), (cg_ref, _CG, DV), (bg_ref, _BG, DV), (ga_ref, _GA, DV),
                        (v_ref, _V, DV), (o_ref, _O, DV), (gb_ref, _GB, DV), (q_ref, _Q, DK), (k_ref, _K, DK)):
        wp_ref[:, off:off + d] = ref[...].T.astype(BF16)
    gif = gif_ref[...].T
    lane = lax.broadcasted_iota(jnp.int32, gif.shape, 1)
    wg_ref[...] = jnp.where(lane < 2 * N_HEADS, gif, 0.0).astype(BF16)


def _prep_win(wt, *, tk):
    o_q = 3 * D_CONV
    o_k = o_q + N_HEADS * DK
    o_v = o_k + N_HEADS * DK
    o_o = o_v + N_HEADS * DV
    o_gif = o_o + N_HEADS * DV
    o_ga = o_gif + 2 * N_HEADS
    o_gb = o_ga + D_CONV
    assert o_ga % 8 == 0
    rows = lambda off, d: pl.BlockSpec((pl.Element(d), pl.Element(tk)),
                                       lambda i, h: (pl.multiple_of(off + d * h, 8), pl.multiple_of(i * tk, 128)))
    return pl.pallas_call(
        _prep_win_kernel,
        grid=(D_MODEL // tk, N_HEADS),
        in_specs=[rows(0, DV), rows(D_CONV, DV), rows(2 * D_CONV, DV), rows(o_ga, DV), rows(o_v, DV),
                  rows(o_o, DV), rows(o_gb, DV), rows(o_q, DK), rows(o_k, DK),
                  pl.BlockSpec((pl.Element(128), pl.Element(tk)),
                               lambda i, h: (o_gif, pl.multiple_of(i * tk, 128)))],
        out_specs=[pl.BlockSpec((tk, HEAD_COLS), lambda i, h: (i, h)),
                   pl.BlockSpec((tk, 128), lambda i, h: (i, 0))],
        out_shape=[jax.ShapeDtypeStruct((D_MODEL, N_HEADS * HEAD_COLS), BF16),
                   jax.ShapeDtypeStruct((D_MODEL, 128), BF16)],
        compiler_params=_params(2),
        name="prep_win",
    )(*([wt] * 10))


def _norm_gates_seq_kernel(x_ref, g_ref, wg_ref, bif_ref, hn_ref, gates_ref, *, tm, n_valid):
    hn = _rms(x_ref[...], g_ref[...]).astype(BF16)
    hn_ref[...] = hn
    gif = jnp.dot(hn, wg_ref[...], preferred_element_type=F32) + bif_ref[...]
    row = lax.broadcasted_iota(jnp.int32, (CHUNK, CHUNK), 0)
    col = lax.broadcasted_iota(jnp.int32, (CHUNK, CHUNK), 1)
    upper = (row <= col).astype(F32)
    for c in range(tm // CHUNK):
        gt = gif[c * CHUNK:(c + 1) * CHUNK, :].T
        ig = gt[0:N_HEADS, :]
        lf = _log_sigmoid(gt[N_HEADS:2 * N_HEADS, :])
        if n_valid < CHUNK:
            tok = lax.broadcasted_iota(jnp.int32, (N_HEADS, CHUNK), 1)
            ig = jnp.where(tok < n_valid, ig, NEG)
            lf = jnp.where(tok < n_valid, lf, 0.0)
        b = jnp.dot(lf, upper, preferred_element_type=F32, precision=lax.Precision.HIGHEST)
        a = ig - b
        for j in range(N_HEADS):
            gates_ref[j, 0:1, c * CHUNK:(c + 1) * CHUNK] = a[j:j + 1, :]
            gates_ref[j, 1:2, c * CHUNK:(c + 1) * CHUNK] = b[j:j + 1, :]


def _norm_gates_seq(x, g, wg, bif, *, tm, n_valid):
    t = x.shape[0]
    return pl.pallas_call(
        functools.partial(_norm_gates_seq_kernel, tm=tm, n_valid=n_valid),
        grid=(t // tm,),
        in_specs=[pl.BlockSpec((tm, D_MODEL), lambda r: (r, 0)),
                  pl.BlockSpec((1, D_MODEL), lambda r: (0, 0)),
                  pl.BlockSpec((D_MODEL, 128), lambda r: (0, 0)),
                  pl.BlockSpec((1, 128), lambda r: (0, 0))],
        out_specs=[pl.BlockSpec((tm, D_MODEL), lambda r: (r, 0)),
                   pl.BlockSpec((N_HEADS, 2, tm), lambda r: (0, 0, r))],
        out_shape=[jax.ShapeDtypeStruct((t, D_MODEL), BF16),
                   jax.ShapeDtypeStruct((N_HEADS, 2, t), F32)],
        compiler_params=_params(1),
        name="norm_gates_seq",
    )(x, g, wg, bif)


def _norm_gates_step_kernel(x_ref, g_ref, wg_ref, bif_ref, hn_ref, gif_ref):
    hn = _rms(x_ref[...], g_ref[...]).astype(BF16)
    hn_ref[...] = hn
    gif_ref[...] = jnp.dot(hn, wg_ref[...], preferred_element_type=F32) + bif_ref[...]


def _norm_gates_step(x, g, wg, bif):
    t = x.shape[0]
    return pl.pallas_call(
        _norm_gates_step_kernel,
        grid=(1,),
        in_specs=[pl.BlockSpec((t, D_MODEL), lambda r: (0, 0)),
                  pl.BlockSpec((1, D_MODEL), lambda r: (0, 0)),
                  pl.BlockSpec((D_MODEL, 128), lambda r: (0, 0)),
                  pl.BlockSpec((1, 128), lambda r: (0, 0))],
        out_specs=[pl.BlockSpec((t, D_MODEL), lambda r: (0, 0)),
                   pl.BlockSpec((t, 128), lambda r: (0, 0))],
        out_shape=[jax.ShapeDtypeStruct((t, D_MODEL), BF16),
                   jax.ShapeDtypeStruct((t, 128), F32)],
        compiler_params=_params(1),
        name="norm_gates_step",
    )(x, g, wg, bif)


def _mixers_kernel(*refs, tm, tiles_per_batch, n_valid, step_bb):
    n_seq = (8, 5, 11)
    n_step = (7, 5, 5) if step_bb else (0, 0, 0)
    cuts = [0]
    for a, b in zip(n_seq, n_step):
        cuts += [cuts[-1] + a, cuts[-1] + a + b]
    seq_in, step_in, seq_out, step_out, seq_scr, step_scr = (refs[i:j] for i, j in zip(cuts[:-1], cuts[1:]))
    hn_ref, w_ref, gates_ref, wsc_ref, c0_ref, n0_ref, m0_ref, sc0_ref = seq_in
    mix_ref, cout_ref, nout_ref, mout_ref, scout_ref = seq_out
    z_ref, cu_ref, c_ref, n_ref, m_ref, dmat_ref, qk_ref, sv_ref, kv_ref, bmix_ref, amix_ref = seq_scr
    head = pl.program_id(0)
    tile = pl.program_id(1)
    rb = tile % tiles_per_batch
    nc = tm // CHUNK

    @pl.when(rb == 0)
    def _():
        c_ref[...] = c0_ref[...]
        n_ref[...] = n0_ref[...]
        m_ref[...] = m0_ref[...]
        cu_ref[0:HALO, :] = sc0_ref[...]

    row = lax.broadcasted_iota(jnp.int32, (CHUNK, CHUNK), 0)
    col = lax.broadcasted_iota(jnp.int32, (CHUNK, CHUNK), 1)
    causal = col <= row
    eye = col == row
    st = [dict() for _ in range(nc)]
    rows_of = lambda c: slice(c * CHUNK, (c + 1) * CHUNK)
    k_of = lambda rows: z_ref[rows, _K:_K + DK] * K_SCALE

    def proj(j):
        cols = slice(j * DV, (j + 1) * DV)
        z_ref[:, cols] = jnp.dot(hn_ref[...], w_ref[:, cols], preferred_element_type=F32)

    def gate_terms(c):
        rows, d = rows_of(c), st[c]
        a_row = gates_ref[0:1, rows]
        b_row = gates_ref[1:2, rows]
        d["a_col"] = jnp.sum(jnp.where(eye, a_row, 0.0), axis=-1, keepdims=True)
        d["b_col"] = jnp.sum(jnp.where(eye, b_row, 0.0), axis=-1, keepdims=True)
        dmat = jnp.where(causal, d["b_col"] + a_row, NEG)
        dmat_ref[rows, :] = dmat
        d["mloc"] = jnp.max(dmat, axis=-1, keepdims=True)
        qk_ref[rows, :] = lax.dot_general(z_ref[rows, _Q:_Q + DK].astype(BF16), k_of(rows).astype(BF16),
                                          (((1,), (1,)), ((), ())), preferred_element_type=F32)

    def stabilisers():
        m_prev = m_ref[:, 0:1]
        for c in range(nc):
            d = st[c]
            d["m_prev"] = m_prev
            d["bm"] = d["b_col"] + m_prev
            d["m_t"] = jnp.maximum(d["bm"], d["mloc"])
            d["m_last"] = d["m_t"][CHUNK - 1:CHUNK, :]
            m_prev = d["m_last"]
        m_ref[...] = jnp.broadcast_to(m_prev, (1, DK))

    def intra(c):
        rows, d = rows_of(c), st[c]
        k = k_of(rows)
        v = z_ref[rows, _V:_V + DV]
        s = qk_ref[rows, :] * jnp.exp(dmat_ref[rows, :] - d["m_t"])
        sv_ref[rows, :] = jnp.dot(s.astype(BF16), v.astype(BF16), preferred_element_type=F32)
        d["ssum"] = jnp.sum(s, axis=-1, keepdims=True)
        d["decay"] = jnp.exp(d["bm"] - d["m_t"])
        d["hden"] = jnp.exp(-d["m_t"])
        b_last = d["b_col"][CHUNK - 1:CHUNK, :]
        g_col = jnp.exp(b_last + d["a_col"] - d["m_last"])
        d["carry"] = jnp.exp(b_last + d["m_prev"] - d["m_last"])
        kv_ref[c] = jnp.dot(k.T.astype(BF16), (g_col * v).astype(BF16), preferred_element_type=F32)
        d["kn"] = jnp.sum(g_col * k, axis=0, keepdims=True)

    def inter(c):
        rows, d = rows_of(c), st[c]
        q = z_ref[rows, _Q:_Q + DK]
        c_prev = c_ref[...]
        n_prev = n_ref[...]
        num = (d["decay"] * jnp.dot(q.astype(BF16), c_prev.astype(BF16), preferred_element_type=F32)
               + sv_ref[rows, :])
        den = d["decay"] * jnp.sum(q * n_prev, axis=-1, keepdims=True) + d["ssum"]
        hb = num / jnp.maximum(jnp.abs(den), d["hden"])
        b_out = jax.nn.sigmoid(z_ref[rows, _O:_O + DV]) * hb
        bmix_ref[rows, :] = jax.nn.sigmoid(z_ref[rows, _GB:_GB + DV]) * b_out
        c_ref[...] = d["carry"] * c_prev + kv_ref[c]
        n_ref[...] = d["carry"] * n_prev + d["kn"]

    def conv(c):
        lo = c * CHUNK
        rows = rows_of(c)
        cu = z_ref[rows, _CG:_CG + DV] * z_ref[rows, _U:_U + DV]
        cu_ref[HALO + lo:HALO + lo + CHUNK, :] = cu
        taps = (cu * wsc_ref[2:3, :] + cu_ref[HALO - 1 + lo:HALO - 1 + lo + CHUNK, :] * wsc_ref[1:2, :]
                + cu_ref[HALO - 2 + lo:HALO - 2 + lo + CHUNK, :] * wsc_ref[0:1, :])
        amix_ref[rows, :] = z_ref[rows, _BG:_BG + DV] * taps

    def merge(c):
        rows = rows_of(c)
        mix = jax.nn.sigmoid(z_ref[rows, _GA:_GA + DV]) * amix_ref[rows, :] + bmix_ref[rows, :]
        mix_ref[rows, :] = mix.astype(BF16)

    piece = lambda off: off // DV
    assert (piece(_Q), piece(_V), piece(_O), piece(_GB)) == (0, 1, 2, 3)
    assert sorted(map(piece, (_U, _CG, _BG))) == [4, 5, 6] and piece(_GA) == HEAD_COLS // DV - 1
    each = lambda fn: [functools.partial(fn, c) for c in range(nc)]
    half = (nc + 1) // 2
    after = {0: [],
             1: each(gate_terms)[:half],
             2: each(gate_terms)[half:] + [stabilisers],
             3: each(intra)[:half],
             4: each(intra)[half:] + each(inter)[:1],
             5: each(inter)[1:1 + half],
             6: each(inter)[1 + half:] + each(conv),
             7: each(merge)}

    if step_bb:
        bb = step_bb
        hns_ref, gifs_ref, mprevs_ref, s0_ref, s1_ref, cs_ref, ns_ref = step_in
        mixs_ref, csout_ref, nsout_ref, msout_ref, cuouts_ref = step_out
        zs_ref, aouts_ref, qts_ref, kts_ref, hbs_ref = step_scr

        @pl.when(tile == 0)
        def _():
            z = jnp.dot(hns_ref[...], w_ref[...], preferred_element_type=F32)
            zs_ref[...] = z
            cu = z[:, _CG:_CG + DV] * z[:, _U:_U + DV]
            taps = s0_ref[...] * wsc_ref[0:1, :] + s1_ref[...] * wsc_ref[1:2, :] + cu * wsc_ref[2:3, :]
            aouts_ref[...] = z[:, _BG:_BG + DV] * taps
            cuouts_ref[...] = cu
            qt = z[:, _Q:_Q + DK].T
            kt = (z[:, _K:_K + DK] * K_SCALE).T
            for jj in range(hns_ref.shape[0] // bb):
                qts_ref[jj] = qt[:, jj * bb:(jj + 1) * bb]
                kts_ref[jj] = kt[:, jj * bb:(jj + 1) * bb]

        srows = pl.ds(pl.multiple_of(tile * bb, bb), bb)
        sd = {}

        def step_gates():
            lane = lax.broadcasted_iota(jnp.int32, (bb, 128), 1)
            pick = lambda x, idx: jnp.sum(jnp.where(lane == idx, x, 0.0), axis=-1, keepdims=True)
            gif = gifs_ref[srows, :]
            ig = pick(gif, head)
            lf = _log_sigmoid(pick(gif, head + N_HEADS))
            m_prev = pick(mprevs_ref[srows, :], head)
            q = zs_ref[srows, _Q:_Q + DK]
            k = zs_ref[srows, _K:_K + DK] * K_SCALE
            n_prev = ns_ref[...]
            m_t = jnp.maximum(lf + m_prev, ig)
            sd["w"] = jnp.exp(ig - m_t)
            sd["decay"] = jnp.exp(lf + m_prev - m_t)
            sd["s"] = jnp.sum(q * k, axis=-1, keepdims=True) * sd["w"]
            den = sd["decay"] * jnp.sum(q * n_prev, axis=-1, keepdims=True) + sd["s"]
            sd["inv"] = 1.0 / jnp.maximum(jnp.abs(den), jnp.exp(-m_t))
            nsout_ref[...] = sd["decay"] * n_prev + sd["w"] * k
            msout_ref[...] = jnp.broadcast_to(m_t, (bb, DK))

        def step_rows(lo, hi):
            qt = qts_ref[tile]
            kt = kts_ref[tile]
            v = zs_ref[srows, _V:_V + DV]
            for i in range(lo, hi):
                c_prev = cs_ref[i]
                d_i = sd["decay"][i:i + 1, :]
                v_i = v[i:i + 1, :]
                qc = jnp.sum(qt[:, i:i + 1] * c_prev, axis=0, keepdims=True)
                hbs_ref[i:i + 1, :] = (d_i * qc + sd["s"][i:i + 1, :] * v_i) * sd["inv"][i:i + 1, :]
                csout_ref[i] = d_i * c_prev + (sd["w"][i:i + 1, :] * kt[:, i:i + 1]) * v_i

        def step_merge():
            b_out = jax.nn.sigmoid(zs_ref[srows, _O:_O + DV]) * hbs_ref[...]
            mixs_ref[...] = (jax.nn.sigmoid(zs_ref[srows, _GA:_GA + DV]) * aouts_ref[srows, :]
                             + jax.nn.sigmoid(zs_ref[srows, _GB:_GB + DV]) * b_out)

        after[0] += [step_gates, functools.partial(step_rows, 0, bb // 2)]
        after[5] += [functools.partial(step_rows, bb // 2, bb)]
        after[6] = [step_merge] + after[6]

    for j in range(HEAD_COLS // DV):
        proj(j)
        for fn in after.get(j, []):
            fn()

    @pl.when(rb == tiles_per_batch - 1)
    def _():
        cout_ref[...] = c_ref[...]
        nout_ref[...] = n_ref[...]
        mout_ref[...] = m_ref[...]
        scout_ref[...] = cu_ref[HALO + n_valid - 2:HALO + n_valid, :]

    cu_ref[0:HALO, :] = cu_ref[tm:tm + HALO, :]


def _mixers(hn, wp, gates, wsc, c0, n0, m0, sc0, step=None, *, n_batch, tm, n_valid):
    t = hn.shape[0]
    n_tiles = t // tm
    tpb = n_tiles // n_batch
    in_specs = [pl.BlockSpec((tm, D_MODEL), lambda h, r: (r, 0)),
                pl.BlockSpec((D_MODEL, HEAD_COLS), lambda h, r: (0, h)),
                pl.BlockSpec((None, 2, tm), lambda h, r: (h, 0, r)),
                pl.BlockSpec((CONV_W, DV), lambda h, r: (0, h)),
                pl.BlockSpec((None, DK, DV), lambda h, r: (h, 0, 0)),
                pl.BlockSpec((None, 1, DK), lambda h, r: (h, 0, 0)),
                pl.BlockSpec((None, 1, DK), lambda h, r: (h, 0, 0)),
                pl.BlockSpec((HALO, DV), lambda h, r: (0, h))]
    out_specs = [pl.BlockSpec((tm, DV), lambda h, r: (r, h)),
                 pl.BlockSpec((None, None, DK, DV), lambda h, r: (r // tpb, h, 0, 0)),
                 pl.BlockSpec((None, None, 1, DK), lambda h, r: (r // tpb, h, 0, 0)),
                 pl.BlockSpec((None, None, 1, DK), lambda h, r: (r // tpb, h, 0, 0)),
                 pl.BlockSpec((None, 2, DV), lambda h, r: (r // tpb, 0, h))]
    out_shape = [jax.ShapeDtypeStruct((t, D_MODEL), BF16),
                 jax.ShapeDtypeStruct((n_batch, N_HEADS, DK, DV), F32),
                 jax.ShapeDtypeStruct((n_batch, N_HEADS, 1, DK), F32),
                 jax.ShapeDtypeStruct((n_batch, N_HEADS, 1, DK), F32),
                 jax.ShapeDtypeStruct((n_batch, 2, D_CONV), F32)]
    scratch = [pltpu.VMEM((tm, HEAD_COLS), F32),
               pltpu.VMEM((tm + HALO, DV), F32),
               pltpu.VMEM((DK, DV), F32),
               pltpu.VMEM((1, DK), F32),
               pltpu.VMEM((1, DK), F32),
               pltpu.VMEM((tm, CHUNK), F32),
               pltpu.VMEM((tm, CHUNK), F32),
               pltpu.VMEM((tm, DV), F32),
               pltpu.VMEM((tm // CHUNK, DK, DV), F32),
               pltpu.VMEM((tm, DV), F32),
               pltpu.VMEM((tm, DV), F32)]
    step_bb = 0
    if step is not None:
        nb = step[0].shape[0]
        assert nb % n_tiles == 0
        step_bb = nb // n_tiles
        in_specs += [pl.BlockSpec((nb, D_MODEL), lambda h, r: (0, 0)),
                     pl.BlockSpec((nb, 128), lambda h, r: (0, 0)),
                     pl.BlockSpec((nb, 128), lambda h, r: (0, 0)),
                     pl.BlockSpec((nb, DV), lambda h, r: (0, h)),
                     pl.BlockSpec((nb, DV), lambda h, r: (0, h)),
                     pl.BlockSpec((step_bb, None, DK, DV), lambda h, r: (r, h, 0, 0)),
                     pl.BlockSpec((None, step_bb, DK), lambda h, r: (h, r, 0))]
        out_specs += [pl.BlockSpec((step_bb, DV), lambda h, r: (r, h)),
                      pl.BlockSpec((step_bb, None, DK, DV), lambda h, r: (r, h, 0, 0)),
                      pl.BlockSpec((None, step_bb, DK), lambda h, r: (h, r, 0)),
                      pl.BlockSpec((None, step_bb, DK), lambda h, r: (h, r, 0)),
                      pl.BlockSpec((nb, DV), lambda h, r: (0, h))]
        out_shape += [jax.ShapeDtypeStruct((nb, D_MODEL), F32),
                      jax.ShapeDtypeStruct((nb, N_HEADS, DK, DV), F32),
                      jax.ShapeDtypeStruct((N_HEADS, nb, DK), F32),
                      jax.ShapeDtypeStruct((N_HEADS, nb, DK), F32),
                      jax.ShapeDtypeStruct((nb, D_CONV), F32)]
        scratch += [pltpu.VMEM((nb, HEAD_COLS), F32),
                    pltpu.VMEM((nb, DV), F32),
                    pltpu.VMEM((n_tiles, DK, step_bb), F32),
                    pltpu.VMEM((n_tiles, DK, step_bb), F32),
                    pltpu.VMEM((step_bb, DV), F32)]
    return pl.pallas_call(
        functools.partial(_mixers_kernel, tm=tm, tiles_per_batch=tpb, n_valid=n_valid, step_bb=step_bb),
        grid=(N_HEADS, n_tiles),
        in_specs=in_specs,
        out_specs=out_specs,
        out_shape=out_shape,
        scratch_shapes=scratch,
        compiler_params=_params(2),
        name="mixers",
    )(hn, wp, gates, wsc, c0, n0, m0, sc0, *(step or ()))


def _mixer_residual(x_ref, mix_ref, wo_ref, g2_ref, acc_ref, h2_ref):
    mix = mix_ref[...]
    if mix.dtype != BF16:
        mix = mix.astype(BF16)
    acc_ref[...] = x_ref[...] + jnp.dot(mix, wo_ref[...], preferred_element_type=F32)
    rows = acc_ref.shape[0]
    for lo in range(0, rows, FFN_SUB):
        hi = min(lo + FFN_SUB, rows)
        h2_ref[lo:hi, :] = _rms(acc_ref[lo:hi, :], g2_ref[...]).astype(BF16)


def _final_norm(acc_ref, g_ref, y_ref):
    rows = y_ref.shape[0]
    for lo in range(0, rows, FFN_SUB):
        hi = min(lo + FFN_SUB, rows)
        y_ref[lo:hi, :] = _rms(acc_ref[lo:hi, :], g_ref[...])


def _ffn_seq_kernel(x_ref, mix_ref, wo_ref, g2_ref, wua_ref, wub_ref, wca_ref, wcb_ref, wd_ref, g_ref,
                    ha0_ref, hb0_ref,
                    y_ref, fca_ref, fcb_ref,
                    acc_ref, h2_ref, ua_ref, ub_ref, ha_ref, hb_ref, act_ref,
                    *, tm, tiles_per_batch, n_valid, n_f):
    r = pl.program_id(0)
    f = pl.program_id(1)
    rb = r % tiles_per_batch

    @pl.when((r == 0) & (f == 0))
    def _():
        ha_ref[...] = jnp.zeros_like(ha_ref)
        hb_ref[...] = jnp.zeros_like(hb_ref)

    @pl.when(f == 0)
    def _():
        _mixer_residual(x_ref, mix_ref, wo_ref, g2_ref, acc_ref, h2_ref)

    def up_proj(wu_ref, u_ref, halo_ref, halo0_ref, tail_ref):
        u_ref[HALO:tm + HALO, :] = jnp.dot(h2_ref[...], wu_ref[...], preferred_element_type=F32)
        u_ref[0:HALO, :] = jnp.where(rb == 0, halo0_ref[...], halo_ref[f])
        halo_ref[f] = u_ref[tm:tm + HALO, :]
        tail_ref[f] = u_ref[HALO + n_valid - 2:HALO + n_valid, :]

    up_proj(wua_ref, ua_ref, ha_ref, ha0_ref, fca_ref)
    up_proj(wub_ref, ub_ref, hb_ref, hb0_ref, fcb_ref)

    def conv(u_ref, wc_ref, lo):
        return (u_ref[HALO + lo:HALO + lo + FFN_SUB, :] * wc_ref[2:3, :]
                + u_ref[HALO - 1 + lo:HALO - 1 + lo + FFN_SUB, :] * wc_ref[1:2, :]
                + u_ref[HALO - 2 + lo:HALO - 2 + lo + FFN_SUB, :] * wc_ref[0:1, :])

    for sb in range(tm // FFN_SUB):
        lo = sb * FFN_SUB
        up_a = conv(ua_ref, wca_ref, lo)
        up_b = conv(ub_ref, wcb_ref, lo)
        act_ref[lo:lo + FFN_SUB, :] = (up_b * jax.nn.sigmoid(up_b) * up_a).astype(BF16)
    acc_ref[...] += jnp.dot(act_ref[...], wd_ref[...], preferred_element_type=F32)

    @pl.when(f == n_f - 1)
    def _():
        _final_norm(acc_ref, g_ref, y_ref)


def _ffn_seq(x, mix, wo, g2, wua, wub, wc, wd, g, halo_a, halo_b, *, n_batch, tm, tf, n_valid):
    t = x.shape[0]
    tpb = t // n_batch // tm
    n_f = D_FF // tf
    return pl.pallas_call(
        functools.partial(_ffn_seq_kernel, tm=tm, tiles_per_batch=tpb, n_valid=n_valid, n_f=n_f),
        grid=(t // tm, n_f),
        in_specs=[pl.BlockSpec((tm, D_MODEL), lambda r, f: (r, 0)),
                  pl.BlockSpec((tm, D_MODEL), lambda r, f: (r, 0)),
                  pl.BlockSpec((D_MODEL, D_MODEL), lambda r, f: (0, 0), pipeline_mode=pl.Buffered(1)),
                  pl.BlockSpec((1, D_MODEL), lambda r, f: (0, 0)),
                  pl.BlockSpec((D_MODEL, tf), lambda r, f: (0, f)),
                  pl.BlockSpec((D_MODEL, tf), lambda r, f: (0, f)),
                  pl.BlockSpec((CONV_W, tf), lambda r, f: (0, f)),
                  pl.BlockSpec((CONV_W, tf), lambda r, f: (0, n_f + f)),
                  pl.BlockSpec((tf, D_MODEL), lambda r, f: (f, 0)),
                  pl.BlockSpec((1, D_MODEL), lambda r, f: (0, 0)),
                  pl.BlockSpec((HALO, tf), lambda r, f: (0, f)),
                  pl.BlockSpec((HALO, tf), lambda r, f: (0, f))],
        out_specs=[pl.BlockSpec((tm, D_MODEL), lambda r, f: (r, 0)),
                   pl.BlockSpec((None, n_f, 2, tf), lambda r, f: (r // tpb, 0, 0, 0)),
                   pl.BlockSpec((None, n_f, 2, tf), lambda r, f: (r // tpb, 0, 0, 0))],
        out_shape=[jax.ShapeDtypeStruct((t, D_MODEL), F32),
                   jax.ShapeDtypeStruct((n_batch, n_f, 2, tf), F32),
                   jax.ShapeDtypeStruct((n_batch, n_f, 2, tf), F32)],
        scratch_shapes=[pltpu.VMEM((tm, D_MODEL), F32),
                        pltpu.VMEM((tm, D_MODEL), BF16),
                        pltpu.VMEM((tm + HALO, tf), F32),
                        pltpu.VMEM((tm + HALO, tf), F32),
                        pltpu.VMEM((n_f, HALO, tf), F32),
                        pltpu.VMEM((n_f, HALO, tf), F32),
                        pltpu.VMEM((tm, tf), BF16)],
        compiler_params=_params(2),
        name="ffn_seq",
    )(x, mix, wo, g2, wua, wub, wc, wc, wd, g, halo_a, halo_b)


def _ffn_small_kernel(x_ref, mix_ref, wo_ref, g2_ref, wua_ref, wub_ref, wca_ref, wcb_ref, wd_ref, g_ref,
                      s0a_ref, s1a_ref, s0b_ref, s1b_ref,
                      y_ref, upa_ref, upb_ref, taila_ref, tailb_ref, wua_bf_ref, wub_bf_ref, wd_bf_ref,
                      acc_ref, h2_ref, *, n_f, n_step, n_tail):
    f = pl.program_id(0)

    @pl.when(f == 0)
    def _():
        _mixer_residual(x_ref, mix_ref, wo_ref, g2_ref, acc_ref, h2_ref)

    def half(wu_ref, wu_bf_ref, wc_ref, s0_ref, s1_ref, up_ref, tail_ref):
        w = wu_ref[...].astype(BF16)
        wu_bf_ref[...] = w
        up = jnp.dot(h2_ref[...], w, preferred_element_type=F32)
        up_ref[...] = up[0:n_step, :]
        tail_ref[f] = up[n_step + n_tail - 2:n_step + n_tail, :]
        return s0_ref[...] * wc_ref[0:1, :] + s1_ref[...] * wc_ref[1:2, :] + up[0:n_step, :] * wc_ref[2:3, :]

    up_a = half(wua_ref, wua_bf_ref, wca_ref, s0a_ref, s1a_ref, upa_ref, taila_ref)
    up_b = half(wub_ref, wub_bf_ref, wcb_ref, s0b_ref, s1b_ref, upb_ref, tailb_ref)
    act = (up_b * jax.nn.sigmoid(up_b) * up_a).astype(BF16)
    wd = wd_ref[...].astype(BF16)
    wd_bf_ref[...] = wd
    acc_ref[0:n_step, :] += jnp.dot(act, wd, preferred_element_type=F32)

    @pl.when(f == n_f - 1)
    def _():
        _final_norm(acc_ref, g_ref, y_ref)


def _ffn_small(x, mix, wo, g2, wu, wc, wd, g, s0, s1, *, tf, n_step):
    rows = x.shape[0]
    n_f = D_FF // tf
    lo = lambda f: (0, f)
    hi = lambda f: (0, n_f + f)
    return pl.pallas_call(
        functools.partial(_ffn_small_kernel, n_f=n_f, n_step=n_step, n_tail=rows - n_step),
        grid=(n_f,),
        in_specs=[pl.BlockSpec((rows, D_MODEL), lambda f: (0, 0)),
                  pl.BlockSpec((rows, D_MODEL), lambda f: (0, 0)),
                  pl.BlockSpec((D_MODEL, D_MODEL), lambda f: (0, 0), pipeline_mode=pl.Buffered(1)),
                  pl.BlockSpec((1, D_MODEL), lambda f: (0, 0)),
                  pl.BlockSpec((D_MODEL, tf), lo),
                  pl.BlockSpec((D_MODEL, tf), hi),
                  pl.BlockSpec((CONV_W, tf), lo),
                  pl.BlockSpec((CONV_W, tf), hi),
                  pl.BlockSpec((tf, D_MODEL), lambda f: (f, 0)),
                  pl.BlockSpec((1, D_MODEL), lambda f: (0, 0)),
                  pl.BlockSpec((n_step, tf), lo),
                  pl.BlockSpec((n_step, tf), lo),
                  pl.BlockSpec((n_step, tf), hi),
                  pl.BlockSpec((n_step, tf), hi)],
        out_specs=[pl.BlockSpec((n_step, D_MODEL), lambda f: (0, 0)),
                   pl.BlockSpec((n_step, tf), lo),
                   pl.BlockSpec((n_step, tf), lo),
                   pl.BlockSpec((n_f, 2, tf), lambda f: (0, 0, 0)),
                   pl.BlockSpec((n_f, 2, tf), lambda f: (0, 0, 0)),
                   pl.BlockSpec((D_MODEL, tf), lo),
                   pl.BlockSpec((D_MODEL, tf), lo),
                   pl.BlockSpec((tf, D_MODEL), lambda f: (f, 0))],
        out_shape=[jax.ShapeDtypeStruct((n_step, D_MODEL), F32),
                   jax.ShapeDtypeStruct((n_step, D_FF), F32),
                   jax.ShapeDtypeStruct((n_step, D_FF), F32),
                   jax.ShapeDtypeStruct((n_f, 2, tf), F32),
                   jax.ShapeDtypeStruct((n_f, 2, tf), F32),
                   jax.ShapeDtypeStruct((D_MODEL, D_FF), BF16),
                   jax.ShapeDtypeStruct((D_MODEL, D_FF), BF16),
                   jax.ShapeDtypeStruct((D_FF, D_MODEL), BF16)],
        scratch_shapes=[pltpu.VMEM((rows, D_MODEL), F32),
                        pltpu.VMEM((rows, D_MODEL), BF16)],
        compiler_params=_params(1),
        name="ffn_small",
    )(x, mix, wo, g2, wu, wu, wc, wc, wd, g, s0, s1, s0, s1)


def _tail_to_halo(tail):
    return jnp.pad(tail, ((HALO - 2, 0), (0, 0)))


def _join_tail_blocks(t):
    b, n_f, _, tf = t.shape
    return t.transpose(0, 2, 1, 3).reshape(b, 2, n_f * tf)


def kernel(x_prompt, x_sample, state_mlstm_C, state_mlstm_n, state_mlstm_m, state_shortconv,
           state_ffnconv, meta_tokens, norm1_g, w_in, b_if, w_shortconv, w_out, norm2_g,
           w_up, w_ffconv, w_down, norm_f_g):
    assert w_in.shape[0] == 1, "single-layer trunk"
    n_batch, seq, _ = x_prompt.shape
    n_dec = x_sample.shape[0]

    wp, wg = _prep_win(w_in[0].T, tk=1024)
    bif = jnp.pad(b_if[0], (0, 128 - 2 * N_HEADS)).reshape(1, 128)
    wo = w_out[0].astype(BF16)
    wsc = w_shortconv[0]
    wfc = w_ffconv[0]
    g1 = norm1_g[0].reshape(1, D_MODEL)
    g2 = norm2_g[0].reshape(1, D_MODEL)
    gf = norm_f_g.reshape(1, D_MODEL)

    xm = jnp.pad(meta_tokens, ((0, CHUNK - N_META), (0, 0)))
    hn_m, gates_m = _norm_gates_seq(xm, g1, wg, bif, tm=CHUNK, n_valid=N_META)
    zc = jnp.zeros((N_HEADS, DK, DV), F32)
    zn = jnp.zeros((N_HEADS, 1, DK), F32)
    mix_m, c_m, n_m, m_m, sc_m = _mixers(
        hn_m, wp, gates_m, wsc, zc, zn, zn, jnp.zeros((HALO, D_CONV), F32),
        n_batch=1, tm=CHUNK, n_valid=N_META)

    xp = x_prompt.reshape(n_batch * seq, D_MODEL)
    tm = 512
    hn_p, gates_p = _norm_gates_seq(xp, g1, wg, bif, tm=tm, n_valid=CHUNK)
    xs = x_sample.reshape(n_dec, D_MODEL)
    hn_s, gif_s = _norm_gates_step(xs, g1, wg, bif)
    mprev = jnp.pad(state_mlstm_m[0], ((0, 0), (0, 128 - N_HEADS)))
    n_hb = state_mlstm_n[0].transpose(1, 0, 2)
    mix_p, c_p, n_p, m_p, sc_p, mix_s, c_s, n_s, m_s, cu_s = _mixers(
        hn_p, wp, gates_p, wsc, c_m[0], n_m[0], m_m[0], _tail_to_halo(sc_m[0]),
        (hn_s, gif_s, mprev, state_shortconv[0, :, 0, :], state_shortconv[0, :, 1, :],
         state_mlstm_C[0], n_hb),
        n_batch=n_batch, tm=2 * tm, n_valid=2 * tm)

    x_sm = jnp.concatenate([xs, meta_tokens], axis=0)
    mix_sm = jnp.concatenate([mix_s, mix_m[:N_META].astype(F32)], axis=0)
    y_s, upa_s, upb_s, fca_m, fcb_m, wua, wub, wd = _ffn_small(
        x_sm, mix_sm, wo, g2, w_up[0], wfc, w_down[0], gf,
        state_ffnconv[0, :, 0, :], state_ffnconv[0, :, 1, :], tf=FFN_TF, n_step=n_dec)
    y_p, fca_p, fcb_p = _ffn_seq(xp, mix_p, wo, g2, wua, wub, wfc, wd, gf,
                                 _tail_to_halo(_join_tail_blocks(fca_m[None])[0]),
                                 _tail_to_halo(_join_tail_blocks(fcb_m[None])[0]),
                                 n_batch=n_batch, tm=tm, tf=FFN_TF, n_valid=tm)

    return (y_p.reshape(n_batch, seq, D_MODEL),
            y_s.reshape(n_dec, 1, D_MODEL),
            c_p[None],
            n_p.reshape(1, n_batch, N_HEADS, DK),
            m_p[:, :, 0, 0][None],
            sc_p[None],
            jnp.concatenate([_join_tail_blocks(fca_p), _join_tail_blocks(fcb_p)], axis=-1)[None],
            c_s[None],
            n_s.transpose(1, 0, 2)[None],
            m_s[:, :, 0].T[None],
            jnp.stack([state_shortconv[0, :, 1, :], cu_s], axis=1)[None],
            jnp.stack([state_ffnconv[0, :, 1, :], jnp.concatenate([upa_s, upb_s], axis=-1)], axis=1)[None])
```

```python
import functools

import jax
import jax.numpy as jnp
from jax import lax
from jax.experimental import pallas as pl
from jax.experimental.pallas import tpu as pltpu

D_MODEL = 2048
N_META = 16
N_HEADS = 8
DK = 128
DV = 256
D_CONV = D_MODEL
CONV_W = 3
D_FF = 5632
CHUNK = 128
EPS = 1e-6
K_SCALE = DK ** -0.5
NEG = -1e30
HEAD_COLS = 7 * DV + 2 * DK
V7X_VMEM_LIMIT = 56 * 1024 * 1024
HALO = 8
FFN_SUB = 64
FFN_TF = 512
DOWN_GROUPS = 2
F32 = jnp.float32
BF16 = jnp.bfloat16

_Q, _K, _V, _O, _GB, _U, _CG, _BG, _GA = (0, 128, 256, 512, 768, 1024, 1280, 1536, 1792)


def _params(n_grid):
    return pltpu.CompilerParams(dimension_semantics=("arbitrary",) * n_grid,
                                vmem_limit_bytes=V7X_VMEM_LIMIT)


def _log_sigmoid(x):
    return jnp.minimum(x, 0.0) - jnp.log1p(jnp.exp(-jnp.abs(x)))


def _rms(x, g):
    return x * lax.rsqrt(jnp.mean(x * x, axis=-1, keepdims=True) + EPS) * g


def _prep_win_kernel(u_ref, cg_ref, bg_ref, ga_ref, v_ref, o_ref, gb_ref, q_ref, k_ref, gif_ref,
                     wp_ref, wg_ref):
    for ref, off, d in ((u_ref, _U, DV), (cg_ref, _CG, DV), (bg_ref, _BG, DV), (ga_ref, _GA, DV),
                        (v_ref, _V, DV), (o_ref, _O, DV), (gb_ref, _GB, DV), (q_ref, _Q, DK), (k_ref, _K, DK)):
        wp_ref[:, off:off + d] = ref[...].T.astype(BF16)
    gif = gif_ref[...].T
    lane = lax.broadcasted_iota(jnp.int32, gif.shape, 1)
    wg_ref[...] = jnp.where(lane < 2 * N_HEADS, gif, 0.0).astype(BF16)


def _prep_win(wt, *, tk):
    o_q = 3 * D_CONV
    o_k = o_q + N_HEADS * DK
    o_v = o_k + N_HEADS * DK
    o_o = o_v + N_HEADS * DV
    o_gif = o_o + N_HEADS * DV
    o_ga = o_gif + 2 * N_HEADS
    o_gb = o_ga + D_CONV
    assert o_ga % 8 == 0
    rows = lambda off, d: pl.BlockSpec((pl.Element(d), pl.Element(tk)),
                                       lambda i, h: (pl.multiple_of(off + d * h, 8), pl.multiple_of(i * tk, 128)))
    return pl.pallas_call(
        _prep_win_kernel,
        grid=(D_MODEL // tk, N_HEADS),
        in_specs=[rows(0, DV), rows(D_CONV, DV), rows(2 * D_CONV, DV), rows(o_ga, DV), rows(o_v, DV),
                  rows(o_o, DV), rows(o_gb, DV), rows(o_q, DK), rows(o_k, DK),
                  pl.BlockSpec((pl.Element(128), pl.Element(tk)),
                               lambda i, h: (o_gif, pl.multiple_of(i * tk, 128)))],
        out_specs=[pl.BlockSpec((tk, HEAD_COLS), lambda i, h: (i, h)),
                   pl.BlockSpec((tk, 128), lambda i, h: (i, 0))],
        out_shape=[jax.ShapeDtypeStruct((D_MODEL, N_HEADS * HEAD_COLS), BF16),
                   jax.ShapeDtypeStruct((D_MODEL, 128), BF16)],
        compiler_params=_params(2),
        name="prep_win",
    )(*([wt] * 10))


def _norm_gates_seq_kernel(x_ref, g_ref, wg_ref, bif_ref, hn_ref, gates_ref, *, tm, n_valid):
    hn = _rms(x_ref[...], g_ref[...]).astype(BF16)
    hn_ref[...] = hn
    gif = jnp.dot(hn, wg_ref[...], preferred_element_type=F32) + bif_ref[...]
    row = lax.broadcasted_iota(jnp.int32, (CHUNK, CHUNK), 0)
    col = lax.broadcasted_iota(jnp.int32, (CHUNK, CHUNK), 1)
    upper = (row <= col).astype(F32)
    for c in range(tm // CHUNK):
        gt = gif[c * CHUNK:(c + 1) * CHUNK, :].T
        ig = gt[0:N_HEADS, :]
        lf = _log_sigmoid(gt[N_HEADS:2 * N_HEADS, :])
        if n_valid < CHUNK:
            tok = lax.broadcasted_iota(jnp.int32, (N_HEADS, CHUNK), 1)
            ig = jnp.where(tok < n_valid, ig, NEG)
            lf = jnp.where(tok < n_valid, lf, 0.0)
        b = jnp.dot(lf, upper, preferred_element_type=F32, precision=lax.Precision.HIGHEST)
        a = ig - b
        for j in range(N_HEADS):
            gates_ref[j, 0:1, c * CHUNK:(c + 1) * CHUNK] = a[j:j + 1, :]
            gates_ref[j, 1:2, c * CHUNK:(c + 1) * CHUNK] = b[j:j + 1, :]


def _norm_gates_seq(x, g, wg, bif, *, tm, n_valid):
    t = x.shape[0]
    return pl.pallas_call(
        functools.partial(_norm_gates_seq_kernel, tm=tm, n_valid=n_valid),
        grid=(t // tm,),
        in_specs=[pl.BlockSpec((tm, D_MODEL), lambda r: (r, 0)),
                  pl.BlockSpec((1, D_MODEL), lambda r: (0, 0)),
                  pl.BlockSpec((D_MODEL, 128), lambda r: (0, 0)),
                  pl.BlockSpec((1, 128), lambda r: (0, 0))],
        out_specs=[pl.BlockSpec((tm, D_MODEL), lambda r: (r, 0)),
                   pl.BlockSpec((N_HEADS, 2, tm), lambda r: (0, 0, r))],
        out_shape=[jax.ShapeDtypeStruct((t, D_MODEL), BF16),
                   jax.ShapeDtypeStruct((N_HEADS, 2, t), F32)],
        compiler_params=_params(1),
        name="norm_gates_seq",
    )(x, g, wg, bif)


def _norm_gates_step_kernel(x_ref, g_ref, wg_ref, bif_ref, hn_ref, gif_ref):
    hn = _rms(x_ref[...], g_ref[...]).astype(BF16)
    hn_ref[...] = hn
    gif_ref[...] = jnp.dot(hn, wg_ref[...], preferred_element_type=F32) + bif_ref[...]


def _norm_gates_step(x, g, wg, bif):
    t = x.shape[0]
    return pl.pallas_call(
        _norm_gates_step_kernel,
        grid=(1,),
        in_specs=[pl.BlockSpec((t, D_MODEL), lambda r: (0, 0)),
                  pl.BlockSpec((1, D_MODEL), lambda r: (0, 0)),
                  pl.BlockSpec((D_MODEL, 128), lambda r: (0, 0)),
                  pl.BlockSpec((1, 128), lambda r: (0, 0))],
        out_specs=[pl.BlockSpec((t, D_MODEL), lambda r: (0, 0)),
                   pl.BlockSpec((t, 128), lambda r: (0, 0))],
        out_shape=[jax.ShapeDtypeStruct((t, D_MODEL), BF16),
                   jax.ShapeDtypeStruct((t, 128), F32)],
        compiler_params=_params(1),
        name="norm_gates_step",
    )(x, g, wg, bif)


def _mixers_kernel(*refs, tm, tiles_per_batch, n_valid, step_bb):
    n_seq = (8, 5, 11)
    n_step = (9, 7, 5) if step_bb else (0, 0, 0)
    cuts = [0]
    for a, b in zip(n_seq, n_step):
        cuts += [cuts[-1] + a, cuts[-1] + a + b]
    seq_in, step_in, seq_out, step_out, seq_scr, step_scr = (refs[i:j] for i, j in zip(cuts[:-1], cuts[1:]))
    hn_ref, w_ref, gates_ref, wsc_ref, c0_ref, n0_ref, m0_ref, sc0_ref = seq_in
    mix_ref, cout_ref, nout_ref, mout_ref, scout_ref = seq_out
    z_ref, cu_ref, c_ref, n_ref, m_ref, dmat_ref, qk_ref, sv_ref, kv_ref, bmix_ref, amix_ref = seq_scr
    head = pl.program_id(0)
    tile = pl.program_id(1)
    rb = tile % tiles_per_batch
    nc = tm // CHUNK

    @pl.when(rb == 0)
    def _():
        c_ref[...] = c0_ref[...]
        n_ref[...] = n0_ref[...]
        m_ref[...] = m0_ref[...]
        cu_ref[0:HALO, :] = sc0_ref[...]

    row = lax.broadcasted_iota(jnp.int32, (CHUNK, CHUNK), 0)
    col = lax.broadcasted_iota(jnp.int32, (CHUNK, CHUNK), 1)
    causal = col <= row
    eye = col == row
    st = [dict() for _ in range(nc)]
    rows_of = lambda c: slice(c * CHUNK, (c + 1) * CHUNK)
    k_of = lambda rows: z_ref[rows, _K:_K + DK] * K_SCALE

    def proj(j):
        cols = slice(j * DV, (j + 1) * DV)
        z_ref[:, cols] = jnp.dot(hn_ref[...], w_ref[:, cols], preferred_element_type=F32)

    def gate_terms(c):
        rows, d = rows_of(c), st[c]
        a_row = gates_ref[0:1, rows]
        b_row = gates_ref[1:2, rows]
        d["a_col"] = jnp.sum(jnp.where(eye, a_row, 0.0), axis=-1, keepdims=True)
        d["b_col"] = jnp.sum(jnp.where(eye, b_row, 0.0), axis=-1, keepdims=True)
        dmat = jnp.where(causal, d["b_col"] + a_row, NEG)
        dmat_ref[rows, :] = dmat
        d["mloc"] = jnp.max(dmat, axis=-1, keepdims=True)
        qk_ref[rows, :] = lax.dot_general(z_ref[rows, _Q:_Q + DK].astype(BF16), k_of(rows).astype(BF16),
                                          (((1,), (1,)), ((), ())), preferred_element_type=F32)

    def stabilisers():
        m_prev = m_ref[:, 0:1]
        for c in range(nc):
            d = st[c]
            d["m_prev"] = m_prev
            d["bm"] = d["b_col"] + m_prev
            d["m_t"] = jnp.maximum(d["bm"], d["mloc"])
            d["m_last"] = d["m_t"][CHUNK - 1:CHUNK, :]
            m_prev = d["m_last"]
        m_ref[...] = jnp.broadcast_to(m_prev, (1, DK))

    def intra(c):
        rows, d = rows_of(c), st[c]
        k = k_of(rows)
        v = z_ref[rows, _V:_V + DV]
        s = qk_ref[rows, :] * jnp.exp(dmat_ref[rows, :] - d["m_t"])
        sv_ref[rows, :] = jnp.dot(s.astype(BF16), v.astype(BF16), preferred_element_type=F32)
        d["ssum"] = jnp.sum(s, axis=-1, keepdims=True)
        d["decay"] = jnp.exp(d["bm"] - d["m_t"])
        d["hden"] = jnp.exp(-d["m_t"])
        b_last = d["b_col"][CHUNK - 1:CHUNK, :]
        g_col = jnp.exp(b_last + d["a_col"] - d["m_last"])
        d["carry"] = jnp.exp(b_last + d["m_prev"] - d["m_last"])
        kv_ref[c] = jnp.dot(k.T.astype(BF16), (g_col * v).astype(BF16), preferred_element_type=F32)
        d["kn"] = jnp.sum(g_col * k, axis=0, keepdims=True)

    def inter(c):
        rows, d = rows_of(c), st[c]
        q = z_ref[rows, _Q:_Q + DK]
        c_prev = c_ref[...]
        n_prev = n_ref[...]
        num = (d["decay"] * jnp.dot(q.astype(BF16), c_prev.astype(BF16), preferred_element_type=F32)
               + sv_ref[rows, :])
        den = d["decay"] * jnp.sum(q * n_prev, axis=-1, keepdims=True) + d["ssum"]
        hb = num / jnp.maximum(jnp.abs(den), d["hden"])
        b_out = jax.nn.sigmoid(z_ref[rows, _O:_O + DV]) * hb
        bmix_ref[rows, :] = jax.nn.sigmoid(z_ref[rows, _GB:_GB + DV]) * b_out
        c_ref[...] = d["carry"] * c_prev + kv_ref[c]
        n_ref[...] = d["carry"] * n_prev + d["kn"]

    def conv(c):
        lo = c * CHUNK
        rows = rows_of(c)
        cu = z_ref[rows, _CG:_CG + DV] * z_ref[rows, _U:_U + DV]
        cu_ref[HALO + lo:HALO + lo + CHUNK, :] = cu
        taps = (cu * wsc_ref[2:3, :] + cu_ref[HALO - 1 + lo:HALO - 1 + lo + CHUNK, :] * wsc_ref[1:2, :]
                + cu_ref[HALO - 2 + lo:HALO - 2 + lo + CHUNK, :] * wsc_ref[0:1, :])
        amix_ref[rows, :] = z_ref[rows, _BG:_BG + DV] * taps

    def merge(c):
        rows = rows_of(c)
        mix = jax.nn.sigmoid(z_ref[rows, _GA:_GA + DV]) * amix_ref[rows, :] + bmix_ref[rows, :]
        mix_ref[rows, :] = mix.astype(BF16)

    piece = lambda off: off // DV
    assert (piece(_Q), piece(_V), piece(_O), piece(_GB)) == (0, 1, 2, 3)
    assert sorted(map(piece, (_U, _CG, _BG))) == [4, 5, 6] and piece(_GA) == HEAD_COLS // DV - 1
    each = lambda fn: [functools.partial(fn, c) for c in range(nc)]
    half = (nc + 1) // 2
    after = {0: [],
             1: each(gate_terms)[:half],
             2: each(gate_terms)[half:] + [stabilisers],
             3: each(intra)[:half],
             4: each(intra)[half:] + each(inter)[:1],
             5: each(inter)[1:1 + half],
             6: each(inter)[1 + half:] + each(conv),
             7: each(merge)}

    if step_bb:
        bb = step_bb
        hns_ref, gifs_ref, mprevs_ref, s0_ref, s1_ref, cs_ref, ns_ref, wu32_ref, wd32_ref = step_in
        mixs_ref, csout_ref, nsout_ref, msout_ref, cuouts_ref, wu16_ref, wd16_ref = step_out
        zs_ref, aouts_ref, qts_ref, kts_ref, hbs_ref = step_scr

        def round_weights():
            wu16_ref[...] = wu32_ref[...].astype(BF16)
            wd16_ref[...] = wd32_ref[...].astype(BF16)

        @pl.when(tile == 0)
        def _():
            z = jnp.dot(hns_ref[...], w_ref[...], preferred_element_type=F32)
            zs_ref[...] = z
            cu = z[:, _CG:_CG + DV] * z[:, _U:_U + DV]
            taps = s0_ref[...] * wsc_ref[0:1, :] + s1_ref[...] * wsc_ref[1:2, :] + cu * wsc_ref[2:3, :]
            aouts_ref[...] = z[:, _BG:_BG + DV] * taps
            cuouts_ref[...] = cu
            qt = z[:, _Q:_Q + DK].T
            kt = (z[:, _K:_K + DK] * K_SCALE).T
            for jj in range(hns_ref.shape[0] // bb):
                qts_ref[jj] = qt[:, jj * bb:(jj + 1) * bb]
                kts_ref[jj] = kt[:, jj * bb:(jj + 1) * bb]

        srows = pl.ds(pl.multiple_of(tile * bb, bb), bb)
        sd = {}

        def step_gates():
            lane = lax.broadcasted_iota(jnp.int32, (bb, 128), 1)
            pick = lambda x, idx: jnp.sum(jnp.where(lane == idx, x, 0.0), axis=-1, keepdims=True)
            gif = gifs_ref[srows, :]
            ig = pick(gif, head)
            lf = _log_sigmoid(pick(gif, head + N_HEADS))
            m_prev = pick(mprevs_ref[srows, :], head)
            q = zs_ref[srows, _Q:_Q + DK]
            k = zs_ref[srows, _K:_K + DK] * K_SCALE
            n_prev = ns_ref[...]
            m_t = jnp.maximum(lf + m_prev, ig)
            sd["w"] = jnp.exp(ig - m_t)
            sd["decay"] = jnp.exp(lf + m_prev - m_t)
            sd["s"] = jnp.sum(q * k, axis=-1, keepdims=True) * sd["w"]
            den = sd["decay"] * jnp.sum(q * n_prev, axis=-1, keepdims=True) + sd["s"]
            sd["inv"] = 1.0 / jnp.maximum(jnp.abs(den), jnp.exp(-m_t))
            nsout_ref[...] = sd["decay"] * n_prev + sd["w"] * k
            msout_ref[...] = jnp.broadcast_to(m_t, (bb, DK))

        def step_rows(lo, hi):
            qt = qts_ref[tile]
            kt = kts_ref[tile]
            v = zs_ref[srows, _V:_V + DV]
            for i in range(lo, hi):
                c_prev = cs_ref[i]
                d_i = sd["decay"][i:i + 1, :]
                v_i = v[i:i + 1, :]
                qc = jnp.sum(qt[:, i:i + 1] * c_prev, axis=0, keepdims=True)
                hbs_ref[i:i + 1, :] = (d_i * qc + sd["s"][i:i + 1, :] * v_i) * sd["inv"][i:i + 1, :]
                csout_ref[i] = d_i * c_prev + (sd["w"][i:i + 1, :] * kt[:, i:i + 1]) * v_i

        def step_merge():
            b_out = jax.nn.sigmoid(zs_ref[srows, _O:_O + DV]) * hbs_ref[...]
            mixs_ref[...] = (jax.nn.sigmoid(zs_ref[srows, _GA:_GA + DV]) * aouts_ref[srows, :]
                             + jax.nn.sigmoid(zs_ref[srows, _GB:_GB + DV]) * b_out)

        after[0] += [round_weights, step_gates, functools.partial(step_rows, 0, bb // 2)]
        after[5] += [functools.partial(step_rows, bb // 2, bb)]
        after[6] = [step_merge] + after[6]

    for j in range(HEAD_COLS // DV):
        proj(j)
        for fn in after.get(j, []):
            fn()

    @pl.when(rb == tiles_per_batch - 1)
    def _():
        cout_ref[...] = c_ref[...]
        nout_ref[...] = n_ref[...]
        mout_ref[...] = m_ref[...]
        scout_ref[...] = cu_ref[HALO + n_valid - 2:HALO + n_valid, :]

    cu_ref[0:HALO, :] = cu_ref[tm:tm + HALO, :]


def _mixers(hn, wp, gates, wsc, c0, n0, m0, sc0, step=None, *, n_batch, tm, n_valid):
    t = hn.shape[0]
    n_tiles = t // tm
    tpb = n_tiles // n_batch
    in_specs = [pl.BlockSpec((tm, D_MODEL), lambda h, r: (r, 0)),
                pl.BlockSpec((D_MODEL, HEAD_COLS), lambda h, r: (0, h)),
                pl.BlockSpec((None, 2, tm), lambda h, r: (h, 0, r)),
                pl.BlockSpec((CONV_W, DV), lambda h, r: (0, h)),
                pl.BlockSpec((None, DK, DV), lambda h, r: (h, 0, 0)),
                pl.BlockSpec((None, 1, DK), lambda h, r: (h, 0, 0)),
                pl.BlockSpec((None, 1, DK), lambda h, r: (h, 0, 0)),
                pl.BlockSpec((HALO, DV), lambda h, r: (0, h))]
    out_specs = [pl.BlockSpec((tm, DV), lambda h, r: (r, h)),
                 pl.BlockSpec((None, None, DK, DV), lambda h, r: (r // tpb, h, 0, 0)),
                 pl.BlockSpec((None, None, 1, DK), lambda h, r: (r // tpb, h, 0, 0)),
                 pl.BlockSpec((None, None, 1, DK), lambda h, r: (r // tpb, h, 0, 0)),
                 pl.BlockSpec((None, 2, DV), lambda h, r: (r // tpb, 0, h))]
    out_shape = [jax.ShapeDtypeStruct((t, D_MODEL), BF16),
                 jax.ShapeDtypeStruct((n_batch, N_HEADS, DK, DV), F32),
                 jax.ShapeDtypeStruct((n_batch, N_HEADS, 1, DK), F32),
                 jax.ShapeDtypeStruct((n_batch, N_HEADS, 1, DK), F32),
                 jax.ShapeDtypeStruct((n_batch, 2, D_CONV), F32)]
    scratch = [pltpu.VMEM((tm, HEAD_COLS), F32),
               pltpu.VMEM((tm + HALO, DV), F32),
               pltpu.VMEM((DK, DV), F32),
               pltpu.VMEM((1, DK), F32),
               pltpu.VMEM((1, DK), F32),
               pltpu.VMEM((tm, CHUNK), F32),
               pltpu.VMEM((tm, CHUNK), F32),
               pltpu.VMEM((tm, DV), F32),
               pltpu.VMEM((tm // CHUNK, DK, DV), F32),
               pltpu.VMEM((tm, DV), F32),
               pltpu.VMEM((tm, DV), F32)]
    step_bb = 0
    if step is not None:
        nb = step[0].shape[0]
        assert nb % n_tiles == 0
        step_bb = nb // n_tiles
        in_specs += [pl.BlockSpec((nb, D_MODEL), lambda h, r: (0, 0)),
                     pl.BlockSpec((nb, 128), lambda h, r: (0, 0)),
                     pl.BlockSpec((nb, 128), lambda h, r: (0, 0)),
                     pl.BlockSpec((nb, DV), lambda h, r: (0, h)),
                     pl.BlockSpec((nb, DV), lambda h, r: (0, h)),
                     pl.BlockSpec((step_bb, None, DK, DV), lambda h, r: (r, h, 0, 0)),
                     pl.BlockSpec((None, step_bb, DK), lambda h, r: (h, r, 0))]
        n_steps = N_HEADS * n_tiles
        wu32, wd32 = step[7], step[8]
        assert wu32.shape[0] % (16 * n_steps) == 0 and wd32.shape[0] % (8 * n_steps) == 0
        wu_rows = wu32.shape[0] // n_steps
        wd_rows, wd_cols = 2 * wd32.shape[0] // n_steps, wd32.shape[1] // 2
        assert wd_rows % 16 == 0
        wu_spec = pl.BlockSpec((wu_rows, wu32.shape[1]), lambda h, r: (h * n_tiles + r, 0))
        wd_spec = pl.BlockSpec((wd_rows, wd_cols), lambda h, r: ((h * n_tiles + r) // 2, (h * n_tiles + r) % 2))
        in_specs += [wu_spec, wd_spec]
        out_specs += [pl.BlockSpec((step_bb, DV), lambda h, r: (r, h)),
                      pl.BlockSpec((step_bb, None, DK, DV), lambda h, r: (r, h, 0, 0)),
                      pl.BlockSpec((None, step_bb, DK), lambda h, r: (h, r, 0)),
                      pl.BlockSpec((None, step_bb, DK), lambda h, r: (h, r, 0)),
                      pl.BlockSpec((nb, DV), lambda h, r: (0, h)),
                      wu_spec, wd_spec]
        out_shape += [jax.ShapeDtypeStruct((nb, D_MODEL), F32),
                      jax.ShapeDtypeStruct((nb, N_HEADS, DK, DV), F32),
                      jax.ShapeDtypeStruct((N_HEADS, nb, DK), F32),
                      jax.ShapeDtypeStruct((N_HEADS, nb, DK), F32),
                      jax.ShapeDtypeStruct((nb, D_CONV), F32),
                      jax.ShapeDtypeStruct(wu32.shape, BF16),
                      jax.ShapeDtypeStruct(wd32.shape, BF16)]
        scratch += [pltpu.VMEM((nb, HEAD_COLS), F32),
                    pltpu.VMEM((nb, DV), F32),
                    pltpu.VMEM((n_tiles, DK, step_bb), F32),
                    pltpu.VMEM((n_tiles, DK, step_bb), F32),
                    pltpu.VMEM((step_bb, DV), F32)]
    return pl.pallas_call(
        functools.partial(_mixers_kernel, tm=tm, tiles_per_batch=tpb, n_valid=n_valid, step_bb=step_bb),
        grid=(N_HEADS, n_tiles),
        in_specs=in_specs,
        out_specs=out_specs,
        out_shape=out_shape,
        scratch_shapes=scratch,
        compiler_params=_params(2),
        name="mixers",
    )(hn, wp, gates, wsc, c0, n0, m0, sc0, *(step or ()))


def _mixer_residual(x_ref, mix_ref, wo_ref, g2_ref, acc_ref, h2_ref):
    mix = mix_ref[...]
    if mix.dtype != BF16:
        mix = mix.astype(BF16)
    acc_ref[...] = x_ref[...] + jnp.dot(mix, wo_ref[...], preferred_element_type=F32)
    rows = acc_ref.shape[0]
    for lo in range(0, rows, FFN_SUB):
        hi = min(lo + FFN_SUB, rows)
        h2_ref[lo:hi, :] = _rms(acc_ref[lo:hi, :], g2_ref[...]).astype(BF16)


def _final_norm(acc_ref, g_ref, y_ref):
    rows = y_ref.shape[0]
    for lo in range(0, rows, FFN_SUB):
        hi = min(lo + FFN_SUB, rows)
        y_ref[lo:hi, :] = _rms(acc_ref[lo:hi, :], g_ref[...])


def _ffn_seq_kernel(x_ref, mix_ref, wo_ref, g2_ref, wua_ref, wub_ref, wca_ref, wcb_ref, wd_ref, g_ref,
                    ha0_ref, hb0_ref,
                    y_ref, fca_ref, fcb_ref,
                    acc_ref, h2_ref, ua_ref, ub_ref, ha_ref, hb_ref, act_ref,
                    *, tm, tiles_per_batch, n_valid, n_f):
    r = pl.program_id(0)
    f = pl.program_id(1)
    rb = r % tiles_per_batch

    @pl.when((r == 0) & (f == 0))
    def _():
        ha_ref[...] = jnp.zeros_like(ha_ref)
        hb_ref[...] = jnp.zeros_like(hb_ref)

    @pl.when(f == 0)
    def _():
        _mixer_residual(x_ref, mix_ref, wo_ref, g2_ref, acc_ref, h2_ref)

    def up_proj(wu_ref, u_ref, halo_ref, halo0_ref, tail_ref):
        u_ref[HALO:tm + HALO, :] = jnp.dot(h2_ref[...], wu_ref[...], preferred_element_type=F32)
        u_ref[0:HALO, :] = jnp.where(rb == 0, halo0_ref[...], halo_ref[f])
        halo_ref[f] = u_ref[tm:tm + HALO, :]
        tail_ref[f] = u_ref[HALO + n_valid - 2:HALO + n_valid, :]

    up_proj(wua_ref, ua_ref, ha_ref, ha0_ref, fca_ref)
    up_proj(wub_ref, ub_ref, hb_ref, hb0_ref, fcb_ref)

    def conv(u_ref, wc_ref, lo):
        win = u_ref[lo:lo + HALO + FFN_SUB, :]
        back1 = pltpu.roll(win, 1, axis=0)[HALO:, :]
        back2 = pltpu.roll(win, 2, axis=0)[HALO:, :]
        return win[HALO:, :] * wc_ref[2:3, :] + back1 * wc_ref[1:2, :] + back2 * wc_ref[0:1, :]

    group = tm // DOWN_GROUPS
    for gi in range(DOWN_GROUPS):
        for lo in range(gi * group, (gi + 1) * group, FFN_SUB):
            up_a = conv(ua_ref, wca_ref, lo)
            up_b = conv(ub_ref, wcb_ref, lo)
            act_ref[lo:lo + FFN_SUB, :] = (up_b * jax.nn.sigmoid(up_b) * up_a).astype(BF16)
        rows = slice(gi * group, (gi + 1) * group)
        acc_ref[rows, :] += jnp.dot(act_ref[rows, :], wd_ref[...], preferred_element_type=F32)

    @pl.when(f == n_f - 1)
    def _():
        _final_norm(acc_ref, g_ref, y_ref)


def _ffn_seq(x, mix, wo, g2, wu, wc, wd, g, halo_a, halo_b, *, n_batch, tm, tf, n_valid):
    t = x.shape[0]
    tpb = t // n_batch // tm
    n_f = D_FF // tf
    return pl.pallas_call(
        functools.partial(_ffn_seq_kernel, tm=tm, tiles_per_batch=tpb, n_valid=n_valid, n_f=n_f),
        grid=(t // tm, n_f),
        in_specs=[pl.BlockSpec((tm, D_MODEL), lambda r, f: (r, 0)),
                  pl.BlockSpec((tm, D_MODEL), lambda r, f: (r, 0)),
                  pl.BlockSpec((D_MODEL, D_MODEL), lambda r, f: (0, 0), pipeline_mode=pl.Buffered(1)),
                  pl.BlockSpec((1, D_MODEL), lambda r, f: (0, 0)),
                  pl.BlockSpec((D_MODEL, tf), lambda r, f: (0, f)),
                  pl.BlockSpec((D_MODEL, tf), lambda r, f: (0, n_f + f)),
                  pl.BlockSpec((CONV_W, tf), lambda r, f: (0, f)),
                  pl.BlockSpec((CONV_W, tf), lambda r, f: (0, n_f + f)),
                  pl.BlockSpec((tf, D_MODEL), lambda r, f: (f, 0)),
                  pl.BlockSpec((1, D_MODEL), lambda r, f: (0, 0)),
                  pl.BlockSpec((HALO, tf), lambda r, f: (0, f)),
                  pl.BlockSpec((HALO, tf), lambda r, f: (0, f))],
        out_specs=[pl.BlockSpec((tm, D_MODEL), lambda r, f: (r, 0)),
                   pl.BlockSpec((None, n_f, 2, tf), lambda r, f: (r // tpb, 0, 0, 0)),
                   pl.BlockSpec((None, n_f, 2, tf), lambda r, f: (r // tpb, 0, 0, 0))],
        out_shape=[jax.ShapeDtypeStruct((t, D_MODEL), F32),
                   jax.ShapeDtypeStruct((n_batch, n_f, 2, tf), F32),
                   jax.ShapeDtypeStruct((n_batch, n_f, 2, tf), F32)],
        scratch_shapes=[pltpu.VMEM((tm, D_MODEL), F32),
                        pltpu.VMEM((tm, D_MODEL), BF16),
                        pltpu.VMEM((tm + HALO, tf), F32),
                        pltpu.VMEM((tm + HALO, tf), F32),
                        pltpu.VMEM((n_f, HALO, tf), F32),
                        pltpu.VMEM((n_f, HALO, tf), F32),
                        pltpu.VMEM((tm, tf), BF16)],
        compiler_params=_params(2),
        name="ffn_seq",
    )(x, mix, wo, g2, wu, wu, wc, wc, wd, g, halo_a, halo_b)


def _ffn_small_kernel(x_ref, mix_ref, wo_ref, g2_ref, wua_ref, wub_ref, wca_ref, wcb_ref, wd_ref, g_ref,
                      s0a_ref, s1a_ref, s0b_ref, s1b_ref,
                      y_ref, upa_ref, upb_ref, taila_ref, tailb_ref,
                      acc_ref, h2_ref, *, n_f, n_step, n_tail):
    f = pl.program_id(0)

    @pl.when(f == 0)
    def _():
        _mixer_residual(x_ref, mix_ref, wo_ref, g2_ref, acc_ref, h2_ref)

    def half(wu_ref, wc_ref, s0_ref, s1_ref, up_ref, tail_ref):
        up = jnp.dot(h2_ref[...], wu_ref[...], preferred_element_type=F32)
        up_ref[...] = up[0:n_step, :]
        tail_ref[f] = up[n_step + n_tail - 2:n_step + n_tail, :]
        return s0_ref[...] * wc_ref[0:1, :] + s1_ref[...] * wc_ref[1:2, :] + up[0:n_step, :] * wc_ref[2:3, :]

    up_a = half(wua_ref, wca_ref, s0a_ref, s1a_ref, upa_ref, taila_ref)
    up_b = half(wub_ref, wcb_ref, s0b_ref, s1b_ref, upb_ref, tailb_ref)
    act = (up_b * jax.nn.sigmoid(up_b) * up_a).astype(BF16)
    acc_ref[0:n_step, :] += jnp.dot(act, wd_ref[...], preferred_element_type=F32)

    @pl.when(f == n_f - 1)
    def _():
        _final_norm(acc_ref, g_ref, y_ref)


def _ffn_small(x, mix, wo, g2, wu, wc, wd, g, s0, s1, *, tf, n_step):
    rows = x.shape[0]
    n_f = D_FF // tf
    lo = lambda f: (0, f)
    hi = lambda f: (0, n_f + f)
    return pl.pallas_call(
        functools.partial(_ffn_small_kernel, n_f=n_f, n_step=n_step, n_tail=rows - n_step),
        grid=(n_f,),
        in_specs=[pl.BlockSpec((rows, D_MODEL), lambda f: (0, 0)),
                  pl.BlockSpec((rows, D_MODEL), lambda f: (0, 0)),
                  pl.BlockSpec((D_MODEL, D_MODEL), lambda f: (0, 0), pipeline_mode=pl.Buffered(1)),
                  pl.BlockSpec((1, D_MODEL), lambda f: (0, 0)),
                  pl.BlockSpec((D_MODEL, tf), lo),
                  pl.BlockSpec((D_MODEL, tf), hi),
                  pl.BlockSpec((CONV_W, tf), lo),
                  pl.BlockSpec((CONV_W, tf), hi),
                  pl.BlockSpec((tf, D_MODEL), lambda f: (f, 0)),
                  pl.BlockSpec((1, D_MODEL), lambda f: (0, 0)),
                  pl.BlockSpec((n_step, tf), lo),
                  pl.BlockSpec((n_step, tf), lo),
                  pl.BlockSpec((n_step, tf), hi),
                  pl.BlockSpec((n_step, tf), hi)],
        out_specs=[pl.BlockSpec((n_step, D_MODEL), lambda f: (0, 0)),
                   pl.BlockSpec((n_step, tf), lo),
                   pl.BlockSpec((n_step, tf), lo),
                   pl.BlockSpec((n_f, 2, tf), lambda f: (0, 0, 0)),
                   pl.BlockSpec((n_f, 2, tf), lambda f: (0, 0, 0))],
        out_shape=[jax.ShapeDtypeStruct((n_step, D_MODEL), F32),
                   jax.ShapeDtypeStruct((n_step, D_FF), F32),
                   jax.ShapeDtypeStruct((n_step, D_FF), F32),
                   jax.ShapeDtypeStruct((n_f, 2, tf), F32),
                   jax.ShapeDtypeStruct((n_f, 2, tf), F32)],
        scratch_shapes=[pltpu.VMEM((rows, D_MODEL), F32),
                        pltpu.VMEM((rows, D_MODEL), BF16)],
        compiler_params=_params(1),
        name="ffn_small",
    )(x, mix, wo, g2, wu, wu, wc, wc, wd, g, s0, s1, s0, s1)


def _tail_to_halo(tail):
    return jnp.pad(tail, ((HALO - 2, 0), (0, 0)))


def _join_tail_blocks(t):
    b, n_f, _, tf = t.shape
    return t.transpose(0, 2, 1, 3).reshape(b, 2, n_f * tf)


def kernel(x_prompt, x_sample, state_mlstm_C, state_mlstm_n, state_mlstm_m, state_shortconv,
           state_ffnconv, meta_tokens, norm1_g, w_in, b_if, w_shortconv, w_out, norm2_g,
           w_up, w_ffconv, w_down, norm_f_g):
    assert w_in.shape[0] == 1, "single-layer trunk"
    n_batch, seq, _ = x_prompt.shape
    n_dec = x_sample.shape[0]

    wp, wg = _prep_win(w_in[0].T, tk=1024)
    bif = jnp.pad(b_if[0], (0, 128 - 2 * N_HEADS)).reshape(1, 128)
    wo = w_out[0].astype(BF16)
    wsc = w_shortconv[0]
    wfc = w_ffconv[0]
    g1 = norm1_g[0].reshape(1, D_MODEL)
    g2 = norm2_g[0].reshape(1, D_MODEL)
    gf = norm_f_g.reshape(1, D_MODEL)

    xm = jnp.pad(meta_tokens, ((0, CHUNK - N_META), (0, 0)))
    hn_m, gates_m = _norm_gates_seq(xm, g1, wg, bif, tm=CHUNK, n_valid=N_META)
    zc = jnp.zeros((N_HEADS, DK, DV), F32)
    zn = jnp.zeros((N_HEADS, 1, DK), F32)
    mix_m, c_m, n_m, m_m, sc_m = _mixers(
        hn_m, wp, gates_m, wsc, zc, zn, zn, jnp.zeros((HALO, D_CONV), F32),
        n_batch=1, tm=CHUNK, n_valid=N_META)

    xp = x_prompt.reshape(n_batch * seq, D_MODEL)
    tm = 512
    hn_p, gates_p = _norm_gates_seq(xp, g1, wg, bif, tm=tm, n_valid=CHUNK)
    xs = x_sample.reshape(n_dec, D_MODEL)
    hn_s, gif_s = _norm_gates_step(xs, g1, wg, bif)
    mprev = jnp.pad(state_mlstm_m[0], ((0, 0), (0, 128 - N_HEADS)))
    n_hb = state_mlstm_n[0].transpose(1, 0, 2)
    mix_p, c_p, n_p, m_p, sc_p, mix_s, c_s, n_s, m_s, cu_s, wu, wd = _mixers(
        hn_p, wp, gates_p, wsc, c_m[0], n_m[0], m_m[0], _tail_to_halo(sc_m[0]),
        (hn_s, gif_s, mprev, state_shortconv[0, :, 0, :], state_shortconv[0, :, 1, :],
         state_mlstm_C[0], n_hb, w_up[0], w_down[0]),
        n_batch=n_batch, tm=2 * tm, n_valid=2 * tm)

    x_sm = jnp.concatenate([xs, meta_tokens], axis=0)
    mix_sm = jnp.concatenate([mix_s, mix_m[:N_META].astype(F32)], axis=0)
    y_s, upa_s, upb_s, fca_m, fcb_m = _ffn_small(
        x_sm, mix_sm, wo, g2, wu, wfc, wd, gf,
        state_ffnconv[0, :, 0, :], state_ffnconv[0, :, 1, :], tf=FFN_TF, n_step=n_dec)
    y_p, fca_p, fcb_p = _ffn_seq(xp, mix_p, wo, g2, wu, wfc, wd, gf,
                                 _tail_to_halo(_join_tail_blocks(fca_m[None])[0]),
                                 _tail_to_halo(_join_tail_blocks(fcb_m[None])[0]),
                                 n_batch=n_batch, tm=tm, tf=FFN_TF, n_valid=tm)

    return (y_p.reshape(n_batch, seq, D_MODEL),
            y_s.reshape(n_dec, 1, D_MODEL),
            c_p[None],
            n_p.reshape(1, n_batch, N_HEADS, DK),
            m_p[:, :, 0, 0][None],
            sc_p[None],
            jnp.concatenate([_join_tail_blocks(fca_p), _join_tail_blocks(fcb_p)], axis=-1)[None],
            c_s[None],
            n_s.transpose(1, 0, 2)[None],
            m_s[:, :, 0].T[None],
            jnp.stack([state_shortconv[0, :, 1, :], cu_s], axis=1)[None],
            jnp.stack([state_ffnconv[0, :, 1, :], jnp.concatenate([upa_s, upb_s], axis=-1)], axis=1)[None])
```

```python
import functools

import jax
import jax.numpy as jnp
from jax import lax
from jax.experimental import pallas as pl
from jax.experimental.pallas import tpu as pltpu

D_MODEL = 2048
N_META = 16
N_HEADS = 8
DK = 128
DV = 256
D_CONV = D_MODEL
CONV_W = 3
D_FF = 5632
CHUNK = 128
EPS = 1e-6
K_SCALE = DK ** -0.5
NEG = -1e30
HEAD_COLS = 7 * DV + 2 * DK
V7X_VMEM_LIMIT = 56 * 1024 * 1024
HALO = 8
FFN_SUB = 64
FFN_TF = 512
DOWN_GROUPS = 2
F32 = jnp.float32
BF16 = jnp.bfloat16

_Q, _K, _V, _O, _GB, _U, _CG, _BG, _GA = (0, 128, 256, 512, 768, 1024, 1280, 1536, 1792)


def _params(n_grid):
    return pltpu.CompilerParams(dimension_semantics=("arbitrary",) * n_grid,
                                vmem_limit_bytes=V7X_VMEM_LIMIT)


def _log_sigmoid(x):
    return jnp.minimum(x, 0.0) - jnp.log1p(jnp.exp(-jnp.abs(x)))


def _rms(x, g):
    return x * lax.rsqrt(jnp.mean(x * x, axis=-1, keepdims=True) + EPS) * g


def _prep_win_kernel(u_ref, cg_ref, bg_ref, ga_ref, v_ref, o_ref, gb_ref, q_ref, k_ref, gif_ref,
                     wp_ref, wg_ref):
    for ref, off, d in ((u_ref, _U, DV), (cg_ref, _CG, DV), (bg_ref, _BG, DV), (ga_ref, _GA, DV),
                        (v_ref, _V, DV), (o_ref, _O, DV), (gb_ref, _GB, DV), (q_ref, _Q, DK), (k_ref, _K, DK)):
        wp_ref[:, off:off + d] = ref[...].T.astype(BF16)
    gif = gif_ref[...].T
    lane = lax.broadcasted_iota(jnp.int32, gif.shape, 1)
    wg_ref[...] = jnp.where(lane < 2 * N_HEADS, gif, 0.0).astype(BF16)


def _prep_win(wt, *, tk):
    o_q = 3 * D_CONV
    o_k = o_q + N_HEADS * DK
    o_v = o_k + N_HEADS * DK
    o_o = o_v + N_HEADS * DV
    o_gif = o_o + N_HEADS * DV
    o_ga = o_gif + 2 * N_HEADS
    o_gb = o_ga + D_CONV
    assert o_ga % 8 == 0
    rows = lambda off, d: pl.BlockSpec((pl.Element(d), pl.Element(tk)),
                                       lambda i, h: (pl.multiple_of(off + d * h, 8), pl.multiple_of(i * tk, 128)))
    return pl.pallas_call(
        _prep_win_kernel,
        grid=(D_MODEL // tk, N_HEADS),
        in_specs=[rows(0, DV), rows(D_CONV, DV), rows(2 * D_CONV, DV), rows(o_ga, DV), rows(o_v, DV),
                  rows(o_o, DV), rows(o_gb, DV), rows(o_q, DK), rows(o_k, DK),
                  pl.BlockSpec((pl.Element(128), pl.Element(tk)),
                               lambda i, h: (o_gif, pl.multiple_of(i * tk, 128)))],
        out_specs=[pl.BlockSpec((tk, HEAD_COLS), lambda i, h: (i, h)),
                   pl.BlockSpec((tk, 128), lambda i, h: (i, 0))],
        out_shape=[jax.ShapeDtypeStruct((D_MODEL, N_HEADS * HEAD_COLS), BF16),
                   jax.ShapeDtypeStruct((D_MODEL, 128), BF16)],
        compiler_params=_params(2),
        name="prep_win",
    )(*([wt] * 10))


def _norm_gates_seq_kernel(x_ref, g_ref, wg_ref, bif_ref, hn_ref, gates_ref, *, tm, n_valid):
    hn = _rms(x_ref[...], g_ref[...]).astype(BF16)
    hn_ref[...] = hn
    gif = jnp.dot(hn, wg_ref[...], preferred_element_type=F32) + bif_ref[...]
    row = lax.broadcasted_iota(jnp.int32, (CHUNK, CHUNK), 0)
    col = lax.broadcasted_iota(jnp.int32, (CHUNK, CHUNK), 1)
    upper = (row <= col).astype(F32)
    for c in range(tm // CHUNK):
        gt = gif[c * CHUNK:(c + 1) * CHUNK, :].T
        ig = gt[0:N_HEADS, :]
        lf = _log_sigmoid(gt[N_HEADS:2 * N_HEADS, :])
        if n_valid < CHUNK:
            tok = lax.broadcasted_iota(jnp.int32, (N_HEADS, CHUNK), 1)
            ig = jnp.where(tok < n_valid, ig, NEG)
            lf = jnp.where(tok < n_valid, lf, 0.0)
        b = jnp.dot(lf, upper, preferred_element_type=F32, precision=lax.Precision.HIGHEST)
        a = ig - b
        for j in range(N_HEADS):
            gates_ref[j, 0:1, c * CHUNK:(c + 1) * CHUNK] = a[j:j + 1, :]
            gates_ref[j, 1:2, c * CHUNK:(c + 1) * CHUNK] = b[j:j + 1, :]


def _norm_gates_seq(x, g, wg, bif, *, tm, n_valid):
    t = x.shape[0]
    return pl.pallas_call(
        functools.partial(_norm_gates_seq_kernel, tm=tm, n_valid=n_valid),
        grid=(t // tm,),
        in_specs=[pl.BlockSpec((tm, D_MODEL), lambda r: (r, 0)),
                  pl.BlockSpec((1, D_MODEL), lambda r: (0, 0)),
                  pl.BlockSpec((D_MODEL, 128), lambda r: (0, 0)),
                  pl.BlockSpec((1, 128), lambda r: (0, 0))],
        out_specs=[pl.BlockSpec((tm, D_MODEL), lambda r: (r, 0)),
                   pl.BlockSpec((N_HEADS, 2, tm), lambda r: (0, 0, r))],
        out_shape=[jax.ShapeDtypeStruct((t, D_MODEL), BF16),
                   jax.ShapeDtypeStruct((N_HEADS, 2, t), F32)],
        compiler_params=_params(1),
        name="norm_gates_seq",
    )(x, g, wg, bif)


def _norm_gates_step_kernel(x_ref, g_ref, wg_ref, bif_ref, hn_ref, gif_ref):
    hn = _rms(x_ref[...], g_ref[...]).astype(BF16)
    hn_ref[...] = hn
    gif_ref[...] = jnp.dot(hn, wg_ref[...], preferred_element_type=F32) + bif_ref[...]


def _norm_gates_step(x, g, wg, bif):
    t = x.shape[0]
    return pl.pallas_call(
        _norm_gates_step_kernel,
        grid=(1,),
        in_specs=[pl.BlockSpec((t, D_MODEL), lambda r: (0, 0)),
                  pl.BlockSpec((1, D_MODEL), lambda r: (0, 0)),
                  pl.BlockSpec((D_MODEL, 128), lambda r: (0, 0)),
                  pl.BlockSpec((1, 128), lambda r: (0, 0))],
        out_specs=[pl.BlockSpec((t, D_MODEL), lambda r: (0, 0)),
                   pl.BlockSpec((t, 128), lambda r: (0, 0))],
        out_shape=[jax.ShapeDtypeStruct((t, D_MODEL), BF16),
                   jax.ShapeDtypeStruct((t, 128), F32)],
        compiler_params=_params(1),
        name="norm_gates_step",
    )(x, g, wg, bif)


def _mixers_kernel(*refs, tm, tiles_per_batch, n_valid, step_bb):
    n_seq = (8, 5, 11)
    n_step = (9, 7, 5) if step_bb else (0, 0, 0)
    cuts = [0]
    for a, b in zip(n_seq, n_step):
        cuts += [cuts[-1] + a, cuts[-1] + a + b]
    seq_in, step_in, seq_out, step_out, seq_scr, step_scr = (refs[i:j] for i, j in zip(cuts[:-1], cuts[1:]))
    hn_ref, w_ref, gates_ref, wsc_ref, c0_ref, n0_ref, m0_ref, sc0_ref = seq_in
    mix_ref, cout_ref, nout_ref, mout_ref, scout_ref = seq_out
    z_ref, cu_ref, c_ref, n_ref, m_ref, dmat_ref, qk_ref, sv_ref, kv_ref, bmix_ref, amix_ref = seq_scr
    head = pl.program_id(0)
    tile = pl.program_id(1)
    rb = tile % tiles_per_batch
    nc = tm // CHUNK

    @pl.when(rb == 0)
    def _():
        c_ref[...] = c0_ref[...]
        n_ref[...] = n0_ref[...]
        m_ref[...] = m0_ref[...]
        cu_ref[0:HALO, :] = sc0_ref[...]

    row = lax.broadcasted_iota(jnp.int32, (CHUNK, CHUNK), 0)
    col = lax.broadcasted_iota(jnp.int32, (CHUNK, CHUNK), 1)
    causal = col <= row
    eye = col == row
    st = [dict() for _ in range(nc)]
    rows_of = lambda c: slice(c * CHUNK, (c + 1) * CHUNK)
    k_of = lambda rows: z_ref[rows, _K:_K + DK] * K_SCALE

    def proj(j):
        cols = slice(j * DV, (j + 1) * DV)
        z_ref[:, cols] = jnp.dot(hn_ref[...], w_ref[:, cols], preferred_element_type=F32)

    def gate_terms(c):
        rows, d = rows_of(c), st[c]
        a_row = gates_ref[0:1, rows]
        b_row = gates_ref[1:2, rows]
        d["a_col"] = jnp.sum(jnp.where(eye, a_row, 0.0), axis=-1, keepdims=True)
        d["b_col"] = jnp.sum(jnp.where(eye, b_row, 0.0), axis=-1, keepdims=True)
        dmat = jnp.where(causal, d["b_col"] + a_row, NEG)
        dmat_ref[rows, :] = dmat
        d["mloc"] = jnp.max(dmat, axis=-1, keepdims=True)
        qk_ref[rows, :] = lax.dot_general(z_ref[rows, _Q:_Q + DK].astype(BF16), k_of(rows).astype(BF16),
                                          (((1,), (1,)), ((), ())), preferred_element_type=F32)

    def stabilisers():
        m_prev = m_ref[:, 0:1]
        for c in range(nc):
            d = st[c]
            d["m_prev"] = m_prev
            d["bm"] = d["b_col"] + m_prev
            d["m_t"] = jnp.maximum(d["bm"], d["mloc"])
            d["m_last"] = d["m_t"][CHUNK - 1:CHUNK, :]
            m_prev = d["m_last"]
        m_ref[...] = jnp.broadcast_to(m_prev, (1, DK))

    def intra(c):
        rows, d = rows_of(c), st[c]
        k = k_of(rows)
        v = z_ref[rows, _V:_V + DV]
        s = qk_ref[rows, :] * jnp.exp(dmat_ref[rows, :] - d["m_t"])
        sv_ref[rows, :] = jnp.dot(s.astype(BF16), v.astype(BF16), preferred_element_type=F32)
        d["ssum"] = jnp.sum(s, axis=-1, keepdims=True)
        d["decay"] = jnp.exp(d["bm"] - d["m_t"])
        d["hden"] = jnp.exp(-d["m_t"])
        b_last = d["b_col"][CHUNK - 1:CHUNK, :]
        g_col = jnp.exp(b_last + d["a_col"] - d["m_last"])
        d["carry"] = jnp.exp(b_last + d["m_prev"] - d["m_last"])
        kv_ref[c] = jnp.dot(k.T.astype(BF16), (g_col * v).astype(BF16), preferred_element_type=F32)
        d["kn"] = jnp.sum(g_col * k, axis=0, keepdims=True)

    def inter(c):
        rows, d = rows_of(c), st[c]
        q = z_ref[rows, _Q:_Q + DK]
        c_prev = c_ref[...]
        n_prev = n_ref[...]
        num = (d["decay"] * jnp.dot(q.astype(BF16), c_prev.astype(BF16), preferred_element_type=F32)
               + sv_ref[rows, :])
        den = d["decay"] * jnp.sum(q * n_prev, axis=-1, keepdims=True) + d["ssum"]
        hb = num / jnp.maximum(jnp.abs(den), d["hden"])
        b_out = jax.nn.sigmoid(z_ref[rows, _O:_O + DV]) * hb
        bmix_ref[rows, :] = jax.nn.sigmoid(z_ref[rows, _GB:_GB + DV]) * b_out
        c_ref[...] = d["carry"] * c_prev + kv_ref[c]
        n_ref[...] = d["carry"] * n_prev + d["kn"]

    def conv(c):
        lo = c * CHUNK
        rows = rows_of(c)
        cu = z_ref[rows, _CG:_CG + DV] * z_ref[rows, _U:_U + DV]
        cu_ref[HALO + lo:HALO + lo + CHUNK, :] = cu
        taps = (cu * wsc_ref[2:3, :] + cu_ref[HALO - 1 + lo:HALO - 1 + lo + CHUNK, :] * wsc_ref[1:2, :]
                + cu_ref[HALO - 2 + lo:HALO - 2 + lo + CHUNK, :] * wsc_ref[0:1, :])
        amix_ref[rows, :] = z_ref[rows, _BG:_BG + DV] * taps

    def merge(c):
        rows = rows_of(c)
        mix = jax.nn.sigmoid(z_ref[rows, _GA:_GA + DV]) * amix_ref[rows, :] + bmix_ref[rows, :]
        mix_ref[rows, :] = mix.astype(BF16)

    piece = lambda off: off // DV
    assert (piece(_Q), piece(_V), piece(_O), piece(_GB)) == (0, 1, 2, 3)
    assert sorted(map(piece, (_U, _CG, _BG))) == [4, 5, 6] and piece(_GA) == HEAD_COLS // DV - 1
    each = lambda fn: [functools.partial(fn, c) for c in range(nc)]
    half = (nc + 1) // 2
    after = {0: [],
             1: each(gate_terms)[:half],
             2: each(gate_terms)[half:] + [stabilisers],
             3: each(intra)[:half],
             4: each(intra)[half:] + each(inter)[:1],
             5: each(inter)[1:1 + half],
             6: each(inter)[1 + half:] + each(conv),
             7: each(merge)}

    if step_bb:
        bb = step_bb
        hns_ref, gifs_ref, mprevs_ref, s0_ref, s1_ref, cs_ref, ns_ref, wu32_ref, wd32_ref = step_in
        mixs_ref, csout_ref, nsout_ref, msout_ref, cuouts_ref, wu16_ref, wd16_ref = step_out
        zs_ref, aouts_ref, qts_ref, kts_ref, hbs_ref = step_scr

        def round_weights():
            wu16_ref[...] = wu32_ref[...].astype(BF16)
            wd16_ref[...] = wd32_ref[...].astype(BF16)

        @pl.when(tile == 0)
        def _():
            z = jnp.dot(hns_ref[...], w_ref[...], preferred_element_type=F32)
            zs_ref[...] = z
            cu = z[:, _CG:_CG + DV] * z[:, _U:_U + DV]
            taps = s0_ref[...] * wsc_ref[0:1, :] + s1_ref[...] * wsc_ref[1:2, :] + cu * wsc_ref[2:3, :]
            aouts_ref[...] = z[:, _BG:_BG + DV] * taps
            cuouts_ref[...] = cu
            qt = z[:, _Q:_Q + DK].T
            kt = (z[:, _K:_K + DK] * K_SCALE).T
            for jj in range(hns_ref.shape[0] // bb):
                qts_ref[jj] = qt[:, jj * bb:(jj + 1) * bb]
                kts_ref[jj] = kt[:, jj * bb:(jj + 1) * bb]

        srows = pl.ds(pl.multiple_of(tile * bb, bb), bb)
        sd = {}

        def step_gates():
            lane = lax.broadcasted_iota(jnp.int32, (bb, 128), 1)
            pick = lambda x, idx: jnp.sum(jnp.where(lane == idx, x, 0.0), axis=-1, keepdims=True)
            gif = gifs_ref[srows, :]
            ig = pick(gif, head)
            lf = _log_sigmoid(pick(gif, head + N_HEADS))
            m_prev = pick(mprevs_ref[srows, :], head)
            q = zs_ref[srows, _Q:_Q + DK]
            k = zs_ref[srows, _K:_K + DK] * K_SCALE
            n_prev = ns_ref[...]
            m_t = jnp.maximum(lf + m_prev, ig)
            sd["w"] = jnp.exp(ig - m_t)
            sd["decay"] = jnp.exp(lf + m_prev - m_t)
            sd["s"] = jnp.sum(q * k, axis=-1, keepdims=True) * sd["w"]
            den = sd["decay"] * jnp.sum(q * n_prev, axis=-1, keepdims=True) + sd["s"]
            sd["inv"] = 1.0 / jnp.maximum(jnp.abs(den), jnp.exp(-m_t))
            nsout_ref[...] = sd["decay"] * n_prev + sd["w"] * k
            msout_ref[...] = jnp.broadcast_to(m_t, (bb, DK))

        def step_rows(lo, hi):
            qt = qts_ref[tile]
            kt = kts_ref[tile]
            v = zs_ref[srows, _V:_V + DV]
            for i in range(lo, hi):
                c_prev = cs_ref[i]
                d_i = sd["decay"][i:i + 1, :]
                v_i = v[i:i + 1, :]
                qc = jnp.sum(qt[:, i:i + 1] * c_prev, axis=0, keepdims=True)
                hbs_ref[i:i + 1, :] = (d_i * qc + sd["s"][i:i + 1, :] * v_i) * sd["inv"][i:i + 1, :]
                csout_ref[i] = d_i * c_prev + (sd["w"][i:i + 1, :] * kt[:, i:i + 1]) * v_i

        def step_merge():
            b_out = jax.nn.sigmoid(zs_ref[srows, _O:_O + DV]) * hbs_ref[...]
            mixs_ref[...] = (jax.nn.sigmoid(zs_ref[srows, _GA:_GA + DV]) * aouts_ref[srows, :]
                             + jax.nn.sigmoid(zs_ref[srows, _GB:_GB + DV]) * b_out)

        after[0] += [round_weights, step_gates, functools.partial(step_rows, 0, bb // 2)]
        after[5] += [functools.partial(step_rows, bb // 2, bb)]
        after[6] = [step_merge] + after[6]

    for j in range(HEAD_COLS // DV):
        proj(j)
        for fn in after.get(j, []):
            fn()

    @pl.when(rb == tiles_per_batch - 1)
    def _():
        cout_ref[...] = c_ref[...]
        nout_ref[...] = n_ref[...]
        mout_ref[...] = m_ref[...]
        scout_ref[...] = cu_ref[HALO + n_valid - 2:HALO + n_valid, :]

    cu_ref[0:HALO, :] = cu_ref[tm:tm + HALO, :]


def _mixers(hn, wp, gates, wsc, c0, n0, m0, sc0, step=None, *, n_batch, tm, n_valid):
    t = hn.shape[0]
    n_tiles = t // tm
    tpb = n_tiles // n_batch
    in_specs = [pl.BlockSpec((tm, D_MODEL), lambda h, r: (r, 0)),
                pl.BlockSpec((D_MODEL, HEAD_COLS), lambda h, r: (0, h)),
                pl.BlockSpec((None, 2, tm), lambda h, r: (h, 0, r)),
                pl.BlockSpec((CONV_W, DV), lambda h, r: (0, h)),
                pl.BlockSpec((None, DK, DV), lambda h, r: (h, 0, 0)),
                pl.BlockSpec((None, 1, DK), lambda h, r: (h, 0, 0)),
                pl.BlockSpec((None, 1, DK), lambda h, r: (h, 0, 0)),
                pl.BlockSpec((HALO, DV), lambda h, r: (0, h))]
    out_specs = [pl.BlockSpec((tm, DV), lambda h, r: (r, h)),
                 pl.BlockSpec((None, None, DK, DV), lambda h, r: (r // tpb, h, 0, 0)),
                 pl.BlockSpec((None, None, 1, DK), lambda h, r: (r // tpb, h, 0, 0)),
                 pl.BlockSpec((None, None, 1, DK), lambda h, r: (r // tpb, h, 0, 0)),
                 pl.BlockSpec((None, 2, DV), lambda h, r: (r // tpb, 0, h))]
    out_shape = [jax.ShapeDtypeStruct((t, D_MODEL), BF16),
                 jax.ShapeDtypeStruct((n_batch, N_HEADS, DK, DV), F32),
                 jax.ShapeDtypeStruct((n_batch, N_HEADS, 1, DK), F32),
                 jax.ShapeDtypeStruct((n_batch, N_HEADS, 1, DK), F32),
                 jax.ShapeDtypeStruct((n_batch, 2, D_CONV), F32)]
    scratch = [pltpu.VMEM((tm, HEAD_COLS), F32),
               pltpu.VMEM((tm + HALO, DV), F32),
               pltpu.VMEM((DK, DV), F32),
               pltpu.VMEM((1, DK), F32),
               pltpu.VMEM((1, DK), F32),
               pltpu.VMEM((tm, CHUNK), F32),
               pltpu.VMEM((tm, CHUNK), F32),
               pltpu.VMEM((tm, DV), F32),
               pltpu.VMEM((tm // CHUNK, DK, DV), F32),
               pltpu.VMEM((tm, DV), F32),
               pltpu.VMEM((tm, DV), F32)]
    step_bb = 0
    if step is not None:
        nb = step[0].shape[0]
        assert nb % n_tiles == 0
        step_bb = nb // n_tiles
        in_specs += [pl.BlockSpec((nb, D_MODEL), lambda h, r: (0, 0)),
                     pl.BlockSpec((nb, 128), lambda h, r: (0, 0)),
                     pl.BlockSpec((nb, 128), lambda h, r: (0, 0)),
                     pl.BlockSpec((nb, DV), lambda h, r: (0, h)),
                     pl.BlockSpec((nb, DV), lambda h, r: (0, h)),
                     pl.BlockSpec((step_bb, None, DK, DV), lambda h, r: (r, h, 0, 0)),
                     pl.BlockSpec((None, step_bb, DK), lambda h, r: (h, r, 0))]
        n_steps = N_HEADS * n_tiles
        wu32, wd32 = step[7], step[8]
        assert wu32.shape[0] % (16 * n_steps) == 0 and wd32.shape[0] % (8 * n_steps) == 0
        wu_rows = wu32.shape[0] // n_steps
        wd_rows, wd_cols = 2 * wd32.shape[0] // n_steps, wd32.shape[1] // 2
        assert wd_rows % 16 == 0
        wu_spec = pl.BlockSpec((wu_rows, wu32.shape[1]), lambda h, r: (h * n_tiles + r, 0))
        wd_spec = pl.BlockSpec((wd_rows, wd_cols), lambda h, r: ((h * n_tiles + r) // 2, (h * n_tiles + r) % 2))
        in_specs += [wu_spec, wd_spec]
        out_specs += [pl.BlockSpec((step_bb, DV), lambda h, r: (r, h)),
                      pl.BlockSpec((step_bb, None, DK, DV), lambda h, r: (r, h, 0, 0)),
                      pl.BlockSpec((None, step_bb, DK), lambda h, r: (h, r, 0)),
                      pl.BlockSpec((None, step_bb, DK), lambda h, r: (h, r, 0)),
                      pl.BlockSpec((nb, DV), lambda h, r: (0, h)),
                      wu_spec, wd_spec]
        out_shape += [jax.ShapeDtypeStruct((nb, D_MODEL), F32),
                      jax.ShapeDtypeStruct((nb, N_HEADS, DK, DV), F32),
                      jax.ShapeDtypeStruct((N_HEADS, nb, DK), F32),
                      jax.ShapeDtypeStruct((N_HEADS, nb, DK), F32),
                      jax.ShapeDtypeStruct((nb, D_CONV), F32),
                      jax.ShapeDtypeStruct(wu32.shape, BF16),
                      jax.ShapeDtypeStruct(wd32.shape, BF16)]
        scratch += [pltpu.VMEM((nb, HEAD_COLS), F32),
                    pltpu.VMEM((nb, DV), F32),
                    pltpu.VMEM((n_tiles, DK, step_bb), F32),
                    pltpu.VMEM((n_tiles, DK, step_bb), F32),
                    pltpu.VMEM((step_bb, DV), F32)]
    return pl.pallas_call(
        functools.partial(_mixers_kernel, tm=tm, tiles_per_batch=tpb, n_valid=n_valid, step_bb=step_bb),
        grid=(N_HEADS, n_tiles),
        in_specs=in_specs,
        out_specs=out_specs,
        out_shape=out_shape,
        scratch_shapes=scratch,
        compiler_params=_params(2),
        name="mixers",
    )(hn, wp, gates, wsc, c0, n0, m0, sc0, *(step or ()))


def _mixer_residual(x_ref, mix_ref, wo_ref, g2_ref, acc_ref, h2_ref):
    mix = mix_ref[...]
    if mix.dtype != BF16:
        mix = mix.astype(BF16)
    acc_ref[...] = x_ref[...] + jnp.dot(mix, wo_ref[...], preferred_element_type=F32)
    rows = acc_ref.shape[0]
    for lo in range(0, rows, FFN_SUB):
        hi = min(lo + FFN_SUB, rows)
        h2_ref[lo:hi, :] = _rms(acc_ref[lo:hi, :], g2_ref[...]).astype(BF16)


def _final_norm(acc_ref, g_ref, y_ref):
    rows = y_ref.shape[0]
    for lo in range(0, rows, FFN_SUB):
        hi = min(lo + FFN_SUB, rows)
        y_ref[lo:hi, :] = _rms(acc_ref[lo:hi, :], g_ref[...])


def _ffn_seq_kernel(x_ref, mix_ref, wo_ref, g2_ref, wua_ref, wub_ref, wca_ref, wcb_ref, wd_ref, g_ref,
                    ha0_ref, hb0_ref,
                    y_ref, fca_ref, fcb_ref,
                    acc_ref, h2_ref, ua_ref, ub_ref, ha_ref, hb_ref, act_ref,
                    *, tm, tiles_per_batch, n_valid, n_f):
    r = pl.program_id(0)
    f = pl.program_id(1)
    rb = r % tiles_per_batch

    @pl.when((r == 0) & (f == 0))
    def _():
        ha_ref[...] = jnp.zeros_like(ha_ref)
        hb_ref[...] = jnp.zeros_like(hb_ref)

    @pl.when(f == 0)
    def _():
        _mixer_residual(x_ref, mix_ref, wo_ref, g2_ref, acc_ref, h2_ref)

    ua_ref[0:HALO, :] = jnp.where(rb == 0, ha0_ref[...], ha_ref[f])
    ub_ref[0:HALO, :] = jnp.where(rb == 0, hb0_ref[...], hb_ref[f])
    group = tm // DOWN_GROUPS

    def up_proj(gi):
        rows = slice(gi * group, (gi + 1) * group)
        urows = slice(HALO + gi * group, HALO + (gi + 1) * group)
        ua_ref[urows, :] = jnp.dot(h2_ref[rows, :], wua_ref[...], preferred_element_type=F32)
        ub_ref[urows, :] = jnp.dot(h2_ref[rows, :], wub_ref[...], preferred_element_type=F32)

    def conv(u_ref, wc_ref, lo):
        win = u_ref[lo:lo + HALO + FFN_SUB, :]
        back1 = pltpu.roll(win, 1, axis=0)[HALO:, :]
        back2 = pltpu.roll(win, 2, axis=0)[HALO:, :]
        return win[HALO:, :] * wc_ref[2:3, :] + back1 * wc_ref[1:2, :] + back2 * wc_ref[0:1, :]

    def glu_down(gi):
        for lo in range(gi * group, (gi + 1) * group, FFN_SUB):
            up_a = conv(ua_ref, wca_ref, lo)
            up_b = conv(ub_ref, wcb_ref, lo)
            act_ref[lo:lo + FFN_SUB, :] = (up_b * jax.nn.sigmoid(up_b) * up_a).astype(BF16)
        rows = slice(gi * group, (gi + 1) * group)
        acc_ref[rows, :] += jnp.dot(act_ref[rows, :], wd_ref[...], preferred_element_type=F32)

    up_proj(0)
    for gi in range(1, DOWN_GROUPS):
        up_proj(gi)
        glu_down(gi - 1)
    glu_down(DOWN_GROUPS - 1)
    for u_ref, halo_ref, tail_ref in ((ua_ref, ha_ref, fca_ref), (ub_ref, hb_ref, fcb_ref)):
        halo_ref[f] = u_ref[tm:tm + HALO, :]
        tail_ref[f] = u_ref[HALO + n_valid - 2:HALO + n_valid, :]

    @pl.when(f == n_f - 1)
    def _():
        _final_norm(acc_ref, g_ref, y_ref)


def _ffn_seq(x, mix, wo, g2, wu, wc, wd, g, halo_a, halo_b, *, n_batch, tm, tf, n_valid):
    t = x.shape[0]
    tpb = t // n_batch // tm
    n_f = D_FF // tf
    return pl.pallas_call(
        functools.partial(_ffn_seq_kernel, tm=tm, tiles_per_batch=tpb, n_valid=n_valid, n_f=n_f),
        grid=(t // tm, n_f),
        in_specs=[pl.BlockSpec((tm, D_MODEL), lambda r, f: (r, 0)),
                  pl.BlockSpec((tm, D_MODEL), lambda r, f: (r, 0)),
                  pl.BlockSpec((D_MODEL, D_MODEL), lambda r, f: (0, 0), pipeline_mode=pl.Buffered(1)),
                  pl.BlockSpec((1, D_MODEL), lambda r, f: (0, 0)),
                  pl.BlockSpec((D_MODEL, tf), lambda r, f: (0, f)),
                  pl.BlockSpec((D_MODEL, tf), lambda r, f: (0, n_f + f)),
                  pl.BlockSpec((CONV_W, tf), lambda r, f: (0, f)),
                  pl.BlockSpec((CONV_W, tf), lambda r, f: (0, n_f + f)),
                  pl.BlockSpec((tf, D_MODEL), lambda r, f: (f, 0)),
                  pl.BlockSpec((1, D_MODEL), lambda r, f: (0, 0)),
                  pl.BlockSpec((HALO, tf), lambda r, f: (0, f)),
                  pl.BlockSpec((HALO, tf), lambda r, f: (0, f))],
        out_specs=[pl.BlockSpec((tm, D_MODEL), lambda r, f: (r, 0)),
                   pl.BlockSpec((None, n_f, 2, tf), lambda r, f: (r // tpb, 0, 0, 0)),
                   pl.BlockSpec((None, n_f, 2, tf), lambda r, f: (r // tpb, 0, 0, 0))],
        out_shape=[jax.ShapeDtypeStruct((t, D_MODEL), F32),
                   jax.ShapeDtypeStruct((n_batch, n_f, 2, tf), F32),
                   jax.ShapeDtypeStruct((n_batch, n_f, 2, tf), F32)],
        scratch_shapes=[pltpu.VMEM((tm, D_MODEL), F32),
                        pltpu.VMEM((tm, D_MODEL), BF16),
                        pltpu.VMEM((tm + HALO, tf), F32),
                        pltpu.VMEM((tm + HALO, tf), F32),
                        pltpu.VMEM((n_f, HALO, tf), F32),
                        pltpu.VMEM((n_f, HALO, tf), F32),
                        pltpu.VMEM((tm, tf), BF16)],
        compiler_params=_params(2),
        name="ffn_seq",
    )(x, mix, wo, g2, wu, wu, wc, wc, wd, g, halo_a, halo_b)


def _ffn_small_kernel(x_ref, mix_ref, wo_ref, g2_ref, wua_ref, wub_ref, wca_ref, wcb_ref, wd_ref, g_ref,
                      s0a_ref, s1a_ref, s0b_ref, s1b_ref,
                      y_ref, upa_ref, upb_ref, taila_ref, tailb_ref,
                      acc_ref, h2_ref, *, n_f, n_step, n_tail):
    f = pl.program_id(0)

    @pl.when(f == 0)
    def _():
        _mixer_residual(x_ref, mix_ref, wo_ref, g2_ref, acc_ref, h2_ref)

    def half(wu_ref, wc_ref, s0_ref, s1_ref, up_ref, tail_ref):
        up = jnp.dot(h2_ref[...], wu_ref[...], preferred_element_type=F32)
        up_ref[...] = up[0:n_step, :]
        tail_ref[f] = up[n_step + n_tail - 2:n_step + n_tail, :]
        return s0_ref[...] * wc_ref[0:1, :] + s1_ref[...] * wc_ref[1:2, :] + up[0:n_step, :] * wc_ref[2:3, :]

    up_a = half(wua_ref, wca_ref, s0a_ref, s1a_ref, upa_ref, taila_ref)
    up_b = half(wub_ref, wcb_ref, s0b_ref, s1b_ref, upb_ref, tailb_ref)
    act = (up_b * jax.nn.sigmoid(up_b) * up_a).astype(BF16)
    acc_ref[0:n_step, :] += jnp.dot(act, wd_ref[...], preferred_element_type=F32)

    @pl.when(f == n_f - 1)
    def _():
        _final_norm(acc_ref, g_ref, y_ref)


def _ffn_small(x, mix, wo, g2, wu, wc, wd, g, s0, s1, *, tf, n_step):
    rows = x.shape[0]
    n_f = D_FF // tf
    lo = lambda f: (0, f)
    hi = lambda f: (0, n_f + f)
    return pl.pallas_call(
        functools.partial(_ffn_small_kernel, n_f=n_f, n_step=n_step, n_tail=rows - n_step),
        grid=(n_f,),
        in_specs=[pl.BlockSpec((rows, D_MODEL), lambda f: (0, 0)),
                  pl.BlockSpec((rows, D_MODEL), lambda f: (0, 0)),
                  pl.BlockSpec((D_MODEL, D_MODEL), lambda f: (0, 0), pipeline_mode=pl.Buffered(1)),
                  pl.BlockSpec((1, D_MODEL), lambda f: (0, 0)),
                  pl.BlockSpec((D_MODEL, tf), lo),
                  pl.BlockSpec((D_MODEL, tf), hi),
                  pl.BlockSpec((CONV_W, tf), lo),
                  pl.BlockSpec((CONV_W, tf), hi),
                  pl.BlockSpec((tf, D_MODEL), lambda f: (f, 0)),
                  pl.BlockSpec((1, D_MODEL), lambda f: (0, 0)),
                  pl.BlockSpec((n_step, tf), lo),
                  pl.BlockSpec((n_step, tf), lo),
                  pl.BlockSpec((n_step, tf), hi),
                  pl.BlockSpec((n_step, tf), hi)],
        out_specs=[pl.BlockSpec((n_step, D_MODEL), lambda f: (0, 0)),
                   pl.BlockSpec((n_step, tf), lo),
                   pl.BlockSpec((n_step, tf), lo),
                   pl.BlockSpec((n_f, 2, tf), lambda f: (0, 0, 0)),
                   pl.BlockSpec((n_f, 2, tf), lambda f: (0, 0, 0))],
        out_shape=[jax.ShapeDtypeStruct((n_step, D_MODEL), F32),
                   jax.ShapeDtypeStruct((n_step, D_FF), F32),
                   jax.ShapeDtypeStruct((n_step, D_FF), F32),
                   jax.ShapeDtypeStruct((n_f, 2, tf), F32),
                   jax.ShapeDtypeStruct((n_f, 2, tf), F32)],
        scratch_shapes=[pltpu.VMEM((rows, D_MODEL), F32),
                        pltpu.VMEM((rows, D_MODEL), BF16)],
        compiler_params=_params(1),
        name="ffn_small",
    )(x, mix, wo, g2, wu, wu, wc, wc, wd, g, s0, s1, s0, s1)


def _tail_to_halo(tail):
    return jnp.pad(tail, ((HALO - 2, 0), (0, 0)))


def _join_tail_blocks(t):
    b, n_f, _, tf = t.shape
    return t.transpose(0, 2, 1, 3).reshape(b, 2, n_f * tf)


def kernel(x_prompt, x_sample, state_mlstm_C, state_mlstm_n, state_mlstm_m, state_shortconv,
           state_ffnconv, meta_tokens, norm1_g, w_in, b_if, w_shortconv, w_out, norm2_g,
           w_up, w_ffconv, w_down, norm_f_g):
    assert w_in.shape[0] == 1, "single-layer trunk"
    n_batch, seq, _ = x_prompt.shape
    n_dec = x_sample.shape[0]

    wp, wg = _prep_win(w_in[0].T, tk=1024)
    bif = jnp.pad(b_if[0], (0, 128 - 2 * N_HEADS)).reshape(1, 128)
    wo = w_out[0].astype(BF16)
    wsc = w_shortconv[0]
    wfc = w_ffconv[0]
    g1 = norm1_g[0].reshape(1, D_MODEL)
    g2 = norm2_g[0].reshape(1, D_MODEL)
    gf = norm_f_g.reshape(1, D_MODEL)

    xm = jnp.pad(meta_tokens, ((0, CHUNK - N_META), (0, 0)))
    hn_m, gates_m = _norm_gates_seq(xm, g1, wg, bif, tm=CHUNK, n_valid=N_META)
    zc = jnp.zeros((N_HEADS, DK, DV), F32)
    zn = jnp.zeros((N_HEADS, 1, DK), F32)
    mix_m, c_m, n_m, m_m, sc_m = _mixers(
        hn_m, wp, gates_m, wsc, zc, zn, zn, jnp.zeros((HALO, D_CONV), F32),
        n_batch=1, tm=CHUNK, n_valid=N_META)

    xp = x_prompt.reshape(n_batch * seq, D_MODEL)
    tm = 512
    hn_p, gates_p = _norm_gates_seq(xp, g1, wg, bif, tm=tm, n_valid=CHUNK)
    xs = x_sample.reshape(n_dec, D_MODEL)
    hn_s, gif_s = _norm_gates_step(xs, g1, wg, bif)
    mprev = jnp.pad(state_mlstm_m[0], ((0, 0), (0, 128 - N_HEADS)))
    n_hb = state_mlstm_n[0].transpose(1, 0, 2)
    mix_p, c_p, n_p, m_p, sc_p, mix_s, c_s, n_s, m_s, cu_s, wu, wd = _mixers(
        hn_p, wp, gates_p, wsc, c_m[0], n_m[0], m_m[0], _tail_to_halo(sc_m[0]),
        (hn_s, gif_s, mprev, state_shortconv[0, :, 0, :], state_shortconv[0, :, 1, :],
         state_mlstm_C[0], n_hb, w_up[0], w_down[0]),
        n_batch=n_batch, tm=2 * tm, n_valid=2 * tm)

    x_sm = jnp.concatenate([xs, meta_tokens], axis=0)
    mix_sm = jnp.concatenate([mix_s, mix_m[:N_META].astype(F32)], axis=0)
    y_s, upa_s, upb_s, fca_m, fcb_m = _ffn_small(
        x_sm, mix_sm, wo, g2, wu, wfc, wd, gf,
        state_ffnconv[0, :, 0, :], state_ffnconv[0, :, 1, :], tf=FFN_TF, n_step=n_dec)
    y_p, fca_p, fcb_p = _ffn_seq(xp, mix_p, wo, g2, wu, wfc, wd, gf,
                                 _tail_to_halo(_join_tail_blocks(fca_m[None])[0]),
                                 _tail_to_halo(_join_tail_blocks(fcb_m[None])[0]),
                                 n_batch=n_batch, tm=tm, tf=FFN_TF, n_valid=tm)

    return (y_p.reshape(n_batch, seq, D_MODEL),
            y_s.reshape(n_dec, 1, D_MODEL),
            c_p[None],
            n_p.reshape(1, n_batch, N_HEADS, DK),
            m_p[:, :, 0, 0][None],
            sc_p[None],
            jnp.concatenate([_join_tail_blocks(fca_p), _join_tail_blocks(fcb_p)], axis=-1)[None],
            c_s[None],
            n_s.transpose(1, 0, 2)[None],
            m_s[:, :, 0].T[None],
            jnp.stack([state_shortconv[0, :, 1, :], cu_s], axis=1)[None],
            jnp.stack([state_ffnconv[0, :, 1, :], jnp.concatenate([upa_s, upb_s], axis=-1)], axis=1)[None])
```

```python
import functools

import jax
import jax.numpy as jnp
from jax import lax
from jax.experimental import pallas as pl
from jax.experimental.pallas import tpu as pltpu

D_MODEL = 2048
N_META = 16
N_HEADS = 8
DK = 128
DV = 256
D_CONV = D_MODEL
CONV_W = 3
D_FF = 5632
CHUNK = 128
EPS = 1e-6
K_SCALE = DK ** -0.5
NEG = -1e30
HEAD_COLS = 7 * DV + 2 * DK
LANES = 128
V7X_VMEM_BYTES = 64 * 1024 * 1024
V7X_VMEM_LIMIT = V7X_VMEM_BYTES * 7 // 8
HALO = 8
NORM_TM = 1024
MIXER_TM = 1024
FFN_TM = 512
FFN_TF = 512
FFN_SUB = 64
DOWN_GROUPS = 2
F32 = jnp.float32
BF16 = jnp.bfloat16

_Q, _K, _V, _O, _GB, _U, _CG, _GA, _BG = (0, 128, 256, 512, 768, 1024, 1280, 1536, 1792)


def _params(n_grid):
    return pltpu.CompilerParams(dimension_semantics=("arbitrary",) * n_grid,
                                vmem_limit_bytes=V7X_VMEM_LIMIT)


def _log_sigmoid(x):
    return jnp.minimum(x, 0.0) - jnp.log1p(jnp.exp(-jnp.abs(x)))


def _rms(x, g):
    return x * lax.rsqrt(jnp.mean(x * x, axis=-1, keepdims=True) + EPS) * g


_O_Q = 3 * D_CONV
_O_K = _O_Q + N_HEADS * DK
_O_V = _O_K + N_HEADS * DK
_O_O = _O_V + N_HEADS * DV
_O_GIF = _O_O + N_HEADS * DV
_O_GA = _O_GIF + 2 * N_HEADS
_O_GB = _O_GA + D_CONV


def _prep_win_kernel(u_ref, cg_ref, bg_ref, ga_ref, v_ref, o_ref, gb_ref, q_ref, k_ref,
                     hn_ref, gates_ref, wsc_ref,
                     wp_ref, mix_ref, cout_ref, nout_ref, mout_ref, scout_ref,
                     z_ref, cu_ref, *, n_valid):
    for ref, off, d in ((u_ref, _U, DV), (cg_ref, _CG, DV), (bg_ref, _BG, DV), (ga_ref, _GA, DV),
                        (v_ref, _V, DV), (o_ref, _O, DV), (gb_ref, _GB, DV), (q_ref, _Q, DK), (k_ref, _K, DK)):
        wp_ref[:, off:off + d] = ref[...].T.astype(BF16)
    z_ref[...] = jnp.dot(hn_ref[...], wp_ref[...], preferred_element_type=F32)

    row = lax.broadcasted_iota(jnp.int32, (CHUNK, CHUNK), 0)
    col = lax.broadcasted_iota(jnp.int32, (CHUNK, CHUNK), 1)
    causal = col <= row
    eye = col == row
    a_row = gates_ref[0:1, :]
    b_row = gates_ref[1:2, :]
    a_col = jnp.sum(jnp.where(eye, a_row, 0.0), axis=-1, keepdims=True)
    b_col = jnp.sum(jnp.where(eye, b_row, 0.0), axis=-1, keepdims=True)
    dmat = jnp.where(causal, b_col + a_row, NEG)
    m_t = jnp.maximum(b_col, jnp.max(dmat, axis=-1, keepdims=True))
    q = z_ref[:, _Q:_Q + DK]
    k = z_ref[:, _K:_K + DK] * K_SCALE
    v = z_ref[:, _V:_V + DV]
    qk = lax.dot_general(q.astype(BF16), k.astype(BF16), (((1,), (1,)), ((), ())), preferred_element_type=F32)
    s = qk * jnp.exp(dmat - m_t)
    num = jnp.dot(s.astype(BF16), v.astype(BF16), preferred_element_type=F32)
    den = jnp.sum(s, axis=-1, keepdims=True)
    bmix = num / jnp.maximum(jnp.abs(den), jnp.exp(-m_t))
    m_last = m_t[CHUNK - 1:CHUNK, :]
    g_col = jnp.exp(b_col[CHUNK - 1:CHUNK, :] + a_col - m_last)
    cout_ref[...] = jnp.dot(k.T.astype(BF16), (g_col * v).astype(BF16), preferred_element_type=F32)
    nout_ref[...] = jnp.sum(g_col * k, axis=0, keepdims=True)
    mout_ref[...] = jnp.broadcast_to(m_last, (1, DK))

    cu = z_ref[:, _CG:_CG + DV] * z_ref[:, _U:_U + DV]
    cu_ref[0:HALO, :] = jnp.zeros((HALO, DV), F32)
    cu_ref[HALO:HALO + CHUNK, :] = cu
    taps = (cu * wsc_ref[2:3, :]
            + cu_ref[HALO - 1:HALO - 1 + CHUNK, :] * wsc_ref[1:2, :]
            + cu_ref[HALO - 2:HALO - 2 + CHUNK, :] * wsc_ref[0:1, :])
    amix = jax.nn.sigmoid(z_ref[:, _GA:_GA + DV]) * taps
    og = jax.nn.sigmoid(z_ref[:, _GB:_GB + DV]) * jax.nn.sigmoid(z_ref[:, _O:_O + DV])
    mix_ref[...] = (amix * z_ref[:, _BG:_BG + DV] + og * bmix).astype(BF16)
    scout_ref[...] = cu_ref[HALO + n_valid - 2:HALO + n_valid, :]


def _gate_cols(wt):
    gif = wt[_O_GIF:_O_GIF + 2 * N_HEADS].T
    return jnp.pad(gif, ((0, 0), (0, LANES - 2 * N_HEADS))).astype(BF16)


def _prep_win(wt, hn, gates, wsc, *, n_valid):
    assert _O_GA % 8 == 0
    rows = lambda off, d: pl.BlockSpec((pl.Element(d), pl.Element(D_MODEL)),
                                       lambda h: (pl.multiple_of(off + d * h, 8), 0))
    return pl.pallas_call(
        functools.partial(_prep_win_kernel, n_valid=n_valid),
        grid=(N_HEADS,),
        in_specs=[rows(0, DV), rows(D_CONV, DV), rows(2 * D_CONV, DV), rows(_O_GA, DV), rows(_O_V, DV),
                  rows(_O_O, DV), rows(_O_GB, DV), rows(_O_Q, DK), rows(_O_K, DK),
                  pl.BlockSpec((CHUNK, D_MODEL), lambda h: (0, 0)),
                  pl.BlockSpec((None, 2, CHUNK), lambda h: (h, 0, 0)),
                  pl.BlockSpec((CONV_W, DV), lambda h: (0, h))],
        out_specs=[pl.BlockSpec((D_MODEL, HEAD_COLS), lambda h: (0, h)),
                   pl.BlockSpec((CHUNK, DV), lambda h: (0, h)),
                   pl.BlockSpec((None, DK, DV), lambda h: (h, 0, 0)),
                   pl.BlockSpec((None, 1, DK), lambda h: (h, 0, 0)),
                   pl.BlockSpec((None, 1, DK), lambda h: (h, 0, 0)),
                   pl.BlockSpec((2, DV), lambda h: (0, h))],
        out_shape=[jax.ShapeDtypeStruct((D_MODEL, N_HEADS * HEAD_COLS), BF16),
                   jax.ShapeDtypeStruct((CHUNK, D_MODEL), BF16),
                   jax.ShapeDtypeStruct((N_HEADS, DK, DV), F32),
                   jax.ShapeDtypeStruct((N_HEADS, 1, DK), F32),
                   jax.ShapeDtypeStruct((N_HEADS, 1, DK), F32),
                   jax.ShapeDtypeStruct((2, D_CONV), F32)],
        scratch_shapes=[pltpu.VMEM((CHUNK, HEAD_COLS), F32),
                        pltpu.VMEM((HALO + CHUNK, DV), F32)],
        compiler_params=_params(1),
        name="prep_win",
    )(*([wt] * 9), hn, gates, wsc)


def _norm_gates_seq_kernel(x_ref, g_ref, wg_ref, bif_ref, hn_ref, gates_ref, *, tm, n_valid):
    hn = _rms(x_ref[...], g_ref[...]).astype(BF16)
    hn_ref[...] = hn
    gif = jnp.dot(hn, wg_ref[...], preferred_element_type=F32) + bif_ref[...]
    row = lax.broadcasted_iota(jnp.int32, (CHUNK, CHUNK), 0)
    col = lax.broadcasted_iota(jnp.int32, (CHUNK, CHUNK), 1)
    upper = (row <= col).astype(F32)
    for c in range(tm // CHUNK):
        gt = gif[c * CHUNK:(c + 1) * CHUNK, :].T
        ig = gt[0:N_HEADS, :]
        lf = _log_sigmoid(gt[N_HEADS:2 * N_HEADS, :])
        if n_valid < CHUNK:
            tok = lax.broadcasted_iota(jnp.int32, (N_HEADS, CHUNK), 1)
            ig = jnp.where(tok < n_valid, ig, NEG)
            lf = jnp.where(tok < n_valid, lf, 0.0)
        b = jnp.dot(lf, upper, preferred_element_type=F32, precision=lax.Precision.HIGHEST)
        a = ig - b
        for j in range(N_HEADS):
            gates_ref[j, 0:1, c * CHUNK:(c + 1) * CHUNK] = a[j:j + 1, :]
            gates_ref[j, 1:2, c * CHUNK:(c + 1) * CHUNK] = b[j:j + 1, :]


def _norm_gates_seq(x, g, wg, bif, *, tm, n_valid):
    t = x.shape[0]
    return pl.pallas_call(
        functools.partial(_norm_gates_seq_kernel, tm=tm, n_valid=n_valid),
        grid=(t // tm,),
        in_specs=[pl.BlockSpec((tm, D_MODEL), lambda r: (r, 0)),
                  pl.BlockSpec((1, D_MODEL), lambda r: (0, 0)),
                  pl.BlockSpec((D_MODEL, LANES), lambda r: (0, 0)),
                  pl.BlockSpec((1, LANES), lambda r: (0, 0))],
        out_specs=[pl.BlockSpec((tm, D_MODEL), lambda r: (r, 0)),
                   pl.BlockSpec((N_HEADS, 2, tm), lambda r: (0, 0, r))],
        out_shape=[jax.ShapeDtypeStruct((t, D_MODEL), BF16),
                   jax.ShapeDtypeStruct((N_HEADS, 2, t), F32)],
        compiler_params=_params(1),
        name="norm_gates_seq",
    )(x, g, wg, bif)


def _norm_gates_step_kernel(x_ref, g_ref, wg_ref, bif_ref, hn_ref, gif_ref):
    hn = _rms(x_ref[...], g_ref[...]).astype(BF16)
    hn_ref[...] = hn
    gif_ref[...] = jnp.dot(hn, wg_ref[...], preferred_element_type=F32) + bif_ref[...]


def _norm_gates_step(x, g, wg, bif):
    t = x.shape[0]
    return pl.pallas_call(
        _norm_gates_step_kernel,
        grid=(1,),
        in_specs=[pl.BlockSpec((t, D_MODEL), lambda r: (0, 0)),
                  pl.BlockSpec((1, D_MODEL), lambda r: (0, 0)),
                  pl.BlockSpec((D_MODEL, LANES), lambda r: (0, 0)),
                  pl.BlockSpec((1, LANES), lambda r: (0, 0))],
        out_specs=[pl.BlockSpec((t, D_MODEL), lambda r: (0, 0)),
                   pl.BlockSpec((t, LANES), lambda r: (0, 0))],
        out_shape=[jax.ShapeDtypeStruct((t, D_MODEL), BF16),
                   jax.ShapeDtypeStruct((t, LANES), F32)],
        compiler_params=_params(1),
        name="norm_gates_step",
    )(x, g, wg, bif)


def _mixers_kernel(*refs, tm, tiles_per_batch, n_valid, step_bb):
    n_seq = (8, 5, 12)
    n_step = (9, 8, 5) if step_bb else (0, 0, 0)
    cuts = [0]
    for a, b in zip(n_seq, n_step):
        cuts += [cuts[-1] + a, cuts[-1] + a + b]
    seq_in, step_in, seq_out, step_out, seq_scr, step_scr = (refs[i:j] for i, j in zip(cuts[:-1], cuts[1:]))
    hn_ref, w_ref, gates_ref, wsc_ref, c0_ref, n0_ref, m0_ref, sc0_ref = seq_in
    mix_ref, cout_ref, nout_ref, mout_ref, scout_ref = seq_out
    z_ref, cu_ref, c_ref, n_ref, m_ref, dmat_ref, qk_ref, sv_ref, kv_ref, bmix_ref, amix_ref, og_ref = seq_scr
    head = pl.program_id(0)
    tile = pl.program_id(1)
    rb = tile % tiles_per_batch
    nc = tm // CHUNK

    @pl.when(rb == 0)
    def _():
        c_ref[...] = c0_ref[...]
        n_ref[...] = n0_ref[...]
        m_ref[...] = m0_ref[...]
        cu_ref[0:HALO, :] = sc0_ref[...]

    row = lax.broadcasted_iota(jnp.int32, (CHUNK, CHUNK), 0)
    col = lax.broadcasted_iota(jnp.int32, (CHUNK, CHUNK), 1)
    causal = col <= row
    eye = col == row
    st = [dict() for _ in range(nc)]
    rows_of = lambda c: slice(c * CHUNK, (c + 1) * CHUNK)
    k_of = lambda rows: z_ref[rows, _K:_K + DK] * K_SCALE

    def proj(j):
        cols = slice(j * DV, (j + 1) * DV)
        z_ref[:, cols] = jnp.dot(hn_ref[...], w_ref[:, cols], preferred_element_type=F32)

    def gate_terms(c):
        rows, d = rows_of(c), st[c]
        a_row = gates_ref[0:1, rows]
        b_row = gates_ref[1:2, rows]
        d["a_col"] = jnp.sum(jnp.where(eye, a_row, 0.0), axis=-1, keepdims=True)
        d["b_col"] = jnp.sum(jnp.where(eye, b_row, 0.0), axis=-1, keepdims=True)
        dmat = jnp.where(causal, d["b_col"] + a_row, NEG)
        dmat_ref[rows, :] = dmat
        d["mloc"] = jnp.max(dmat, axis=-1, keepdims=True)
        qk_ref[rows, :] = lax.dot_general(z_ref[rows, _Q:_Q + DK].astype(BF16), k_of(rows).astype(BF16),
                                          (((1,), (1,)), ((), ())), preferred_element_type=F32)

    def stabilisers():
        m_prev = m_ref[:, 0:1]
        for c in range(nc):
            d = st[c]
            d["m_prev"] = m_prev
            d["bm"] = d["b_col"] + m_prev
            d["m_t"] = jnp.maximum(d["bm"], d["mloc"])
            d["m_last"] = d["m_t"][CHUNK - 1:CHUNK, :]
            m_prev = d["m_last"]
        m_ref[...] = jnp.broadcast_to(m_prev, (1, DK))

    def intra(c):
        rows, d = rows_of(c), st[c]
        k = k_of(rows)
        v = z_ref[rows, _V:_V + DV]
        s = qk_ref[rows, :] * jnp.exp(dmat_ref[rows, :] - d["m_t"])
        sv_ref[rows, :] = jnp.dot(s.astype(BF16), v.astype(BF16), preferred_element_type=F32)
        d["ssum"] = jnp.sum(s, axis=-1, keepdims=True)
        d["decay"] = jnp.exp(d["bm"] - d["m_t"])
        d["hden"] = jnp.exp(-d["m_t"])
        b_last = d["b_col"][CHUNK - 1:CHUNK, :]
        g_col = jnp.exp(b_last + d["a_col"] - d["m_last"])
        d["carry"] = jnp.exp(b_last + d["m_prev"] - d["m_last"])
        kv_ref[c] = jnp.dot(k.T.astype(BF16), (g_col * v).astype(BF16), preferred_element_type=F32)
        d["kn"] = jnp.sum(g_col * k, axis=0, keepdims=True)

    def inter(c):
        rows, d = rows_of(c), st[c]
        q = z_ref[rows, _Q:_Q + DK]
        c_prev = c_ref[...]
        n_prev = n_ref[...]
        num = (d["decay"] * jnp.dot(q.astype(BF16), c_prev.astype(BF16), preferred_element_type=F32)
               + sv_ref[rows, :])
        den = d["decay"] * jnp.sum(q * n_prev, axis=-1, keepdims=True) + d["ssum"]
        bmix_ref[rows, :] = num / jnp.maximum(jnp.abs(den), d["hden"])
        c_ref[...] = d["carry"] * c_prev + kv_ref[c]
        n_ref[...] = d["carry"] * n_prev + d["kn"]

    def out_gate(c):
        rows = rows_of(c)
        og_ref[rows, :] = jax.nn.sigmoid(z_ref[rows, _GB:_GB + DV]) * jax.nn.sigmoid(z_ref[rows, _O:_O + DV])

    def conv(c):
        lo = c * CHUNK
        rows = rows_of(c)
        cu = z_ref[rows, _CG:_CG + DV] * z_ref[rows, _U:_U + DV]
        cu_ref[HALO + lo:HALO + lo + CHUNK, :] = cu
        amix_ref[rows, :] = (cu * wsc_ref[2:3, :]
                             + cu_ref[HALO - 1 + lo:HALO - 1 + lo + CHUNK, :] * wsc_ref[1:2, :]
                             + cu_ref[HALO - 2 + lo:HALO - 2 + lo + CHUNK, :] * wsc_ref[0:1, :])

    def gate(c):
        rows = rows_of(c)
        amix_ref[rows, :] = jax.nn.sigmoid(z_ref[rows, _GA:_GA + DV]) * amix_ref[rows, :]

    def merge(c):
        rows = rows_of(c)
        mix = amix_ref[rows, :] * z_ref[rows, _BG:_BG + DV] + og_ref[rows, :] * bmix_ref[rows, :]
        mix_ref[rows, :] = mix.astype(BF16)

    piece = lambda off: off // DV
    assert (piece(_Q), piece(_V), piece(_O), piece(_GB)) == (0, 1, 2, 3)
    assert sorted(map(piece, (_U, _CG))) == [4, 5] and (piece(_GA), piece(_BG)) == (6, 7)
    each = lambda fn: [functools.partial(fn, c) for c in range(nc)]
    half = (nc + 1) // 2
    after = {0: [],
             1: each(gate_terms),
             2: [stabilisers] + each(intra)[:half] + each(inter)[:1],
             3: each(intra)[half:] + each(inter)[1:3] + each(out_gate)[:half],
             4: each(inter)[3:6] + each(out_gate)[half:],
             5: each(inter)[6:] + each(conv),
             6: each(gate),
             7: each(merge)}

    if step_bb:
        bb = step_bb
        hns_ref, gifs_ref, mprevs_ref, sc_ref, cs_ref, ns_ref, wu32_ref, wd32_ref, wo32_ref = step_in
        mixs_ref, csout_ref, nsout_ref, msout_ref, scnew_ref, wu16_ref, wd16_ref, wo16_ref = step_out
        zs_ref, aouts_ref, qts_ref, kts_ref, hbs_ref = step_scr

        def round_weights():
            wu16_ref[...] = wu32_ref[...].astype(BF16)
            wd16_ref[...] = wd32_ref[...].astype(BF16)
            wo16_ref[...] = wo32_ref[...].astype(BF16)

        @pl.when(tile == 0)
        def _():
            z = jnp.dot(hns_ref[...], w_ref[...], preferred_element_type=F32)
            zs_ref[...] = z
            cu = z[:, _CG:_CG + DV] * z[:, _U:_U + DV]
            s0 = sc_ref[:, 0, :]
            s1 = sc_ref[:, 1, :]
            taps = s0 * wsc_ref[0:1, :] + s1 * wsc_ref[1:2, :] + cu * wsc_ref[2:3, :]
            aouts_ref[...] = z[:, _BG:_BG + DV] * taps
            scnew_ref[:, 0, :] = s1
            scnew_ref[:, 1, :] = cu
            qt = z[:, _Q:_Q + DK].T
            kt = (z[:, _K:_K + DK] * K_SCALE).T
            for jj in range(hns_ref.shape[0] // bb):
                qts_ref[jj] = qt[:, jj * bb:(jj + 1) * bb]
                kts_ref[jj] = kt[:, jj * bb:(jj + 1) * bb]

        srows = pl.ds(pl.multiple_of(tile * bb, bb), bb)
        sd = {}

        def step_gates():
            lane = lax.broadcasted_iota(jnp.int32, (bb, LANES), 1)
            pick = lambda x, idx: jnp.sum(jnp.where(lane == idx, x, 0.0), axis=-1, keepdims=True)
            gif = gifs_ref[srows, :]
            ig = pick(gif, head)
            lf = _log_sigmoid(pick(gif, head + N_HEADS))
            m_prev = pick(mprevs_ref[srows, :], head)
            q = zs_ref[srows, _Q:_Q + DK]
            k = zs_ref[srows, _K:_K + DK] * K_SCALE
            n_prev = ns_ref[...]
            m_t = jnp.maximum(lf + m_prev, ig)
            sd["w"] = jnp.exp(ig - m_t)
            sd["decay"] = jnp.exp(lf + m_prev - m_t)
            sd["s"] = jnp.sum(q * k, axis=-1, keepdims=True) * sd["w"]
            den = sd["decay"] * jnp.sum(q * n_prev, axis=-1, keepdims=True) + sd["s"]
            sd["inv"] = 1.0 / jnp.maximum(jnp.abs(den), jnp.exp(-m_t))
            nsout_ref[...] = sd["decay"] * n_prev + sd["w"] * k
            msout_ref[...] = jnp.broadcast_to(m_t, (bb, DK))

        def step_rows(lo, hi):
            qt = qts_ref[tile]
            kt = kts_ref[tile]
            v = zs_ref[srows, _V:_V + DV]
            for i in range(lo, hi):
                c_prev = cs_ref[i]
                d_i = sd["decay"][i:i + 1, :]
                v_i = v[i:i + 1, :]
                qc = jnp.sum(qt[:, i:i + 1] * c_prev, axis=0, keepdims=True)
                hbs_ref[i:i + 1, :] = (d_i * qc + sd["s"][i:i + 1, :] * v_i) * sd["inv"][i:i + 1, :]
                csout_ref[i] = d_i * c_prev + (sd["w"][i:i + 1, :] * kt[:, i:i + 1]) * v_i

        def step_merge():
            b_out = jax.nn.sigmoid(zs_ref[srows, _O:_O + DV]) * hbs_ref[...]
            mixs_ref[...] = (jax.nn.sigmoid(zs_ref[srows, _GA:_GA + DV]) * aouts_ref[srows, :]
                             + jax.nn.sigmoid(zs_ref[srows, _GB:_GB + DV]) * b_out)

        after[0] += [round_weights, step_gates, functools.partial(step_rows, 0, bb // 2)]
        after[5] += [functools.partial(step_rows, bb // 2, bb)]
        after[6] = [step_merge] + after[6]

    for j in range(HEAD_COLS // DV):
        proj(j)
        for fn in after.get(j, []):
            fn()

    @pl.when(rb == tiles_per_batch - 1)
    def _():
        cout_ref[...] = c_ref[...]
        nout_ref[...] = n_ref[...]
        mout_ref[...] = m_ref[...]
        scout_ref[...] = cu_ref[HALO + n_valid - 2:HALO + n_valid, :]

    cu_ref[0:HALO, :] = cu_ref[tm:tm + HALO, :]


def _mixers(hn, wp, gates, wsc, c0, n0, m0, sc0, step=None, *, n_batch, tm, n_valid):
    t = hn.shape[0]
    n_tiles = t // tm
    tpb = n_tiles // n_batch
    in_specs = [pl.BlockSpec((tm, D_MODEL), lambda h, r: (r, 0)),
                pl.BlockSpec((D_MODEL, HEAD_COLS), lambda h, r: (0, h)),
                pl.BlockSpec((None, 2, tm), lambda h, r: (h, 0, r)),
                pl.BlockSpec((CONV_W, DV), lambda h, r: (0, h)),
                pl.BlockSpec((None, DK, DV), lambda h, r: (h, 0, 0)),
                pl.BlockSpec((None, 1, DK), lambda h, r: (h, 0, 0)),
                pl.BlockSpec((None, 1, DK), lambda h, r: (h, 0, 0)),
                pl.BlockSpec((HALO, DV), lambda h, r: (0, h))]
    out_specs = [pl.BlockSpec((tm, DV), lambda h, r: (r, h)),
                 pl.BlockSpec((None, None, DK, DV), lambda h, r: (r // tpb, h, 0, 0)),
                 pl.BlockSpec((None, None, 1, DK), lambda h, r: (r // tpb, h, 0, 0)),
                 pl.BlockSpec((None, None, 1, DK), lambda h, r: (r // tpb, h, 0, 0)),
                 pl.BlockSpec((None, 2, DV), lambda h, r: (r // tpb, 0, h))]
    out_shape = [jax.ShapeDtypeStruct((t, D_MODEL), BF16),
                 jax.ShapeDtypeStruct((n_batch, N_HEADS, DK, DV), F32),
                 jax.ShapeDtypeStruct((n_batch, N_HEADS, 1, DK), F32),
                 jax.ShapeDtypeStruct((n_batch, N_HEADS, 1, DK), F32),
                 jax.ShapeDtypeStruct((n_batch, 2, D_CONV), F32)]
    scratch = [pltpu.VMEM((tm, HEAD_COLS), F32),
               pltpu.VMEM((tm + HALO, DV), F32),
               pltpu.VMEM((DK, DV), F32),
               pltpu.VMEM((1, DK), F32),
               pltpu.VMEM((1, DK), F32),
               pltpu.VMEM((tm, CHUNK), F32),
               pltpu.VMEM((tm, CHUNK), F32),
               pltpu.VMEM((tm, DV), F32),
               pltpu.VMEM((tm // CHUNK, DK, DV), F32),
               pltpu.VMEM((tm, DV), F32),
               pltpu.VMEM((tm, DV), F32),
               pltpu.VMEM((tm, DV), F32)]
    step_bb = 0
    if step is not None:
        nb = step[0].shape[0]
        assert nb % n_tiles == 0
        step_bb = nb // n_tiles
        in_specs += [pl.BlockSpec((nb, D_MODEL), lambda h, r: (0, 0)),
                     pl.BlockSpec((nb, LANES), lambda h, r: (0, 0)),
                     pl.BlockSpec((nb, LANES), lambda h, r: (0, 0)),
                     pl.BlockSpec((None, nb, CONV_W - 1, DV), lambda h, r: (0, 0, 0, h)),
                     pl.BlockSpec((step_bb, None, DK, DV), lambda h, r: (r, h, 0, 0)),
                     pl.BlockSpec((None, step_bb, DK), lambda h, r: (h, r, 0))]
        n_steps = N_HEADS * n_tiles
        wu32, wd32, wo32 = step[6], step[7], step[8]
        assert wu32.shape[0] % (16 * n_steps) == 0 and wd32.shape[0] % (8 * n_steps) == 0
        assert wo32.shape[0] % (16 * n_steps) == 0
        wu_rows = wu32.shape[0] // n_steps
        wd_rows, wd_cols = 2 * wd32.shape[0] // n_steps, wd32.shape[1] // 2
        assert wd_rows % 16 == 0
        wu_spec = pl.BlockSpec((wu_rows, wu32.shape[1]), lambda h, r: (h * n_tiles + r, 0))
        wd_spec = pl.BlockSpec((wd_rows, wd_cols), lambda h, r: ((h * n_tiles + r) // 2, (h * n_tiles + r) % 2))
        wo_spec = pl.BlockSpec((wo32.shape[0] // n_steps, wo32.shape[1]), lambda h, r: (h * n_tiles + r, 0))
        in_specs += [wu_spec, wd_spec, wo_spec]
        out_specs += [pl.BlockSpec((step_bb, DV), lambda h, r: (r, h)),
                      pl.BlockSpec((step_bb, None, DK, DV), lambda h, r: (r, h, 0, 0)),
                      pl.BlockSpec((None, step_bb, DK), lambda h, r: (h, r, 0)),
                      pl.BlockSpec((None, step_bb, DK), lambda h, r: (h, r, 0)),
                      pl.BlockSpec((None, nb, CONV_W - 1, DV), lambda h, r: (0, 0, 0, h)),
                      wu_spec, wd_spec, wo_spec]
        out_shape += [jax.ShapeDtypeStruct((nb, D_MODEL), F32),
                      jax.ShapeDtypeStruct((nb, N_HEADS, DK, DV), F32),
                      jax.ShapeDtypeStruct((N_HEADS, nb, DK), F32),
                      jax.ShapeDtypeStruct((N_HEADS, nb, DK), F32),
                      jax.ShapeDtypeStruct((1, nb, CONV_W - 1, D_CONV), F32),
                      jax.ShapeDtypeStruct(wu32.shape, BF16),
                      jax.ShapeDtypeStruct(wd32.shape, BF16),
                      jax.ShapeDtypeStruct(wo32.shape, BF16)]
        scratch += [pltpu.VMEM((nb, HEAD_COLS), F32),
                    pltpu.VMEM((nb, DV), F32),
                    pltpu.VMEM((n_tiles, DK, step_bb), F32),
                    pltpu.VMEM((n_tiles, DK, step_bb), F32),
                    pltpu.VMEM((step_bb, DV), F32)]
    return pl.pallas_call(
        functools.partial(_mixers_kernel, tm=tm, tiles_per_batch=tpb, n_valid=n_valid, step_bb=step_bb),
        grid=(N_HEADS, n_tiles),
        in_specs=in_specs,
        out_specs=out_specs,
        out_shape=out_shape,
        scratch_shapes=scratch,
        compiler_params=_params(2),
        name="mixers",
    )(hn, wp, gates, wsc, c0, n0, m0, sc0, *(step or ()))


def _mixer_residual(x_ref, mix_ref, wo_ref, g2_ref, acc_ref, h2_ref):
    mix = mix_ref[...]
    if mix.dtype != BF16:
        mix = mix.astype(BF16)
    acc_ref[...] = x_ref[...] + jnp.dot(mix, wo_ref[...], preferred_element_type=F32)
    rows = acc_ref.shape[0]
    for lo in range(0, rows, FFN_SUB):
        hi = min(lo + FFN_SUB, rows)
        h2_ref[lo:hi, :] = _rms(acc_ref[lo:hi, :], g2_ref[...]).astype(BF16)


def _final_norm(acc_ref, g_ref, y_ref):
    rows = y_ref.shape[0]
    for lo in range(0, rows, FFN_SUB):
        hi = min(lo + FFN_SUB, rows)
        y_ref[lo:hi, :] = _rms(acc_ref[lo:hi, :], g_ref[...])


def _ffn_seq_kernel(x_ref, mix_ref, wo_ref, g2_ref, wua_ref, wub_ref, wca_ref, wcb_ref, wd_ref, g_ref,
                    ha0_ref, hb0_ref,
                    y_ref, fca_ref, fcb_ref,
                    acc_ref, h2_ref, ua_ref, ub_ref, ha_ref, hb_ref, act_ref,
                    *, tm, tiles_per_batch, n_valid, n_f):
    r = pl.program_id(0)
    f = pl.program_id(1)
    rb = r % tiles_per_batch

    @pl.when((r == 0) & (f == 0))
    def _():
        ha_ref[...] = jnp.zeros_like(ha_ref)
        hb_ref[...] = jnp.zeros_like(hb_ref)

    @pl.when(f == 0)
    def _():
        _mixer_residual(x_ref, mix_ref, wo_ref, g2_ref, acc_ref, h2_ref)

    ua_ref[0:HALO, :] = jnp.where(rb == 0, ha0_ref[...], ha_ref[f])
    ub_ref[0:HALO, :] = jnp.where(rb == 0, hb0_ref[...], hb_ref[f])
    group = tm // DOWN_GROUPS

    def up_proj(gi):
        rows = slice(gi * group, (gi + 1) * group)
        urows = slice(HALO + gi * group, HALO + (gi + 1) * group)
        ua_ref[urows, :] = jnp.dot(h2_ref[rows, :], wua_ref[...], preferred_element_type=F32)
        ub_ref[urows, :] = jnp.dot(h2_ref[rows, :], wub_ref[...], preferred_element_type=F32)

    def conv(u_ref, wc_ref, lo):
        win = u_ref[lo:lo + HALO + FFN_SUB, :]
        back1 = pltpu.roll(win, 1, axis=0)[HALO:, :]
        back2 = pltpu.roll(win, 2, axis=0)[HALO:, :]
        return win[HALO:, :] * wc_ref[2:3, :] + back1 * wc_ref[1:2, :] + back2 * wc_ref[0:1, :]

    def glu_down(gi):
        for lo in range(gi * group, (gi + 1) * group, FFN_SUB):
            up_a = conv(ua_ref, wca_ref, lo)
            up_b = conv(ub_ref, wcb_ref, lo)
            act_ref[lo:lo + FFN_SUB, :] = (up_b * jax.nn.sigmoid(up_b) * up_a).astype(BF16)
        rows = slice(gi * group, (gi + 1) * group)
        acc_ref[rows, :] += jnp.dot(act_ref[rows, :], wd_ref[...], preferred_element_type=F32)

    up_proj(0)
    for gi in range(1, DOWN_GROUPS):
        up_proj(gi)
        glu_down(gi - 1)
    glu_down(DOWN_GROUPS - 1)
    for u_ref, halo_ref, tail_ref in ((ua_ref, ha_ref, fca_ref), (ub_ref, hb_ref, fcb_ref)):
        halo_ref[f] = u_ref[tm:tm + HALO, :]
        tail_ref[f] = u_ref[HALO + n_valid - 2:HALO + n_valid, :]

    @pl.when(f == n_f - 1)
    def _():
        _final_norm(acc_ref, g_ref, y_ref)


def _ffn_seq(x, mix, wo, g2, wu, wc, wd, g, halo_a, halo_b, *, n_batch, tm, tf, n_valid):
    t = x.shape[0]
    tpb = t // n_batch // tm
    n_f = D_FF // tf
    return pl.pallas_call(
        functools.partial(_ffn_seq_kernel, tm=tm, tiles_per_batch=tpb, n_valid=n_valid, n_f=n_f),
        grid=(t // tm, n_f),
        in_specs=[pl.BlockSpec((tm, D_MODEL), lambda r, f: (r, 0)),
                  pl.BlockSpec((tm, D_MODEL), lambda r, f: (r, 0)),
                  pl.BlockSpec((D_MODEL, D_MODEL), lambda r, f: (0, 0), pipeline_mode=pl.Buffered(1)),
                  pl.BlockSpec((1, D_MODEL), lambda r, f: (0, 0)),
                  pl.BlockSpec((D_MODEL, tf), lambda r, f: (0, f)),
                  pl.BlockSpec((D_MODEL, tf), lambda r, f: (0, n_f + f)),
                  pl.BlockSpec((CONV_W, tf), lambda r, f: (0, f)),
                  pl.BlockSpec((CONV_W, tf), lambda r, f: (0, n_f + f)),
                  pl.BlockSpec((tf, D_MODEL), lambda r, f: (f, 0)),
                  pl.BlockSpec((1, D_MODEL), lambda r, f: (0, 0)),
                  pl.BlockSpec((HALO, tf), lambda r, f: (0, f)),
                  pl.BlockSpec((HALO, tf), lambda r, f: (0, f))],
        out_specs=[pl.BlockSpec((tm, D_MODEL), lambda r, f: (r, 0)),
                   pl.BlockSpec((None, n_f, 2, tf), lambda r, f: (r // tpb, 0, 0, 0)),
                   pl.BlockSpec((None, n_f, 2, tf), lambda r, f: (r // tpb, 0, 0, 0))],
        out_shape=[jax.ShapeDtypeStruct((t, D_MODEL), F32),
                   jax.ShapeDtypeStruct((n_batch, n_f, 2, tf), F32),
                   jax.ShapeDtypeStruct((n_batch, n_f, 2, tf), F32)],
        scratch_shapes=[pltpu.VMEM((tm, D_MODEL), F32),
                        pltpu.VMEM((tm, D_MODEL), BF16),
                        pltpu.VMEM((tm + HALO, tf), F32),
                        pltpu.VMEM((tm + HALO, tf), F32),
                        pltpu.VMEM((n_f, HALO, tf), F32),
                        pltpu.VMEM((n_f, HALO, tf), F32),
                        pltpu.VMEM((tm, tf), BF16)],
        compiler_params=_params(2),
        name="ffn_seq",
    )(x, mix, wo, g2, wu, wu, wc, wc, wd, g, halo_a, halo_b)


def _ffn_small_kernel(x_ref, mix_ref, wo_ref, g2_ref, wua_ref, wub_ref, wca_ref, wcb_ref, wd_ref, g_ref,
                      sa_ref, sb_ref,
                      y_ref, sa_new_ref, sb_new_ref, taila_ref, tailb_ref,
                      acc_ref, h2_ref, *, n_f, n_step, n_tail):
    f = pl.program_id(0)

    @pl.when(f == 0)
    def _():
        _mixer_residual(x_ref, mix_ref, wo_ref, g2_ref, acc_ref, h2_ref)

    def half(wu_ref, wc_ref, s_ref, s_new_ref, tail_ref):
        up = jnp.dot(h2_ref[...], wu_ref[...], preferred_element_type=F32)
        s0 = s_ref[:, 0, :]
        s1 = s_ref[:, 1, :]
        s_new_ref[:, 0, :] = s1
        s_new_ref[:, 1, :] = up[0:n_step, :]
        tail_ref[f] = up[n_step + n_tail - 2:n_step + n_tail, :]
        return s0 * wc_ref[0:1, :] + s1 * wc_ref[1:2, :] + up[0:n_step, :] * wc_ref[2:3, :]

    up_a = half(wua_ref, wca_ref, sa_ref, sa_new_ref, taila_ref)
    up_b = half(wub_ref, wcb_ref, sb_ref, sb_new_ref, tailb_ref)
    act = (up_b * jax.nn.sigmoid(up_b) * up_a).astype(BF16)
    acc_ref[0:n_step, :] += jnp.dot(act, wd_ref[...], preferred_element_type=F32)

    @pl.when(f == n_f - 1)
    def _():
        _final_norm(acc_ref, g_ref, y_ref)


def _ffn_small(x, mix, wo, g2, wu, wc, wd, g, s, *, tf, n_step):
    rows = x.shape[0]
    n_f = D_FF // tf
    lo = lambda f: (0, f)
    hi = lambda f: (0, n_f + f)
    return pl.pallas_call(
        functools.partial(_ffn_small_kernel, n_f=n_f, n_step=n_step, n_tail=rows - n_step),
        grid=(n_f,),
        in_specs=[pl.BlockSpec((rows, D_MODEL), lambda f: (0, 0)),
                  pl.BlockSpec((rows, D_MODEL), lambda f: (0, 0)),
                  pl.BlockSpec((D_MODEL, D_MODEL), lambda f: (0, 0), pipeline_mode=pl.Buffered(1)),
                  pl.BlockSpec((1, D_MODEL), lambda f: (0, 0)),
                  pl.BlockSpec((D_MODEL, tf), lo),
                  pl.BlockSpec((D_MODEL, tf), hi),
                  pl.BlockSpec((CONV_W, tf), lo),
                  pl.BlockSpec((CONV_W, tf), hi),
                  pl.BlockSpec((tf, D_MODEL), lambda f: (f, 0)),
                  pl.BlockSpec((1, D_MODEL), lambda f: (0, 0)),
                  pl.BlockSpec((None, n_step, CONV_W - 1, tf), lambda f: (0, 0, 0, f)),
                  pl.BlockSpec((None, n_step, CONV_W - 1, tf), lambda f: (0, 0, 0, n_f + f))],
        out_specs=[pl.BlockSpec((n_step, D_MODEL), lambda f: (0, 0)),
                   pl.BlockSpec((n_step, CONV_W - 1, tf), lambda f: (0, 0, f)),
                   pl.BlockSpec((n_step, CONV_W - 1, tf), lambda f: (0, 0, f)),
                   pl.BlockSpec((n_f, 2, tf), lambda f: (0, 0, 0)),
                   pl.BlockSpec((n_f, 2, tf), lambda f: (0, 0, 0))],
        out_shape=[jax.ShapeDtypeStruct((n_step, D_MODEL), F32),
                   jax.ShapeDtypeStruct((n_step, CONV_W - 1, D_FF), F32),
                   jax.ShapeDtypeStruct((n_step, CONV_W - 1, D_FF), F32),
                   jax.ShapeDtypeStruct((n_f, 2, tf), F32),
                   jax.ShapeDtypeStruct((n_f, 2, tf), F32)],
        scratch_shapes=[pltpu.VMEM((rows, D_MODEL), F32),
                        pltpu.VMEM((rows, D_MODEL), BF16)],
        compiler_params=_params(1),
        name="ffn_small",
    )(x, mix, wo, g2, wu, wu, wc, wc, wd, g, s, s)


def _tail_to_halo(tail):
    return jnp.pad(tail, ((HALO - 2, 0), (0, 0)))


def _join_tail_blocks(t):
    b, n_f, _, tf = t.shape
    return t.transpose(0, 2, 1, 3).reshape(b, 2, n_f * tf)


def kernel(x_prompt, x_sample, state_mlstm_C, state_mlstm_n, state_mlstm_m, state_shortconv,
           state_ffnconv, meta_tokens, norm1_g, w_in, b_if, w_shortconv, w_out, norm2_g,
           w_up, w_ffconv, w_down, norm_f_g):
    assert w_in.shape[0] == 1, "single-layer trunk"
    n_batch, seq, _ = x_prompt.shape
    n_dec = x_sample.shape[0]

    wt = w_in[0].T
    wg = _gate_cols(wt)
    bif = jnp.pad(b_if[0], (0, LANES - 2 * N_HEADS)).reshape(1, LANES)
    wsc = w_shortconv[0]
    wfc = w_ffconv[0]
    g1 = norm1_g[0].reshape(1, D_MODEL)
    g2 = norm2_g[0].reshape(1, D_MODEL)
    gf = norm_f_g.reshape(1, D_MODEL)

    xm = jnp.pad(meta_tokens, ((0, CHUNK - N_META), (0, 0)))
    hn_m, gates_m = _norm_gates_seq(xm, g1, wg, bif, tm=CHUNK, n_valid=N_META)
    wp, mix_m, c_m, n_m, m_m, sc_m = _prep_win(wt, hn_m, gates_m, wsc, n_valid=N_META)

    xp = x_prompt.reshape(n_batch * seq, D_MODEL)
    hn_p, gates_p = _norm_gates_seq(xp, g1, wg, bif, tm=NORM_TM, n_valid=CHUNK)
    xs = x_sample.reshape(n_dec, D_MODEL)
    hn_s, gif_s = _norm_gates_step(xs, g1, wg, bif)
    mprev = jnp.pad(state_mlstm_m[0], ((0, 0), (0, LANES - N_HEADS)))
    n_hb = state_mlstm_n[0].transpose(1, 0, 2)
    mix_p, c_p, n_p, m_p, sc_p, mix_s, c_s, n_s, m_s, sc_s, wu, wd, wo = _mixers(
        hn_p, wp, gates_p, wsc, c_m, n_m, m_m, _tail_to_halo(sc_m),
        (hn_s, gif_s, mprev, state_shortconv, state_mlstm_C[0], n_hb, w_up[0], w_down[0], w_out[0]),
        n_batch=n_batch, tm=MIXER_TM, n_valid=MIXER_TM)

    x_sm = jnp.concatenate([xs, meta_tokens], axis=0)
    mix_sm = jnp.concatenate([mix_s, mix_m[:N_META].astype(F32)], axis=0)
    y_s, fca_s, fcb_s, fca_m, fcb_m = _ffn_small(
        x_sm, mix_sm, wo, g2, wu, wfc, wd, gf, state_ffnconv, tf=FFN_TF, n_step=n_dec)
    y_p, fca_p, fcb_p = _ffn_seq(xp, mix_p, wo, g2, wu, wfc, wd, gf,
                                 _tail_to_halo(_join_tail_blocks(fca_m[None])[0]),
                                 _tail_to_halo(_join_tail_blocks(fcb_m[None])[0]),
                                 n_batch=n_batch, tm=FFN_TM, tf=FFN_TF, n_valid=FFN_TM)

    return (y_p.reshape(n_batch, seq, D_MODEL),
            y_s.reshape(n_dec, 1, D_MODEL),
            c_p[None],
            n_p.reshape(1, n_batch, N_HEADS, DK),
            m_p[:, :, 0, 0][None],
            sc_p[None],
            jnp.concatenate([_join_tail_blocks(fca_p), _join_tail_blocks(fcb_p)], axis=-1)[None],
            c_s[None],
            n_s.transpose(1, 0, 2)[None],
            m_s[:, :, 0].T[None],
            sc_s,
            jnp.concatenate([fca_s, fcb_s], axis=-1)[None])
```

```python
import functools

import jax
import jax.numpy as jnp
from jax import lax
from jax.experimental import pallas as pl
from jax.experimental.pallas import tpu as pltpu

D_MODEL = 2048
N_META = 16
N_HEADS = 8
DK = 128
DV = 256
D_CONV = D_MODEL
CONV_W = 3
D_FF = 5632
CHUNK = 128
EPS = 1e-6
K_SCALE = DK ** -0.5
NEG = -1e30
HEAD_COLS = 7 * DV + 2 * DK
LANES = 128
V7X_VMEM_BYTES = 64 * 1024 * 1024
V7X_VMEM_LIMIT = V7X_VMEM_BYTES * 7 // 8
HALO = 8
NORM_TM = 1024
MIXER_TM = 1024
FFN_TM = 512
FFN_TF = 512
FFN_SUB = 64
DOWN_GROUPS = 2
F32 = jnp.float32
BF16 = jnp.bfloat16

_Q, _K, _V, _O, _GB, _U, _CG, _GA, _BG = (0, 128, 256, 512, 768, 1024, 1280, 1536, 1792)


def _params(n_grid):
    return pltpu.CompilerParams(dimension_semantics=("arbitrary",) * n_grid,
                                vmem_limit_bytes=V7X_VMEM_LIMIT)


def _log_sigmoid(x):
    return jnp.minimum(x, 0.0) - jnp.log1p(jnp.exp(-jnp.abs(x)))


def _rms(x, g):
    return x * lax.rsqrt(jnp.mean(x * x, axis=-1, keepdims=True) + EPS) * g


_O_Q = 3 * D_CONV
_O_K = _O_Q + N_HEADS * DK
_O_V = _O_K + N_HEADS * DK
_O_O = _O_V + N_HEADS * DV
_O_GIF = _O_O + N_HEADS * DV
_O_GA = _O_GIF + 2 * N_HEADS
_O_GB = _O_GA + D_CONV


def _prep_win_kernel(u_ref, cg_ref, bg_ref, ga_ref, v_ref, o_ref, gb_ref, q_ref, k_ref,
                     hn_ref, gates_ref, wsc_ref,
                     wp_ref, mix_ref, cout_ref, nout_ref, mout_ref, scout_ref,
                     z_ref, cu_ref, *, n_valid):
    for ref, off, d in ((u_ref, _U, DV), (cg_ref, _CG, DV), (bg_ref, _BG, DV), (ga_ref, _GA, DV),
                        (v_ref, _V, DV), (o_ref, _O, DV), (gb_ref, _GB, DV), (q_ref, _Q, DK), (k_ref, _K, DK)):
        wp_ref[:, off:off + d] = ref[...].T.astype(BF16)
    z_ref[...] = jnp.dot(hn_ref[...], wp_ref[...], preferred_element_type=F32)

    row = lax.broadcasted_iota(jnp.int32, (CHUNK, CHUNK), 0)
    col = lax.broadcasted_iota(jnp.int32, (CHUNK, CHUNK), 1)
    causal = col <= row
    eye = col == row
    a_row = gates_ref[0:1, :]
    b_row = gates_ref[1:2, :]
    a_col = jnp.sum(jnp.where(eye, a_row, 0.0), axis=-1, keepdims=True)
    b_col = jnp.sum(jnp.where(eye, b_row, 0.0), axis=-1, keepdims=True)
    dmat = jnp.where(causal, b_col + a_row, NEG)
    m_t = jnp.maximum(b_col, jnp.max(dmat, axis=-1, keepdims=True))
    q = z_ref[:, _Q:_Q + DK]
    k = z_ref[:, _K:_K + DK] * K_SCALE
    v = z_ref[:, _V:_V + DV]
    qk = lax.dot_general(q.astype(BF16), k.astype(BF16), (((1,), (1,)), ((), ())), preferred_element_type=F32)
    s = qk * jnp.exp(dmat - m_t)
    num = jnp.dot(s.astype(BF16), v.astype(BF16), preferred_element_type=F32)
    den = jnp.sum(s, axis=-1, keepdims=True)
    bmix = num / jnp.maximum(jnp.abs(den), jnp.exp(-m_t))
    m_last = m_t[CHUNK - 1:CHUNK, :]
    g_col = jnp.exp(b_col[CHUNK - 1:CHUNK, :] + a_col - m_last)
    cout_ref[...] = jnp.dot(k.T.astype(BF16), (g_col * v).astype(BF16), preferred_element_type=F32)
    nout_ref[...] = jnp.sum(g_col * k, axis=0, keepdims=True)
    mout_ref[...] = jnp.broadcast_to(m_last, (1, DK))

    cu = z_ref[:, _CG:_CG + DV] * z_ref[:, _U:_U + DV]
    cu_ref[0:HALO, :] = jnp.zeros((HALO, DV), F32)
    cu_ref[HALO:HALO + CHUNK, :] = cu
    taps = (cu * wsc_ref[2:3, :]
            + cu_ref[HALO - 1:HALO - 1 + CHUNK, :] * wsc_ref[1:2, :]
            + cu_ref[HALO - 2:HALO - 2 + CHUNK, :] * wsc_ref[0:1, :])
    amix = jax.nn.sigmoid(z_ref[:, _GA:_GA + DV]) * taps
    og = jax.nn.sigmoid(z_ref[:, _GB:_GB + DV]) * jax.nn.sigmoid(z_ref[:, _O:_O + DV])
    mix_ref[...] = (amix * z_ref[:, _BG:_BG + DV] + og * bmix).astype(BF16)
    scout_ref[...] = cu_ref[HALO + n_valid - 2:HALO + n_valid, :]


def _gate_cols_kernel(gif_ref, wg_ref):
    gif = gif_ref[...].T
    lane = lax.broadcasted_iota(jnp.int32, gif.shape, 1)
    wg_ref[...] = jnp.where(lane < 2 * N_HEADS, gif, 0.0).astype(BF16)


def _gate_cols(wt):
    return pl.pallas_call(
        _gate_cols_kernel,
        grid=(1,),
        in_specs=[pl.BlockSpec((pl.Element(LANES), pl.Element(D_MODEL)), lambda i: (_O_GIF, 0))],
        out_specs=pl.BlockSpec((D_MODEL, LANES), lambda i: (0, 0)),
        out_shape=jax.ShapeDtypeStruct((D_MODEL, LANES), BF16),
        compiler_params=_params(1),
        name="gate_cols",
    )(wt)


def _prep_win(wt, hn, gates, wsc, *, n_valid):
    assert _O_GA % 8 == 0
    rows = lambda off, d: pl.BlockSpec((pl.Element(d), pl.Element(D_MODEL)),
                                       lambda h: (pl.multiple_of(off + d * h, 8), 0))
    return pl.pallas_call(
        functools.partial(_prep_win_kernel, n_valid=n_valid),
        grid=(N_HEADS,),
        in_specs=[rows(0, DV), rows(D_CONV, DV), rows(2 * D_CONV, DV), rows(_O_GA, DV), rows(_O_V, DV),
                  rows(_O_O, DV), rows(_O_GB, DV), rows(_O_Q, DK), rows(_O_K, DK),
                  pl.BlockSpec((CHUNK, D_MODEL), lambda h: (0, 0)),
                  pl.BlockSpec((None, 2, CHUNK), lambda h: (h, 0, 0)),
                  pl.BlockSpec((CONV_W, DV), lambda h: (0, h))],
        out_specs=[pl.BlockSpec((D_MODEL, HEAD_COLS), lambda h: (0, h)),
                   pl.BlockSpec((CHUNK, DV), lambda h: (0, h)),
                   pl.BlockSpec((None, DK, DV), lambda h: (h, 0, 0)),
                   pl.BlockSpec((None, 1, DK), lambda h: (h, 0, 0)),
                   pl.BlockSpec((None, 1, DK), lambda h: (h, 0, 0)),
                   pl.BlockSpec((2, DV), lambda h: (0, h))],
        out_shape=[jax.ShapeDtypeStruct((D_MODEL, N_HEADS * HEAD_COLS), BF16),
                   jax.ShapeDtypeStruct((CHUNK, D_MODEL), BF16),
                   jax.ShapeDtypeStruct((N_HEADS, DK, DV), F32),
                   jax.ShapeDtypeStruct((N_HEADS, 1, DK), F32),
                   jax.ShapeDtypeStruct((N_HEADS, 1, DK), F32),
                   jax.ShapeDtypeStruct((2, D_CONV), F32)],
        scratch_shapes=[pltpu.VMEM((CHUNK, HEAD_COLS), F32),
                        pltpu.VMEM((HALO + CHUNK, DV), F32)],
        compiler_params=_params(1),
        name="prep_win",
    )(*([wt] * 9), hn, gates, wsc)


def _norm_gates_seq_kernel(x_ref, g_ref, wg_ref, bif_ref, hn_ref, gates_ref, *, tm, n_valid):
    hn = _rms(x_ref[...], g_ref[...]).astype(BF16)
    hn_ref[...] = hn
    gif = jnp.dot(hn, wg_ref[...], preferred_element_type=F32) + bif_ref[...]
    row = lax.broadcasted_iota(jnp.int32, (CHUNK, CHUNK), 0)
    col = lax.broadcasted_iota(jnp.int32, (CHUNK, CHUNK), 1)
    upper = (row <= col).astype(F32)
    for c in range(tm // CHUNK):
        gt = gif[c * CHUNK:(c + 1) * CHUNK, :].T
        ig = gt[0:N_HEADS, :]
        lf = _log_sigmoid(gt[N_HEADS:2 * N_HEADS, :])
        if n_valid < CHUNK:
            tok = lax.broadcasted_iota(jnp.int32, (N_HEADS, CHUNK), 1)
            ig = jnp.where(tok < n_valid, ig, NEG)
            lf = jnp.where(tok < n_valid, lf, 0.0)
        b = jnp.dot(lf, upper, preferred_element_type=F32, precision=lax.Precision.HIGHEST)
        a = ig - b
        for j in range(N_HEADS):
            gates_ref[j, 0:1, c * CHUNK:(c + 1) * CHUNK] = a[j:j + 1, :]
            gates_ref[j, 1:2, c * CHUNK:(c + 1) * CHUNK] = b[j:j + 1, :]


def _norm_gates_seq(x, g, wg, bif, *, tm, n_valid):
    t = x.shape[0]
    return pl.pallas_call(
        functools.partial(_norm_gates_seq_kernel, tm=tm, n_valid=n_valid),
        grid=(t // tm,),
        in_specs=[pl.BlockSpec((tm, D_MODEL), lambda r: (r, 0)),
                  pl.BlockSpec((1, D_MODEL), lambda r: (0, 0)),
                  pl.BlockSpec((D_MODEL, LANES), lambda r: (0, 0)),
                  pl.BlockSpec((1, LANES), lambda r: (0, 0))],
        out_specs=[pl.BlockSpec((tm, D_MODEL), lambda r: (r, 0)),
                   pl.BlockSpec((N_HEADS, 2, tm), lambda r: (0, 0, r))],
        out_shape=[jax.ShapeDtypeStruct((t, D_MODEL), BF16),
                   jax.ShapeDtypeStruct((N_HEADS, 2, t), F32)],
        compiler_params=_params(1),
        name="norm_gates_seq",
    )(x, g, wg, bif)


def _norm_gates_step_kernel(x_ref, g_ref, wg_ref, bif_ref, hn_ref, gif_ref):
    hn = _rms(x_ref[...], g_ref[...]).astype(BF16)
    hn_ref[...] = hn
    gif_ref[...] = jnp.dot(hn, wg_ref[...], preferred_element_type=F32) + bif_ref[...]


def _norm_gates_step(x, g, wg, bif):
    t = x.shape[0]
    return pl.pallas_call(
        _norm_gates_step_kernel,
        grid=(1,),
        in_specs=[pl.BlockSpec((t, D_MODEL), lambda r: (0, 0)),
                  pl.BlockSpec((1, D_MODEL), lambda r: (0, 0)),
                  pl.BlockSpec((D_MODEL, LANES), lambda r: (0, 0)),
                  pl.BlockSpec((1, LANES), lambda r: (0, 0))],
        out_specs=[pl.BlockSpec((t, D_MODEL), lambda r: (0, 0)),
                   pl.BlockSpec((t, LANES), lambda r: (0, 0))],
        out_shape=[jax.ShapeDtypeStruct((t, D_MODEL), BF16),
                   jax.ShapeDtypeStruct((t, LANES), F32)],
        compiler_params=_params(1),
        name="norm_gates_step",
    )(x, g, wg, bif)


def _mixers_kernel(*refs, tm, tiles_per_batch, n_valid, step_bb):
    n_seq = (8, 5, 12)
    n_step = (9, 8, 5) if step_bb else (0, 0, 0)
    cuts = [0]
    for a, b in zip(n_seq, n_step):
        cuts += [cuts[-1] + a, cuts[-1] + a + b]
    seq_in, step_in, seq_out, step_out, seq_scr, step_scr = (refs[i:j] for i, j in zip(cuts[:-1], cuts[1:]))
    hn_ref, w_ref, gates_ref, wsc_ref, c0_ref, n0_ref, m0_ref, sc0_ref = seq_in
    mix_ref, cout_ref, nout_ref, mout_ref, scout_ref = seq_out
    z_ref, cu_ref, c_ref, n_ref, m_ref, dmat_ref, qk_ref, sv_ref, kv_ref, bmix_ref, amix_ref, og_ref = seq_scr
    head = pl.program_id(0)
    tile = pl.program_id(1)
    rb = tile % tiles_per_batch
    nc = tm // CHUNK

    @pl.when(rb == 0)
    def _():
        c_ref[...] = c0_ref[...]
        n_ref[...] = n0_ref[...]
        m_ref[...] = m0_ref[...]
        cu_ref[0:HALO, :] = sc0_ref[...]

    row = lax.broadcasted_iota(jnp.int32, (CHUNK, CHUNK), 0)
    col = lax.broadcasted_iota(jnp.int32, (CHUNK, CHUNK), 1)
    causal = col <= row
    eye = col == row
    st = [dict() for _ in range(nc)]
    rows_of = lambda c: slice(c * CHUNK, (c + 1) * CHUNK)
    k_of = lambda rows: z_ref[rows, _K:_K + DK] * K_SCALE

    def proj(j):
        cols = slice(j * DV, (j + 1) * DV)
        z_ref[:, cols] = jnp.dot(hn_ref[...], w_ref[:, cols], preferred_element_type=F32)

    def gate_terms(c):
        rows, d = rows_of(c), st[c]
        a_row = gates_ref[0:1, rows]
        b_row = gates_ref[1:2, rows]
        d["a_col"] = jnp.sum(jnp.where(eye, a_row, 0.0), axis=-1, keepdims=True)
        d["b_col"] = jnp.sum(jnp.where(eye, b_row, 0.0), axis=-1, keepdims=True)
        dmat = jnp.where(causal, d["b_col"] + a_row, NEG)
        dmat_ref[rows, :] = dmat
        d["mloc"] = jnp.max(dmat, axis=-1, keepdims=True)
        qk_ref[rows, :] = lax.dot_general(z_ref[rows, _Q:_Q + DK].astype(BF16), k_of(rows).astype(BF16),
                                          (((1,), (1,)), ((), ())), preferred_element_type=F32)

    def stabilisers():
        m_prev = m_ref[:, 0:1]
        for c in range(nc):
            d = st[c]
            d["m_prev"] = m_prev
            d["bm"] = d["b_col"] + m_prev
            d["m_t"] = jnp.maximum(d["bm"], d["mloc"])
            d["m_last"] = d["m_t"][CHUNK - 1:CHUNK, :]
            m_prev = d["m_last"]
        m_ref[...] = jnp.broadcast_to(m_prev, (1, DK))

    def intra(c):
        rows, d = rows_of(c), st[c]
        k = k_of(rows)
        v = z_ref[rows, _V:_V + DV]
        s = qk_ref[rows, :] * jnp.exp(dmat_ref[rows, :] - d["m_t"])
        sv_ref[rows, :] = jnp.dot(s.astype(BF16), v.astype(BF16), preferred_element_type=F32)
        d["ssum"] = jnp.sum(s, axis=-1, keepdims=True)
        d["decay"] = jnp.exp(d["bm"] - d["m_t"])
        d["hden"] = jnp.exp(-d["m_t"])
        b_last = d["b_col"][CHUNK - 1:CHUNK, :]
        g_col = jnp.exp(b_last + d["a_col"] - d["m_last"])
        d["carry"] = jnp.exp(b_last + d["m_prev"] - d["m_last"])
        kv_ref[c] = jnp.dot(k.T.astype(BF16), (g_col * v).astype(BF16), preferred_element_type=F32)
        d["kn"] = jnp.sum(g_col * k, axis=0, keepdims=True)

    def inter(c):
        rows, d = rows_of(c), st[c]
        q = z_ref[rows, _Q:_Q + DK]
        c_prev = c_ref[...]
        n_prev = n_ref[...]
        num = (d["decay"] * jnp.dot(q.astype(BF16), c_prev.astype(BF16), preferred_element_type=F32)
               + sv_ref[rows, :])
        den = d["decay"] * jnp.sum(q * n_prev, axis=-1, keepdims=True) + d["ssum"]
        bmix_ref[rows, :] = num / jnp.maximum(jnp.abs(den), d["hden"])
        c_ref[...] = d["carry"] * c_prev + kv_ref[c]
        n_ref[...] = d["carry"] * n_prev + d["kn"]

    def out_gate(c):
        rows = rows_of(c)
        og_ref[rows, :] = jax.nn.sigmoid(z_ref[rows, _GB:_GB + DV]) * jax.nn.sigmoid(z_ref[rows, _O:_O + DV])

    def conv(c):
        lo = c * CHUNK
        rows = rows_of(c)
        cu = z_ref[rows, _CG:_CG + DV] * z_ref[rows, _U:_U + DV]
        cu_ref[HALO + lo:HALO + lo + CHUNK, :] = cu
        amix_ref[rows, :] = (cu * wsc_ref[2:3, :]
                             + cu_ref[HALO - 1 + lo:HALO - 1 + lo + CHUNK, :] * wsc_ref[1:2, :]
                             + cu_ref[HALO - 2 + lo:HALO - 2 + lo + CHUNK, :] * wsc_ref[0:1, :])

    def gate(c):
        rows = rows_of(c)
        amix_ref[rows, :] = jax.nn.sigmoid(z_ref[rows, _GA:_GA + DV]) * amix_ref[rows, :]

    def merge(c):
        rows = rows_of(c)
        mix = amix_ref[rows, :] * z_ref[rows, _BG:_BG + DV] + og_ref[rows, :] * bmix_ref[rows, :]
        mix_ref[rows, :] = mix.astype(BF16)

    piece = lambda off: off // DV
    assert (piece(_Q), piece(_V), piece(_O), piece(_GB)) == (0, 1, 2, 3)
    assert sorted(map(piece, (_U, _CG))) == [4, 5] and (piece(_GA), piece(_BG)) == (6, 7)
    each = lambda fn: [functools.partial(fn, c) for c in range(nc)]
    half = (nc + 1) // 2
    after = {0: [],
             1: each(gate_terms),
             2: [stabilisers] + each(intra)[:half] + each(inter)[:1],
             3: each(intra)[half:] + each(inter)[1:3] + each(out_gate)[:half],
             4: each(inter)[3:6] + each(out_gate)[half:],
             5: each(inter)[6:] + each(conv),
             6: each(gate),
             7: each(merge)}

    if step_bb:
        bb = step_bb
        hns_ref, gifs_ref, mprevs_ref, sc_ref, cs_ref, ns_ref, wu32_ref, wd32_ref, wo32_ref = step_in
        mixs_ref, csout_ref, nsout_ref, msout_ref, scnew_ref, wu16_ref, wd16_ref, wo16_ref = step_out
        zs_ref, aouts_ref, qts_ref, kts_ref, hbs_ref = step_scr

        def round_weights():
            wu16_ref[...] = wu32_ref[...].astype(BF16)
            wd16_ref[...] = wd32_ref[...].astype(BF16)
            wo16_ref[...] = wo32_ref[...].astype(BF16)

        @pl.when(tile == 0)
        def _():
            z = jnp.dot(hns_ref[...], w_ref[...], preferred_element_type=F32)
            zs_ref[...] = z
            cu = z[:, _CG:_CG + DV] * z[:, _U:_U + DV]
            s0 = sc_ref[:, 0, :]
            s1 = sc_ref[:, 1, :]
            taps = s0 * wsc_ref[0:1, :] + s1 * wsc_ref[1:2, :] + cu * wsc_ref[2:3, :]
            aouts_ref[...] = z[:, _BG:_BG + DV] * taps
            scnew_ref[:, 0, :] = s1
            scnew_ref[:, 1, :] = cu
            qt = z[:, _Q:_Q + DK].T
            kt = (z[:, _K:_K + DK] * K_SCALE).T
            for jj in range(hns_ref.shape[0] // bb):
                qts_ref[jj] = qt[:, jj * bb:(jj + 1) * bb]
                kts_ref[jj] = kt[:, jj * bb:(jj + 1) * bb]

        srows = pl.ds(pl.multiple_of(tile * bb, bb), bb)
        sd = {}

        def step_gates():
            lane = lax.broadcasted_iota(jnp.int32, (bb, LANES), 1)
            pick = lambda x, idx: jnp.sum(jnp.where(lane == idx, x, 0.0), axis=-1, keepdims=True)
            gif = gifs_ref[srows, :]
            ig = pick(gif, head)
            lf = _log_sigmoid(pick(gif, head + N_HEADS))
            m_prev = pick(mprevs_ref[srows, :], head)
            q = zs_ref[srows, _Q:_Q + DK]
            k = zs_ref[srows, _K:_K + DK] * K_SCALE
            n_prev = ns_ref[...]
            m_t = jnp.maximum(lf + m_prev, ig)
            sd["w"] = jnp.exp(ig - m_t)
            sd["decay"] = jnp.exp(lf + m_prev - m_t)
            sd["s"] = jnp.sum(q * k, axis=-1, keepdims=True) * sd["w"]
            den = sd["decay"] * jnp.sum(q * n_prev, axis=-1, keepdims=True) + sd["s"]
            sd["inv"] = 1.0 / jnp.maximum(jnp.abs(den), jnp.exp(-m_t))
            nsout_ref[...] = sd["decay"] * n_prev + sd["w"] * k
            msout_ref[...] = jnp.broadcast_to(m_t, (bb, DK))

        def step_rows(lo, hi):
            qt = qts_ref[tile]
            kt = kts_ref[tile]
            v = zs_ref[srows, _V:_V + DV]
            for i in range(lo, hi):
                c_prev = cs_ref[i]
                d_i = sd["decay"][i:i + 1, :]
                v_i = v[i:i + 1, :]
                qc = jnp.sum(qt[:, i:i + 1] * c_prev, axis=0, keepdims=True)
                hbs_ref[i:i + 1, :] = (d_i * qc + sd["s"][i:i + 1, :] * v_i) * sd["inv"][i:i + 1, :]
                csout_ref[i] = d_i * c_prev + (sd["w"][i:i + 1, :] * kt[:, i:i + 1]) * v_i

        def step_merge():
            b_out = jax.nn.sigmoid(zs_ref[srows, _O:_O + DV]) * hbs_ref[...]
            mixs_ref[...] = (jax.nn.sigmoid(zs_ref[srows, _GA:_GA + DV]) * aouts_ref[srows, :]
                             + jax.nn.sigmoid(zs_ref[srows, _GB:_GB + DV]) * b_out)

        after[0] += [round_weights, step_gates, functools.partial(step_rows, 0, bb // 2)]
        after[5] += [functools.partial(step_rows, bb // 2, bb)]
        after[6] = [step_merge] + after[6]

    for j in range(HEAD_COLS // DV):
        proj(j)
        for fn in after.get(j, []):
            fn()

    @pl.when(rb == tiles_per_batch - 1)
    def _():
        cout_ref[...] = c_ref[...]
        nout_ref[...] = n_ref[...]
        mout_ref[...] = m_ref[...]
        scout_ref[...] = cu_ref[HALO + n_valid - 2:HALO + n_valid, :]

    cu_ref[0:HALO, :] = cu_ref[tm:tm + HALO, :]


def _mixers(hn, wp, gates, wsc, c0, n0, m0, sc0, step=None, *, n_batch, tm, n_valid):
    t = hn.shape[0]
    n_tiles = t // tm
    tpb = n_tiles // n_batch
    in_specs = [pl.BlockSpec((tm, D_MODEL), lambda h, r: (r, 0)),
                pl.BlockSpec((D_MODEL, HEAD_COLS), lambda h, r: (0, h)),
                pl.BlockSpec((None, 2, tm), lambda h, r: (h, 0, r)),
                pl.BlockSpec((CONV_W, DV), lambda h, r: (0, h)),
                pl.BlockSpec((None, DK, DV), lambda h, r: (h, 0, 0)),
                pl.BlockSpec((None, 1, DK), lambda h, r: (h, 0, 0)),
                pl.BlockSpec((None, 1, DK), lambda h, r: (h, 0, 0)),
                pl.BlockSpec((HALO, DV), lambda h, r: (0, h))]
    out_specs = [pl.BlockSpec((tm, DV), lambda h, r: (r, h)),
                 pl.BlockSpec((None, None, DK, DV), lambda h, r: (r // tpb, h, 0, 0)),
                 pl.BlockSpec((None, None, 1, DK), lambda h, r: (r // tpb, h, 0, 0)),
                 pl.BlockSpec((None, None, 1, DK), lambda h, r: (r // tpb, h, 0, 0)),
                 pl.BlockSpec((None, 2, DV), lambda h, r: (r // tpb, 0, h))]
    out_shape = [jax.ShapeDtypeStruct((t, D_MODEL), BF16),
                 jax.ShapeDtypeStruct((n_batch, N_HEADS, DK, DV), F32),
                 jax.ShapeDtypeStruct((n_batch, N_HEADS, 1, DK), F32),
                 jax.ShapeDtypeStruct((n_batch, N_HEADS, 1, DK), F32),
                 jax.ShapeDtypeStruct((n_batch, 2, D_CONV), F32)]
    scratch = [pltpu.VMEM((tm, HEAD_COLS), F32),
               pltpu.VMEM((tm + HALO, DV), F32),
               pltpu.VMEM((DK, DV), F32),
               pltpu.VMEM((1, DK), F32),
               pltpu.VMEM((1, DK), F32),
               pltpu.VMEM((tm, CHUNK), F32),
               pltpu.VMEM((tm, CHUNK), F32),
               pltpu.VMEM((tm, DV), F32),
               pltpu.VMEM((tm // CHUNK, DK, DV), F32),
               pltpu.VMEM((tm, DV), F32),
               pltpu.VMEM((tm, DV), F32),
               pltpu.VMEM((tm, DV), F32)]
    step_bb = 0
    if step is not None:
        nb = step[0].shape[0]
        assert nb % n_tiles == 0
        step_bb = nb // n_tiles
        in_specs += [pl.BlockSpec((nb, D_MODEL), lambda h, r: (0, 0)),
                     pl.BlockSpec((nb, LANES), lambda h, r: (0, 0)),
                     pl.BlockSpec((nb, LANES), lambda h, r: (0, 0)),
                     pl.BlockSpec((None, nb, CONV_W - 1, DV), lambda h, r: (0, 0, 0, h)),
                     pl.BlockSpec((step_bb, None, DK, DV), lambda h, r: (r, h, 0, 0)),
                     pl.BlockSpec((None, step_bb, DK), lambda h, r: (h, r, 0))]
        n_steps = N_HEADS * n_tiles
        wu32, wd32, wo32 = step[6], step[7], step[8]
        assert wu32.shape[0] % (16 * n_steps) == 0 and wd32.shape[0] % (8 * n_steps) == 0
        assert wo32.shape[0] % (16 * n_steps) == 0
        wu_rows = wu32.shape[0] // n_steps
        wd_rows, wd_cols = 2 * wd32.shape[0] // n_steps, wd32.shape[1] // 2
        assert wd_rows % 16 == 0
        wu_spec = pl.BlockSpec((wu_rows, wu32.shape[1]), lambda h, r: (h * n_tiles + r, 0))
        wd_spec = pl.BlockSpec((wd_rows, wd_cols), lambda h, r: ((h * n_tiles + r) // 2, (h * n_tiles + r) % 2))
        wo_spec = pl.BlockSpec((wo32.shape[0] // n_steps, wo32.shape[1]), lambda h, r: (h * n_tiles + r, 0))
        in_specs += [wu_spec, wd_spec, wo_spec]
        out_specs += [pl.BlockSpec((step_bb, DV), lambda h, r: (r, h)),
                      pl.BlockSpec((step_bb, None, DK, DV), lambda h, r: (r, h, 0, 0)),
                      pl.BlockSpec((None, step_bb, DK), lambda h, r: (h, r, 0)),
                      pl.BlockSpec((None, step_bb, DK), lambda h, r: (h, r, 0)),
                      pl.BlockSpec((None, nb, CONV_W - 1, DV), lambda h, r: (0, 0, 0, h)),
                      wu_spec, wd_spec, wo_spec]
        out_shape += [jax.ShapeDtypeStruct((nb, D_MODEL), F32),
                      jax.ShapeDtypeStruct((nb, N_HEADS, DK, DV), F32),
                      jax.ShapeDtypeStruct((N_HEADS, nb, DK), F32),
                      jax.ShapeDtypeStruct((N_HEADS, nb, DK), F32),
                      jax.ShapeDtypeStruct((1, nb, CONV_W - 1, D_CONV), F32),
                      jax.ShapeDtypeStruct(wu32.shape, BF16),
                      jax.ShapeDtypeStruct(wd32.shape, BF16),
                      jax.ShapeDtypeStruct(wo32.shape, BF16)]
        scratch += [pltpu.VMEM((nb, HEAD_COLS), F32),
                    pltpu.VMEM((nb, DV), F32),
                    pltpu.VMEM((n_tiles, DK, step_bb), F32),
                    pltpu.VMEM((n_tiles, DK, step_bb), F32),
                    pltpu.VMEM((step_bb, DV), F32)]
    return pl.pallas_call(
        functools.partial(_mixers_kernel, tm=tm, tiles_per_batch=tpb, n_valid=n_valid, step_bb=step_bb),
        grid=(N_HEADS, n_tiles),
        in_specs=in_specs,
        out_specs=out_specs,
        out_shape=out_shape,
        scratch_shapes=scratch,
        compiler_params=_params(2),
        name="mixers",
    )(hn, wp, gates, wsc, c0, n0, m0, sc0, *(step or ()))


def _mixer_residual(x_ref, mix_ref, wo_ref, g2_ref, acc_ref, h2_ref):
    mix = mix_ref[...]
    if mix.dtype != BF16:
        mix = mix.astype(BF16)
    acc_ref[...] = x_ref[...] + jnp.dot(mix, wo_ref[...], preferred_element_type=F32)
    rows = acc_ref.shape[0]
    for lo in range(0, rows, FFN_SUB):
        hi = min(lo + FFN_SUB, rows)
        h2_ref[lo:hi, :] = _rms(acc_ref[lo:hi, :], g2_ref[...]).astype(BF16)


def _final_norm(acc_ref, g_ref, y_ref):
    rows = y_ref.shape[0]
    for lo in range(0, rows, FFN_SUB):
        hi = min(lo + FFN_SUB, rows)
        y_ref[lo:hi, :] = _rms(acc_ref[lo:hi, :], g_ref[...])


def _ffn_seq_kernel(x_ref, mix_ref, wo_ref, g2_ref, wua_ref, wub_ref, wca_ref, wcb_ref, wd_ref, g_ref,
                    ha0_ref, hb0_ref,
                    y_ref, fca_ref, fcb_ref,
                    acc_ref, h2_ref, ua_ref, ub_ref, ha_ref, hb_ref, act_ref,
                    *, tm, tiles_per_batch, n_valid, n_f):
    r = pl.program_id(0)
    f = pl.program_id(1)
    rb = r % tiles_per_batch

    @pl.when((r == 0) & (f == 0))
    def _():
        ha_ref[...] = jnp.zeros_like(ha_ref)
        hb_ref[...] = jnp.zeros_like(hb_ref)

    @pl.when(f == 0)
    def _():
        _mixer_residual(x_ref, mix_ref, wo_ref, g2_ref, acc_ref, h2_ref)

    ua_ref[0:HALO, :] = jnp.where(rb == 0, ha0_ref[...], ha_ref[f])
    ub_ref[0:HALO, :] = jnp.where(rb == 0, hb0_ref[...], hb_ref[f])
    group = tm // DOWN_GROUPS

    def up_proj(gi):
        rows = slice(gi * group, (gi + 1) * group)
        urows = slice(HALO + gi * group, HALO + (gi + 1) * group)
        ua_ref[urows, :] = jnp.dot(h2_ref[rows, :], wua_ref[...], preferred_element_type=F32)
        ub_ref[urows, :] = jnp.dot(h2_ref[rows, :], wub_ref[...], preferred_element_type=F32)

    def conv(u_ref, wc_ref, lo):
        win = u_ref[lo:lo + HALO + FFN_SUB, :]
        back1 = pltpu.roll(win, 1, axis=0)[HALO:, :]
        back2 = pltpu.roll(win, 2, axis=0)[HALO:, :]
        return win[HALO:, :] * wc_ref[2:3, :] + back1 * wc_ref[1:2, :] + back2 * wc_ref[0:1, :]

    def glu_down(gi):
        for lo in range(gi * group, (gi + 1) * group, FFN_SUB):
            up_a = conv(ua_ref, wca_ref, lo)
            up_b = conv(ub_ref, wcb_ref, lo)
            act_ref[lo:lo + FFN_SUB, :] = (up_b * jax.nn.sigmoid(up_b) * up_a).astype(BF16)
        rows = slice(gi * group, (gi + 1) * group)
        acc_ref[rows, :] += jnp.dot(act_ref[rows, :], wd_ref[...], preferred_element_type=F32)

    up_proj(0)
    for gi in range(1, DOWN_GROUPS):
        up_proj(gi)
        glu_down(gi - 1)
    glu_down(DOWN_GROUPS - 1)
    for u_ref, halo_ref, tail_ref in ((ua_ref, ha_ref, fca_ref), (ub_ref, hb_ref, fcb_ref)):
        halo_ref[f] = u_ref[tm:tm + HALO, :]
        tail_ref[f] = u_ref[HALO + n_valid - 2:HALO + n_valid, :]

    @pl.when(f == n_f - 1)
    def _():
        _final_norm(acc_ref, g_ref, y_ref)


def _ffn_seq(x, mix, wo, g2, wu, wc, wd, g, halo_a, halo_b, *, n_batch, tm, tf, n_valid):
    t = x.shape[0]
    tpb = t // n_batch // tm
    n_f = D_FF // tf
    return pl.pallas_call(
        functools.partial(_ffn_seq_kernel, tm=tm, tiles_per_batch=tpb, n_valid=n_valid, n_f=n_f),
        grid=(t // tm, n_f),
        in_specs=[pl.BlockSpec((tm, D_MODEL), lambda r, f: (r, 0)),
                  pl.BlockSpec((tm, D_MODEL), lambda r, f: (r, 0)),
                  pl.BlockSpec((D_MODEL, D_MODEL), lambda r, f: (0, 0), pipeline_mode=pl.Buffered(1)),
                  pl.BlockSpec((1, D_MODEL), lambda r, f: (0, 0)),
                  pl.BlockSpec((D_MODEL, tf), lambda r, f: (0, f)),
                  pl.BlockSpec((D_MODEL, tf), lambda r, f: (0, n_f + f)),
                  pl.BlockSpec((CONV_W, tf), lambda r, f: (0, f)),
                  pl.BlockSpec((CONV_W, tf), lambda r, f: (0, n_f + f)),
                  pl.BlockSpec((tf, D_MODEL), lambda r, f: (f, 0)),
                  pl.BlockSpec((1, D_MODEL), lambda r, f: (0, 0)),
                  pl.BlockSpec((HALO, tf), lambda r, f: (0, f)),
                  pl.BlockSpec((HALO, tf), lambda r, f: (0, f))],
        out_specs=[pl.BlockSpec((tm, D_MODEL), lambda r, f: (r, 0)),
                   pl.BlockSpec((None, n_f, 2, tf), lambda r, f: (r // tpb, 0, 0, 0)),
                   pl.BlockSpec((None, n_f, 2, tf), lambda r, f: (r // tpb, 0, 0, 0))],
        out_shape=[jax.ShapeDtypeStruct((t, D_MODEL), F32),
                   jax.ShapeDtypeStruct((n_batch, n_f, 2, tf), F32),
                   jax.ShapeDtypeStruct((n_batch, n_f, 2, tf), F32)],
        scratch_shapes=[pltpu.VMEM((tm, D_MODEL), F32),
                        pltpu.VMEM((tm, D_MODEL), BF16),
                        pltpu.VMEM((tm + HALO, tf), F32),
                        pltpu.VMEM((tm + HALO, tf), F32),
                        pltpu.VMEM((n_f, HALO, tf), F32),
                        pltpu.VMEM((n_f, HALO, tf), F32),
                        pltpu.VMEM((tm, tf), BF16)],
        compiler_params=_params(2),
        name="ffn_seq",
    )(x, mix, wo, g2, wu, wu, wc, wc, wd, g, halo_a, halo_b)


def _ffn_small_kernel(x_ref, mix_ref, wo_ref, g2_ref, wua_ref, wub_ref, wca_ref, wcb_ref, wd_ref, g_ref,
                      sa_ref, sb_ref,
                      y_ref, sa_new_ref, sb_new_ref, taila_ref, tailb_ref,
                      acc_ref, h2_ref, *, n_f, n_step, n_tail):
    f = pl.program_id(0)

    @pl.when(f == 0)
    def _():
        _mixer_residual(x_ref, mix_ref, wo_ref, g2_ref, acc_ref, h2_ref)

    def half(wu_ref, wc_ref, s_ref, s_new_ref, tail_ref):
        up = jnp.dot(h2_ref[...], wu_ref[...], preferred_element_type=F32)
        s0 = s_ref[:, 0, :]
        s1 = s_ref[:, 1, :]
        s_new_ref[:, 0, :] = s1
        s_new_ref[:, 1, :] = up[0:n_step, :]
        tail_ref[f] = up[n_step + n_tail - 2:n_step + n_tail, :]
        return s0 * wc_ref[0:1, :] + s1 * wc_ref[1:2, :] + up[0:n_step, :] * wc_ref[2:3, :]

    up_a = half(wua_ref, wca_ref, sa_ref, sa_new_ref, taila_ref)
    up_b = half(wub_ref, wcb_ref, sb_ref, sb_new_ref, tailb_ref)
    act = (up_b * jax.nn.sigmoid(up_b) * up_a).astype(BF16)
    acc_ref[0:n_step, :] += jnp.dot(act, wd_ref[...], preferred_element_type=F32)

    @pl.when(f == n_f - 1)
    def _():
        _final_norm(acc_ref, g_ref, y_ref)


def _ffn_small(x, mix, wo, g2, wu, wc, wd, g, s, *, tf, n_step):
    rows = x.shape[0]
    n_f = D_FF // tf
    lo = lambda f: (0, f)
    hi = lambda f: (0, n_f + f)
    return pl.pallas_call(
        functools.partial(_ffn_small_kernel, n_f=n_f, n_step=n_step, n_tail=rows - n_step),
        grid=(n_f,),
        in_specs=[pl.BlockSpec((rows, D_MODEL), lambda f: (0, 0)),
                  pl.BlockSpec((rows, D_MODEL), lambda f: (0, 0)),
                  pl.BlockSpec((D_MODEL, D_MODEL), lambda f: (0, 0), pipeline_mode=pl.Buffered(1)),
                  pl.BlockSpec((1, D_MODEL), lambda f: (0, 0)),
                  pl.BlockSpec((D_MODEL, tf), lo),
                  pl.BlockSpec((D_MODEL, tf), hi),
                  pl.BlockSpec((CONV_W, tf), lo),
                  pl.BlockSpec((CONV_W, tf), hi),
                  pl.BlockSpec((tf, D_MODEL), lambda f: (f, 0)),
                  pl.BlockSpec((1, D_MODEL), lambda f: (0, 0)),
                  pl.BlockSpec((None, n_step, CONV_W - 1, tf), lambda f: (0, 0, 0, f)),
                  pl.BlockSpec((None, n_step, CONV_W - 1, tf), lambda f: (0, 0, 0, n_f + f))],
        out_specs=[pl.BlockSpec((n_step, D_MODEL), lambda f: (0, 0)),
                   pl.BlockSpec((n_step, CONV_W - 1, tf), lambda f: (0, 0, f)),
                   pl.BlockSpec((n_step, CONV_W - 1, tf), lambda f: (0, 0, f)),
                   pl.BlockSpec((n_f, 2, tf), lambda f: (0, 0, 0)),
                   pl.BlockSpec((n_f, 2, tf), lambda f: (0, 0, 0))],
        out_shape=[jax.ShapeDtypeStruct((n_step, D_MODEL), F32),
                   jax.ShapeDtypeStruct((n_step, CONV_W - 1, D_FF), F32),
                   jax.ShapeDtypeStruct((n_step, CONV_W - 1, D_FF), F32),
                   jax.ShapeDtypeStruct((n_f, 2, tf), F32),
                   jax.ShapeDtypeStruct((n_f, 2, tf), F32)],
        scratch_shapes=[pltpu.VMEM((rows, D_MODEL), F32),
                        pltpu.VMEM((rows, D_MODEL), BF16)],
        compiler_params=_params(1),
        name="ffn_small",
    )(x, mix, wo, g2, wu, wu, wc, wc, wd, g, s, s)


def _tail_to_halo(tail):
    return jnp.pad(tail, ((HALO - 2, 0), (0, 0)))


def _join_tail_blocks(t):
    b, n_f, _, tf = t.shape
    return t.transpose(0, 2, 1, 3).reshape(b, 2, n_f * tf)


def kernel(x_prompt, x_sample, state_mlstm_C, state_mlstm_n, state_mlstm_m, state_shortconv,
           state_ffnconv, meta_tokens, norm1_g, w_in, b_if, w_shortconv, w_out, norm2_g,
           w_up, w_ffconv, w_down, norm_f_g):
    assert w_in.shape[0] == 1, "single-layer trunk"
    n_batch, seq, _ = x_prompt.shape
    n_dec = x_sample.shape[0]

    wt = w_in[0].T
    wg = _gate_cols(wt)
    bif = jnp.pad(b_if[0], (0, LANES - 2 * N_HEADS)).reshape(1, LANES)
    wsc = w_shortconv[0]
    wfc = w_ffconv[0]
    g1 = norm1_g[0].reshape(1, D_MODEL)
    g2 = norm2_g[0].reshape(1, D_MODEL)
    gf = norm_f_g.reshape(1, D_MODEL)

    xm = jnp.pad(meta_tokens, ((0, CHUNK - N_META), (0, 0)))
    hn_m, gates_m = _norm_gates_seq(xm, g1, wg, bif, tm=CHUNK, n_valid=N_META)
    wp, mix_m, c_m, n_m, m_m, sc_m = _prep_win(wt, hn_m, gates_m, wsc, n_valid=N_META)

    xp = x_prompt.reshape(n_batch * seq, D_MODEL)
    hn_p, gates_p = _norm_gates_seq(xp, g1, wg, bif, tm=NORM_TM, n_valid=CHUNK)
    xs = x_sample.reshape(n_dec, D_MODEL)
    hn_s, gif_s = _norm_gates_step(xs, g1, wg, bif)
    mprev = jnp.pad(state_mlstm_m[0], ((0, 0), (0, LANES - N_HEADS)))
    n_hb = state_mlstm_n[0].transpose(1, 0, 2)
    mix_p, c_p, n_p, m_p, sc_p, mix_s, c_s, n_s, m_s, sc_s, wu, wd, wo = _mixers(
        hn_p, wp, gates_p, wsc, c_m, n_m, m_m, _tail_to_halo(sc_m),
        (hn_s, gif_s, mprev, state_shortconv, state_mlstm_C[0], n_hb, w_up[0], w_down[0], w_out[0]),
        n_batch=n_batch, tm=MIXER_TM, n_valid=MIXER_TM)

    x_sm = jnp.concatenate([xs, meta_tokens], axis=0)
    mix_sm = jnp.concatenate([mix_s, mix_m[:N_META].astype(F32)], axis=0)
    y_s, fca_s, fcb_s, fca_m, fcb_m = _ffn_small(
        x_sm, mix_sm, wo, g2, wu, wfc, wd, gf, state_ffnconv, tf=FFN_TF, n_step=n_dec)
    y_p, fca_p, fcb_p = _ffn_seq(xp, mix_p, wo, g2, wu, wfc, wd, gf,
                                 _tail_to_halo(_join_tail_blocks(fca_m[None])[0]),
                                 _tail_to_halo(_join_tail_blocks(fcb_m[None])[0]),
                                 n_batch=n_batch, tm=FFN_TM, tf=FFN_TF, n_valid=FFN_TM)

    return (y_p.reshape(n_batch, seq, D_MODEL),
            y_s.reshape(n_dec, 1, D_MODEL),
            c_p[None],
            n_p.reshape(1, n_batch, N_HEADS, DK),
            m_p[:, :, 0, 0][None],
            sc_p[None],
            jnp.concatenate([_join_tail_blocks(fca_p), _join_tail_blocks(fcb_p)], axis=-1)[None],
            c_s[None],
            n_s.transpose(1, 0, 2)[None],
            m_s[:, :, 0].T[None],
            sc_s,
            jnp.concatenate([fca_s, fcb_s], axis=-1)[None])
```

```python
import functools

import jax
import jax.numpy as jnp
from jax import lax
from jax.experimental import pallas as pl
from jax.experimental.pallas import tpu as pltpu

D_MODEL = 2048
N_META = 16
N_HEADS = 8
DK = 128
DV = 256
D_CONV = D_MODEL
CONV_W = 3
D_FF = 5632
CHUNK = 128
EPS = 1e-6
K_SCALE = DK ** -0.5
NEG = -1e30
HEAD_COLS = 7 * DV + 2 * DK
LANES = 128
V7X_VMEM_BYTES = 64 * 1024 * 1024
V7X_VMEM_LIMIT = V7X_VMEM_BYTES * 7 // 8
HALO = 8
NORM_TM = 1024
MIXER_TM = 1024
FFN_TM = 512
FFN_TF = 512
FFN_SUB = 64
DOWN_GROUPS = 2
F32 = jnp.float32
BF16 = jnp.bfloat16

_Q, _K, _V, _O, _GB, _U, _CG, _GA, _BG = (0, 128, 256, 512, 768, 1024, 1280, 1536, 1792)


def _params(n_grid):
    return pltpu.CompilerParams(dimension_semantics=("arbitrary",) * n_grid,
                                vmem_limit_bytes=V7X_VMEM_LIMIT)


def _log_sigmoid(x):
    return jnp.minimum(x, 0.0) - jnp.log1p(jnp.exp(-jnp.abs(x)))


def _rms(x, g):
    return x * lax.rsqrt(jnp.mean(x * x, axis=-1, keepdims=True) + EPS) * g


_O_Q = 3 * D_CONV
_O_K = _O_Q + N_HEADS * DK
_O_V = _O_K + N_HEADS * DK
_O_O = _O_V + N_HEADS * DV
_O_GIF = _O_O + N_HEADS * DV
_O_GA = _O_GIF + 2 * N_HEADS
_O_GB = _O_GA + D_CONV


def _prep_win_kernel(u_ref, cg_ref, bg_ref, ga_ref, v_ref, o_ref, gb_ref, q_ref, k_ref,
                     hn_ref, gates_ref, wsc_ref,
                     wp_ref, mix_ref, cout_ref, nout_ref, mout_ref, scout_ref,
                     z_ref, cu_ref, *, n_valid):
    for ref, off, d in ((u_ref, _U, DV), (cg_ref, _CG, DV), (bg_ref, _BG, DV), (ga_ref, _GA, DV),
                        (v_ref, _V, DV), (o_ref, _O, DV), (gb_ref, _GB, DV), (q_ref, _Q, DK), (k_ref, _K, DK)):
        wp_ref[:, off:off + d] = ref[...].T.astype(BF16)
    z_ref[...] = jnp.dot(hn_ref[...], wp_ref[...], preferred_element_type=F32)

    row = lax.broadcasted_iota(jnp.int32, (CHUNK, CHUNK), 0)
    col = lax.broadcasted_iota(jnp.int32, (CHUNK, CHUNK), 1)
    causal = col <= row
    eye = col == row
    a_row = gates_ref[0:1, :]
    b_row = gates_ref[1:2, :]
    a_col = jnp.sum(jnp.where(eye, a_row, 0.0), axis=-1, keepdims=True)
    b_col = jnp.sum(jnp.where(eye, b_row, 0.0), axis=-1, keepdims=True)
    dmat = jnp.where(causal, b_col + a_row, NEG)
    m_t = jnp.maximum(b_col, jnp.max(dmat, axis=-1, keepdims=True))
    q = z_ref[:, _Q:_Q + DK]
    k = z_ref[:, _K:_K + DK] * K_SCALE
    v = z_ref[:, _V:_V + DV]
    qk = lax.dot_general(q.astype(BF16), k.astype(BF16), (((1,), (1,)), ((), ())), preferred_element_type=F32)
    s = qk * jnp.exp(dmat - m_t)
    num = jnp.dot(s.astype(BF16), v.astype(BF16), preferred_element_type=F32)
    den = jnp.sum(s, axis=-1, keepdims=True)
    bmix = num / jnp.maximum(jnp.abs(den), jnp.exp(-m_t))
    m_last = m_t[CHUNK - 1:CHUNK, :]
    g_col = jnp.exp(b_col[CHUNK - 1:CHUNK, :] + a_col - m_last)
    cout_ref[...] = jnp.dot(k.T.astype(BF16), (g_col * v).astype(BF16), preferred_element_type=F32)
    nout_ref[...] = jnp.sum(g_col * k, axis=0, keepdims=True)
    mout_ref[...] = jnp.broadcast_to(m_last, (1, DK))

    cu = z_ref[:, _CG:_CG + DV] * z_ref[:, _U:_U + DV]
    cu_ref[0:HALO, :] = jnp.zeros((HALO, DV), F32)
    cu_ref[HALO:HALO + CHUNK, :] = cu
    taps = (cu * wsc_ref[2:3, :]
            + cu_ref[HALO - 1:HALO - 1 + CHUNK, :] * wsc_ref[1:2, :]
            + cu_ref[HALO - 2:HALO - 2 + CHUNK, :] * wsc_ref[0:1, :])
    amix = jax.nn.sigmoid(z_ref[:, _GA:_GA + DV]) * taps
    og = jax.nn.sigmoid(z_ref[:, _GB:_GB + DV]) * jax.nn.sigmoid(z_ref[:, _O:_O + DV])
    mix_ref[...] = (amix * z_ref[:, _BG:_BG + DV] + og * bmix).astype(BF16)
    scout_ref[...] = cu_ref[HALO + n_valid - 2:HALO + n_valid, :]


def _gate_cols_kernel(gif_ref, wg_ref):
    gif = gif_ref[...].T
    lane = lax.broadcasted_iota(jnp.int32, gif.shape, 1)
    wg_ref[...] = jnp.where(lane < 2 * N_HEADS, gif, 0.0).astype(BF16)


def _gate_cols(wt):
    return pl.pallas_call(
        _gate_cols_kernel,
        grid=(1,),
        in_specs=[pl.BlockSpec((pl.Element(LANES), pl.Element(D_MODEL)), lambda i: (_O_GIF, 0))],
        out_specs=pl.BlockSpec((D_MODEL, LANES), lambda i: (0, 0)),
        out_shape=jax.ShapeDtypeStruct((D_MODEL, LANES), BF16),
        compiler_params=_params(1),
        name="gate_cols",
    )(wt)


def _prep_win(wt, hn, gates, wsc, *, n_valid):
    assert _O_GA % 8 == 0
    rows = lambda off, d: pl.BlockSpec((pl.Element(d), pl.Element(D_MODEL)),
                                       lambda h: (pl.multiple_of(off + d * h, 8), 0))
    return pl.pallas_call(
        functools.partial(_prep_win_kernel, n_valid=n_valid),
        grid=(N_HEADS,),
        in_specs=[rows(0, DV), rows(D_CONV, DV), rows(2 * D_CONV, DV), rows(_O_GA, DV), rows(_O_V, DV),
                  rows(_O_O, DV), rows(_O_GB, DV), rows(_O_Q, DK), rows(_O_K, DK),
                  pl.BlockSpec((CHUNK, D_MODEL), lambda h: (0, 0)),
                  pl.BlockSpec((None, 2, CHUNK), lambda h: (h, 0, 0)),
                  pl.BlockSpec((CONV_W, DV), lambda h: (0, h))],
        out_specs=[pl.BlockSpec((D_MODEL, HEAD_COLS), lambda h: (0, h)),
                   pl.BlockSpec((CHUNK, DV), lambda h: (0, h)),
                   pl.BlockSpec((None, DK, DV), lambda h: (h, 0, 0)),
                   pl.BlockSpec((None, 1, DK), lambda h: (h, 0, 0)),
                   pl.BlockSpec((None, 1, DK), lambda h: (h, 0, 0)),
                   pl.BlockSpec((2, DV), lambda h: (0, h))],
        out_shape=[jax.ShapeDtypeStruct((D_MODEL, N_HEADS * HEAD_COLS), BF16),
                   jax.ShapeDtypeStruct((CHUNK, D_MODEL), BF16),
                   jax.ShapeDtypeStruct((N_HEADS, DK, DV), F32),
                   jax.ShapeDtypeStruct((N_HEADS, 1, DK), F32),
                   jax.ShapeDtypeStruct((N_HEADS, 1, DK), F32),
                   jax.ShapeDtypeStruct((2, D_CONV), F32)],
        scratch_shapes=[pltpu.VMEM((CHUNK, HEAD_COLS), F32),
                        pltpu.VMEM((HALO + CHUNK, DV), F32)],
        compiler_params=_params(1),
        name="prep_win",
    )(*([wt] * 9), hn, gates, wsc)


def _norm_gates_seq_kernel(x_ref, g_ref, wg_ref, bif_ref, hn_ref, gates_ref, *, tm, n_valid):
    hn = _rms(x_ref[...], g_ref[...]).astype(BF16)
    hn_ref[...] = hn
    gif = jnp.dot(hn, wg_ref[...], preferred_element_type=F32) + bif_ref[...]
    row = lax.broadcasted_iota(jnp.int32, (CHUNK, CHUNK), 0)
    col = lax.broadcasted_iota(jnp.int32, (CHUNK, CHUNK), 1)
    upper = (row <= col).astype(F32)
    for c in range(tm // CHUNK):
        gt = gif[c * CHUNK:(c + 1) * CHUNK, :].T
        ig = gt[0:N_HEADS, :]
        lf = _log_sigmoid(gt[N_HEADS:2 * N_HEADS, :])
        if n_valid < CHUNK:
            tok = lax.broadcasted_iota(jnp.int32, (N_HEADS, CHUNK), 1)
            ig = jnp.where(tok < n_valid, ig, NEG)
            lf = jnp.where(tok < n_valid, lf, 0.0)
        b = jnp.dot(lf, upper, preferred_element_type=F32, precision=lax.Precision.HIGHEST)
        a = ig - b
        for j in range(N_HEADS):
            gates_ref[j, 0:1, c * CHUNK:(c + 1) * CHUNK] = a[j:j + 1, :]
            gates_ref[j, 1:2, c * CHUNK:(c + 1) * CHUNK] = b[j:j + 1, :]


def _norm_gates_seq(x, g, wg, bif, *, tm, n_valid):
    t = x.shape[0]
    return pl.pallas_call(
        functools.partial(_norm_gates_seq_kernel, tm=tm, n_valid=n_valid),
        grid=(t // tm,),
        in_specs=[pl.BlockSpec((tm, D_MODEL), lambda r: (r, 0)),
                  pl.BlockSpec((1, D_MODEL), lambda r: (0, 0)),
                  pl.BlockSpec((D_MODEL, LANES), lambda r: (0, 0)),
                  pl.BlockSpec((1, LANES), lambda r: (0, 0))],
        out_specs=[pl.BlockSpec((tm, D_MODEL), lambda r: (r, 0)),
                   pl.BlockSpec((N_HEADS, 2, tm), lambda r: (0, 0, r))],
        out_shape=[jax.ShapeDtypeStruct((t, D_MODEL), BF16),
                   jax.ShapeDtypeStruct((N_HEADS, 2, t), F32)],
        compiler_params=_params(1),
        name="norm_gates_seq",
    )(x, g, wg, bif)


def _norm_gates_step_kernel(x_ref, g_ref, wg_ref, bif_ref, hn_ref, gif_ref):
    hn = _rms(x_ref[...], g_ref[...]).astype(BF16)
    hn_ref[...] = hn
    gif_ref[...] = jnp.dot(hn, wg_ref[...], preferred_element_type=F32) + bif_ref[...]


def _norm_gates_step(x, g, wg, bif):
    t = x.shape[0]
    return pl.pallas_call(
        _norm_gates_step_kernel,
        grid=(1,),
        in_specs=[pl.BlockSpec((t, D_MODEL), lambda r: (0, 0)),
                  pl.BlockSpec((1, D_MODEL), lambda r: (0, 0)),
                  pl.BlockSpec((D_MODEL, LANES), lambda r: (0, 0)),
                  pl.BlockSpec((1, LANES), lambda r: (0, 0))],
        out_specs=[pl.BlockSpec((t, D_MODEL), lambda r: (0, 0)),
                   pl.BlockSpec((t, LANES), lambda r: (0, 0))],
        out_shape=[jax.ShapeDtypeStruct((t, D_MODEL), BF16),
                   jax.ShapeDtypeStruct((t, LANES), F32)],
        compiler_params=_params(1),
        name="norm_gates_step",
    )(x, g, wg, bif)


def _mixers_kernel(*refs, tm, tiles_per_batch, n_valid, step_bb):
    n_seq = (8, 5, 12)
    n_step = (9, 8, 5) if step_bb else (0, 0, 0)
    cuts = [0]
    for a, b in zip(n_seq, n_step):
        cuts += [cuts[-1] + a, cuts[-1] + a + b]
    seq_in, step_in, seq_out, step_out, seq_scr, step_scr = (refs[i:j] for i, j in zip(cuts[:-1], cuts[1:]))
    hn_ref, w_ref, gates_ref, wsc_ref, c0_ref, n0_ref, m0_ref, sc0_ref = seq_in
    mix_ref, cout_ref, nout_ref, mout_ref, scout_ref = seq_out
    z_ref, cu_ref, c_ref, n_ref, m_ref, dmat_ref, qk_ref, sv_ref, kv_ref, bmix_ref, amix_ref, og_ref = seq_scr
    head = pl.program_id(0)
    tile = pl.program_id(1)
    rb = tile % tiles_per_batch
    nc = tm // CHUNK

    @pl.when(rb == 0)
    def _():
        c_ref[...] = c0_ref[...]
        n_ref[...] = n0_ref[...]
        m_ref[...] = m0_ref[...]
        cu_ref[0:HALO, :] = sc0_ref[...]

    row = lax.broadcasted_iota(jnp.int32, (CHUNK, CHUNK), 0)
    col = lax.broadcasted_iota(jnp.int32, (CHUNK, CHUNK), 1)
    causal = col <= row
    eye = col == row
    st = [dict() for _ in range(nc)]
    rows_of = lambda c: slice(c * CHUNK, (c + 1) * CHUNK)
    k_of = lambda rows: z_ref[rows, _K:_K + DK] * K_SCALE

    def proj(j):
        cols = slice(j * DV, (j + 1) * DV)
        z_ref[:, cols] = jnp.dot(hn_ref[...], w_ref[:, cols], preferred_element_type=F32)

    def gate_terms(c):
        rows, d = rows_of(c), st[c]
        a_row = gates_ref[0:1, rows]
        b_row = gates_ref[1:2, rows]
        d["a_col"] = jnp.sum(jnp.where(eye, a_row, 0.0), axis=-1, keepdims=True)
        d["b_col"] = jnp.sum(jnp.where(eye, b_row, 0.0), axis=-1, keepdims=True)
        dmat = jnp.where(causal, d["b_col"] + a_row, NEG)
        dmat_ref[rows, :] = dmat
        d["mloc"] = jnp.max(dmat, axis=-1, keepdims=True)
        qk_ref[rows, :] = lax.dot_general(z_ref[rows, _Q:_Q + DK].astype(BF16), k_of(rows).astype(BF16),
                                          (((1,), (1,)), ((), ())), preferred_element_type=F32)

    def stabilisers():
        m_prev = m_ref[:, 0:1]
        for c in range(nc):
            d = st[c]
            d["m_prev"] = m_prev
            d["bm"] = d["b_col"] + m_prev
            d["m_t"] = jnp.maximum(d["bm"], d["mloc"])
            d["m_last"] = d["m_t"][CHUNK - 1:CHUNK, :]
            m_prev = d["m_last"]
        m_ref[...] = jnp.broadcast_to(m_prev, (1, DK))

    def intra(c):
        rows, d = rows_of(c), st[c]
        k = k_of(rows)
        v = z_ref[rows, _V:_V + DV]
        s = qk_ref[rows, :] * jnp.exp(dmat_ref[rows, :] - d["m_t"])
        sv_ref[rows, :] = jnp.dot(s.astype(BF16), v.astype(BF16), preferred_element_type=F32)
        d["ssum"] = jnp.sum(s, axis=-1, keepdims=True)
        d["decay"] = jnp.exp(d["bm"] - d["m_t"])
        d["hden"] = jnp.exp(-d["m_t"])
        b_last = d["b_col"][CHUNK - 1:CHUNK, :]
        g_col = jnp.exp(b_last + d["a_col"] - d["m_last"])
        d["carry"] = jnp.exp(b_last + d["m_prev"] - d["m_last"])
        kv_ref[c] = jnp.dot(k.T.astype(BF16), (g_col * v).astype(BF16), preferred_element_type=F32)
        d["kn"] = jnp.sum(g_col * k, axis=0, keepdims=True)

    def inter(c):
        rows, d = rows_of(c), st[c]
        q = z_ref[rows, _Q:_Q + DK]
        c_prev = c_ref[...]
        n_prev = n_ref[...]
        num = (d["decay"] * jnp.dot(q.astype(BF16), c_prev.astype(BF16), preferred_element_type=F32)
               + sv_ref[rows, :])
        den = d["decay"] * jnp.sum(q * n_prev, axis=-1, keepdims=True) + d["ssum"]
        bmix_ref[rows, :] = num / jnp.maximum(jnp.abs(den), d["hden"])
        c_ref[...] = d["carry"] * c_prev + kv_ref[c]
        n_ref[...] = d["carry"] * n_prev + d["kn"]

    def out_gate(c):
        rows = rows_of(c)
        og_ref[rows, :] = jax.nn.sigmoid(z_ref[rows, _GB:_GB + DV]) * jax.nn.sigmoid(z_ref[rows, _O:_O + DV])

    def conv(c):
        lo = c * CHUNK
        rows = rows_of(c)
        cu = z_ref[rows, _CG:_CG + DV] * z_ref[rows, _U:_U + DV]
        cu_ref[HALO + lo:HALO + lo + CHUNK, :] = cu
        amix_ref[rows, :] = (cu * wsc_ref[2:3, :]
                             + cu_ref[HALO - 1 + lo:HALO - 1 + lo + CHUNK, :] * wsc_ref[1:2, :]
                             + cu_ref[HALO - 2 + lo:HALO - 2 + lo + CHUNK, :] * wsc_ref[0:1, :])

    def gate(c):
        rows = rows_of(c)
        amix_ref[rows, :] = jax.nn.sigmoid(z_ref[rows, _GA:_GA + DV]) * amix_ref[rows, :]

    def merge(c):
        rows = rows_of(c)
        mix = amix_ref[rows, :] * z_ref[rows, _BG:_BG + DV] + og_ref[rows, :] * bmix_ref[rows, :]
        mix_ref[rows, :] = mix.astype(BF16)

    piece = lambda off: off // DV
    assert (piece(_Q), piece(_V), piece(_O), piece(_GB)) == (0, 1, 2, 3)
    assert sorted(map(piece, (_U, _CG))) == [4, 5] and (piece(_GA), piece(_BG)) == (6, 7)
    each = lambda fn: [functools.partial(fn, c) for c in range(nc)]
    half = (nc + 1) // 2
    after = {0: [],
             1: each(gate_terms),
             2: [stabilisers] + each(intra)[:half] + each(inter)[:1],
             3: each(intra)[half:] + each(inter)[1:3] + each(out_gate)[:half],
             4: each(inter)[3:6] + each(out_gate)[half:],
             5: each(inter)[6:] + each(conv),
             6: each(gate),
             7: each(merge)}

    if step_bb:
        bb = step_bb
        hns_ref, gifs_ref, mprevs_ref, sc_ref, cs_ref, ns_ref, wu32_ref, wd32_ref, wo32_ref = step_in
        mixs_ref, csout_ref, nsout_ref, msout_ref, scnew_ref, wu16_ref, wd16_ref, wo16_ref = step_out
        zs_ref, aouts_ref, qts_ref, kts_ref, hbs_ref = step_scr

        def round_weights():
            wu16_ref[...] = wu32_ref[...].astype(BF16)
            wd16_ref[...] = wd32_ref[...].astype(BF16)
            wo16_ref[...] = wo32_ref[...].astype(BF16)

        @pl.when(tile == 0)
        def _():
            z = jnp.dot(hns_ref[...], w_ref[...], preferred_element_type=F32)
            zs_ref[...] = z
            cu = z[:, _CG:_CG + DV] * z[:, _U:_U + DV]
            s0 = sc_ref[:, 0, :]
            s1 = sc_ref[:, 1, :]
            taps = s0 * wsc_ref[0:1, :] + s1 * wsc_ref[1:2, :] + cu * wsc_ref[2:3, :]
            aouts_ref[...] = z[:, _BG:_BG + DV] * taps
            scnew_ref[:, 0, :] = s1
            scnew_ref[:, 1, :] = cu
            qt = z[:, _Q:_Q + DK].T
            kt = (z[:, _K:_K + DK] * K_SCALE).T
            for jj in range(hns_ref.shape[0] // bb):
                qts_ref[jj] = qt[:, jj * bb:(jj + 1) * bb]
                kts_ref[jj] = kt[:, jj * bb:(jj + 1) * bb]

        srows = pl.ds(pl.multiple_of(tile * bb, bb), bb)
        sd = {}

        def step_gates():
            lane = lax.broadcasted_iota(jnp.int32, (bb, LANES), 1)
            pick = lambda x, idx: jnp.sum(jnp.where(lane == idx, x, 0.0), axis=-1, keepdims=True)
            gif = gifs_ref[srows, :]
            ig = pick(gif, head)
            lf = _log_sigmoid(pick(gif, head + N_HEADS))
            m_prev = pick(mprevs_ref[srows, :], head)
            q = zs_ref[srows, _Q:_Q + DK]
            k = zs_ref[srows, _K:_K + DK] * K_SCALE
            n_prev = ns_ref[...]
            m_t = jnp.maximum(lf + m_prev, ig)
            sd["w"] = jnp.exp(ig - m_t)
            sd["decay"] = jnp.exp(lf + m_prev - m_t)
            sd["s"] = jnp.sum(q * k, axis=-1, keepdims=True) * sd["w"]
            den = sd["decay"] * jnp.sum(q * n_prev, axis=-1, keepdims=True) + sd["s"]
            sd["inv"] = 1.0 / jnp.maximum(jnp.abs(den), jnp.exp(-m_t))
            nsout_ref[...] = sd["decay"] * n_prev + sd["w"] * k
            msout_ref[...] = jnp.broadcast_to(m_t, (bb, DK))

        def step_rows(lo, hi):
            qt = qts_ref[tile]
            kt = kts_ref[tile]
            v = zs_ref[srows, _V:_V + DV]
            for i in range(lo, hi):
                c_prev = cs_ref[i]
                d_i = sd["decay"][i:i + 1, :]
                v_i = v[i:i + 1, :]
                qc = jnp.sum(qt[:, i:i + 1] * c_prev, axis=0, keepdims=True)
                hbs_ref[i:i + 1, :] = (d_i * qc + sd["s"][i:i + 1, :] * v_i) * sd["inv"][i:i + 1, :]
                csout_ref[i] = d_i * c_prev + (sd["w"][i:i + 1, :] * kt[:, i:i + 1]) * v_i

        def step_merge():
            b_out = jax.nn.sigmoid(zs_ref[srows, _O:_O + DV]) * hbs_ref[...]
            mixs_ref[...] = (jax.nn.sigmoid(zs_ref[srows, _GA:_GA + DV]) * aouts_ref[srows, :]
                             + jax.nn.sigmoid(zs_ref[srows, _GB:_GB + DV]) * b_out)

        after[0] += [round_weights, step_gates, functools.partial(step_rows, 0, bb // 2)]
        after[5] += [functools.partial(step_rows, bb // 2, bb)]
        after[6] = [step_merge] + after[6]

    for j in range(HEAD_COLS // DV):
        proj(j)
        for fn in after.get(j, []):
            fn()

    @pl.when(rb == tiles_per_batch - 1)
    def _():
        cout_ref[...] = c_ref[...]
        nout_ref[...] = n_ref[...]
        mout_ref[...] = m_ref[...]
        scout_ref[...] = cu_ref[HALO + n_valid - 2:HALO + n_valid, :]

    cu_ref[0:HALO, :] = cu_ref[tm:tm + HALO, :]


def _mixers(hn, wp, gates, wsc, c0, n0, m0, sc0, step=None, *, n_batch, tm, n_valid):
    t = hn.shape[0]
    n_tiles = t // tm
    tpb = n_tiles // n_batch
    in_specs = [pl.BlockSpec((tm, D_MODEL), lambda h, r: (r, 0)),
                pl.BlockSpec((D_MODEL, HEAD_COLS), lambda h, r: (0, h)),
                pl.BlockSpec((None, 2, tm), lambda h, r: (h, 0, r)),
                pl.BlockSpec((CONV_W, DV), lambda h, r: (0, h)),
                pl.BlockSpec((None, DK, DV), lambda h, r: (h, 0, 0)),
                pl.BlockSpec((None, 1, DK), lambda h, r: (h, 0, 0)),
                pl.BlockSpec((None, 1, DK), lambda h, r: (h, 0, 0)),
                pl.BlockSpec((HALO, DV), lambda h, r: (0, h))]
    out_specs = [pl.BlockSpec((tm, DV), lambda h, r: (r, h)),
                 pl.BlockSpec((None, None, DK, DV), lambda h, r: (r // tpb, h, 0, 0)),
                 pl.BlockSpec((None, None, 1, DK), lambda h, r: (r // tpb, h, 0, 0)),
                 pl.BlockSpec((None, None, 1, DK), lambda h, r: (r // tpb, h, 0, 0)),
                 pl.BlockSpec((None, 2, DV), lambda h, r: (r // tpb, 0, h))]
    out_shape = [jax.ShapeDtypeStruct((t, D_MODEL), BF16),
                 jax.ShapeDtypeStruct((n_batch, N_HEADS, DK, DV), F32),
                 jax.ShapeDtypeStruct((n_batch, N_HEADS, 1, DK), F32),
                 jax.ShapeDtypeStruct((n_batch, N_HEADS, 1, DK), F32),
                 jax.ShapeDtypeStruct((n_batch, 2, D_CONV), F32)]
    scratch = [pltpu.VMEM((tm, HEAD_COLS), F32),
               pltpu.VMEM((tm + HALO, DV), F32),
               pltpu.VMEM((DK, DV), F32),
               pltpu.VMEM((1, DK), F32),
               pltpu.VMEM((1, DK), F32),
               pltpu.VMEM((tm, CHUNK), F32),
               pltpu.VMEM((tm, CHUNK), F32),
               pltpu.VMEM((tm, DV), F32),
               pltpu.VMEM((tm // CHUNK, DK, DV), F32),
               pltpu.VMEM((tm, DV), F32),
               pltpu.VMEM((tm, DV), F32),
               pltpu.VMEM((tm, DV), F32)]
    step_bb = 0
    if step is not None:
        nb = step[0].shape[0]
        assert nb % n_tiles == 0
        step_bb = nb // n_tiles
        in_specs += [pl.BlockSpec((nb, D_MODEL), lambda h, r: (0, 0)),
                     pl.BlockSpec((nb, LANES), lambda h, r: (0, 0)),
                     pl.BlockSpec((nb, LANES), lambda h, r: (0, 0)),
                     pl.BlockSpec((None, nb, CONV_W - 1, DV), lambda h, r: (0, 0, 0, h)),
                     pl.BlockSpec((step_bb, None, DK, DV), lambda h, r: (r, h, 0, 0)),
                     pl.BlockSpec((None, step_bb, DK), lambda h, r: (h, r, 0))]
        n_steps = N_HEADS * n_tiles
        wu32, wd32, wo32 = step[6], step[7], step[8]
        assert wu32.shape[0] % (16 * n_steps) == 0 and wd32.shape[0] % (8 * n_steps) == 0
        assert wo32.shape[0] % (16 * n_steps) == 0
        wu_rows = wu32.shape[0] // n_steps
        wd_rows, wd_cols = 2 * wd32.shape[0] // n_steps, wd32.shape[1] // 2
        assert wd_rows % 16 == 0
        wu_spec = pl.BlockSpec((wu_rows, wu32.shape[1]), lambda h, r: (h * n_tiles + r, 0))
        wd_spec = pl.BlockSpec((wd_rows, wd_cols), lambda h, r: ((h * n_tiles + r) // 2, (h * n_tiles + r) % 2))
        wo_spec = pl.BlockSpec((wo32.shape[0] // n_steps, wo32.shape[1]), lambda h, r: (h * n_tiles + r, 0))
        in_specs += [wu_spec, wd_spec, wo_spec]
        out_specs += [pl.BlockSpec((step_bb, DV), lambda h, r: (r, h)),
                      pl.BlockSpec((step_bb, None, DK, DV), lambda h, r: (r, h, 0, 0)),
                      pl.BlockSpec((None, step_bb, DK), lambda h, r: (h, r, 0)),
                      pl.BlockSpec((None, step_bb, DK), lambda h, r: (h, r, 0)),
                      pl.BlockSpec((None, nb, CONV_W - 1, DV), lambda h, r: (0, 0, 0, h)),
                      wu_spec, wd_spec, wo_spec]
        out_shape += [jax.ShapeDtypeStruct((nb, D_MODEL), F32),
                      jax.ShapeDtypeStruct((nb, N_HEADS, DK, DV), F32),
                      jax.ShapeDtypeStruct((N_HEADS, nb, DK), F32),
                      jax.ShapeDtypeStruct((N_HEADS, nb, DK), F32),
                      jax.ShapeDtypeStruct((1, nb, CONV_W - 1, D_CONV), F32),
                      jax.ShapeDtypeStruct(wu32.shape, BF16),
                      jax.ShapeDtypeStruct(wd32.shape, BF16),
                      jax.ShapeDtypeStruct(wo32.shape, BF16)]
        scratch += [pltpu.VMEM((nb, HEAD_COLS), F32),
                    pltpu.VMEM((nb, DV), F32),
                    pltpu.VMEM((n_tiles, DK, step_bb), F32),
                    pltpu.VMEM((n_tiles, DK, step_bb), F32),
                    pltpu.VMEM((step_bb, DV), F32)]
    return pl.pallas_call(
        functools.partial(_mixers_kernel, tm=tm, tiles_per_batch=tpb, n_valid=n_valid, step_bb=step_bb),
        grid=(N_HEADS, n_tiles),
        in_specs=in_specs,
        out_specs=out_specs,
        out_shape=out_shape,
        scratch_shapes=scratch,
        compiler_params=_params(2),
        name="mixers",
    )(hn, wp, gates, wsc, c0, n0, m0, sc0, *(step or ()))


def _mixer_residual(x_ref, mix_ref, wo_ref, g2_ref, acc_ref, h2_ref):
    mix = mix_ref[...]
    if mix.dtype != BF16:
        mix = mix.astype(BF16)
    acc_ref[...] = x_ref[...] + jnp.dot(mix, wo_ref[...], preferred_element_type=F32)
    rows = acc_ref.shape[0]
    for lo in range(0, rows, FFN_SUB):
        hi = min(lo + FFN_SUB, rows)
        h2_ref[lo:hi, :] = _rms(acc_ref[lo:hi, :], g2_ref[...]).astype(BF16)


def _final_norm(acc_ref, g_ref, y_ref):
    rows = y_ref.shape[0]
    for lo in range(0, rows, FFN_SUB):
        hi = min(lo + FFN_SUB, rows)
        y_ref[lo:hi, :] = _rms(acc_ref[lo:hi, :], g_ref[...])


def _ffn_seq_kernel(x_ref, mix_ref, wo_ref, g2_ref, wua_ref, wub_ref, wca_ref, wcb_ref, wd_ref, g_ref,
                    ha0_ref, hb0_ref,
                    y_ref, fc_ref,
                    acc_ref, h2_ref, ua_ref, ub_ref, ha_ref, hb_ref, act_ref,
                    *, tm, tiles_per_batch, n_valid, n_f):
    r = pl.program_id(0)
    f = pl.program_id(1)
    rb = r % tiles_per_batch

    @pl.when((r == 0) & (f == 0))
    def _():
        ha_ref[...] = jnp.zeros_like(ha_ref)
        hb_ref[...] = jnp.zeros_like(hb_ref)

    @pl.when(f == 0)
    def _():
        _mixer_residual(x_ref, mix_ref, wo_ref, g2_ref, acc_ref, h2_ref)

    ua_ref[0:HALO, :] = jnp.where(rb == 0, ha0_ref[...], ha_ref[f])
    ub_ref[0:HALO, :] = jnp.where(rb == 0, hb0_ref[...], hb_ref[f])
    group = tm // DOWN_GROUPS

    def up_proj(gi):
        rows = slice(gi * group, (gi + 1) * group)
        urows = slice(HALO + gi * group, HALO + (gi + 1) * group)
        ua_ref[urows, :] = jnp.dot(h2_ref[rows, :], wua_ref[...], preferred_element_type=F32)
        ub_ref[urows, :] = jnp.dot(h2_ref[rows, :], wub_ref[...], preferred_element_type=F32)

    def conv(u_ref, wc_ref, lo):
        win = u_ref[lo:lo + HALO + FFN_SUB, :]
        back1 = pltpu.roll(win, 1, axis=0)[HALO:, :]
        back2 = pltpu.roll(win, 2, axis=0)[HALO:, :]
        return win[HALO:, :] * wc_ref[2:3, :] + back1 * wc_ref[1:2, :] + back2 * wc_ref[0:1, :]

    def glu_down(gi):
        for lo in range(gi * group, (gi + 1) * group, FFN_SUB):
            up_a = conv(ua_ref, wca_ref, lo)
            up_b = conv(ub_ref, wcb_ref, lo)
            act_ref[lo:lo + FFN_SUB, :] = (up_b * jax.nn.sigmoid(up_b) * up_a).astype(BF16)
        rows = slice(gi * group, (gi + 1) * group)
        acc_ref[rows, :] += jnp.dot(act_ref[rows, :], wd_ref[...], preferred_element_type=F32)

    up_proj(0)
    for gi in range(1, DOWN_GROUPS):
        up_proj(gi)
        glu_down(gi - 1)
    glu_down(DOWN_GROUPS - 1)
    for u_ref, halo_ref, first in ((ua_ref, ha_ref, 0), (ub_ref, hb_ref, n_f)):
        halo_ref[f] = u_ref[tm:tm + HALO, :]
        fc_ref[first + f] = u_ref[HALO + n_valid - 2:HALO + n_valid, :]

    @pl.when(f == n_f - 1)
    def _():
        _final_norm(acc_ref, g_ref, y_ref)


def _ffn_seq(x, mix, wo, g2, wu, wc, wd, g, halo_a, halo_b, *, n_batch, tm, tf, n_valid):
    t = x.shape[0]
    tpb = t // n_batch // tm
    n_f = D_FF // tf
    return pl.pallas_call(
        functools.partial(_ffn_seq_kernel, tm=tm, tiles_per_batch=tpb, n_valid=n_valid, n_f=n_f),
        grid=(t // tm, n_f),
        in_specs=[pl.BlockSpec((tm, D_MODEL), lambda r, f: (r, 0)),
                  pl.BlockSpec((tm, D_MODEL), lambda r, f: (r, 0)),
                  pl.BlockSpec((D_MODEL, D_MODEL), lambda r, f: (0, 0), pipeline_mode=pl.Buffered(1)),
                  pl.BlockSpec((1, D_MODEL), lambda r, f: (0, 0)),
                  pl.BlockSpec((D_MODEL, tf), lambda r, f: (0, f)),
                  pl.BlockSpec((D_MODEL, tf), lambda r, f: (0, n_f + f)),
                  pl.BlockSpec((CONV_W, tf), lambda r, f: (0, f)),
                  pl.BlockSpec((CONV_W, tf), lambda r, f: (0, n_f + f)),
                  pl.BlockSpec((tf, D_MODEL), lambda r, f: (f, 0)),
                  pl.BlockSpec((1, D_MODEL), lambda r, f: (0, 0)),
                  pl.BlockSpec((HALO, tf), lambda r, f: (0, f)),
                  pl.BlockSpec((HALO, tf), lambda r, f: (0, f))],
        out_specs=[pl.BlockSpec((tm, D_MODEL), lambda r, f: (r, 0)),
                   pl.BlockSpec((None, 2 * n_f, 2, tf), lambda r, f: (r // tpb, 0, 0, 0))],
        out_shape=[jax.ShapeDtypeStruct((t, D_MODEL), F32),
                   jax.ShapeDtypeStruct((n_batch, 2 * n_f, 2, tf), F32)],
        scratch_shapes=[pltpu.VMEM((tm, D_MODEL), F32),
                        pltpu.VMEM((tm, D_MODEL), BF16),
                        pltpu.VMEM((tm + HALO, tf), F32),
                        pltpu.VMEM((tm + HALO, tf), F32),
                        pltpu.VMEM((n_f, HALO, tf), F32),
                        pltpu.VMEM((n_f, HALO, tf), F32),
                        pltpu.VMEM((tm, tf), BF16)],
        compiler_params=_params(2),
        name="ffn_seq",
    )(x, mix, wo, g2, wu, wu, wc, wc, wd, g, halo_a, halo_b)


def _ffn_small_kernel(x_ref, mix_ref, wo_ref, g2_ref, wua_ref, wub_ref, wca_ref, wcb_ref, wd_ref, g_ref,
                      sa_ref, sb_ref,
                      y_ref, sa_new_ref, sb_new_ref, taila_ref, tailb_ref,
                      acc_ref, h2_ref, *, n_f, n_step, n_tail):
    f = pl.program_id(0)

    @pl.when(f == 0)
    def _():
        _mixer_residual(x_ref, mix_ref, wo_ref, g2_ref, acc_ref, h2_ref)

    def half(wu_ref, wc_ref, s_ref, s_new_ref, tail_ref):
        up = jnp.dot(h2_ref[...], wu_ref[...], preferred_element_type=F32)
        s0 = s_ref[:, 0, :]
        s1 = s_ref[:, 1, :]
        s_new_ref[:, 0, :] = s1
        s_new_ref[:, 1, :] = up[0:n_step, :]
        tail_ref[f] = up[n_step + n_tail - 2:n_step + n_tail, :]
        return s0 * wc_ref[0:1, :] + s1 * wc_ref[1:2, :] + up[0:n_step, :] * wc_ref[2:3, :]

    up_a = half(wua_ref, wca_ref, sa_ref, sa_new_ref, taila_ref)
    up_b = half(wub_ref, wcb_ref, sb_ref, sb_new_ref, tailb_ref)
    act = (up_b * jax.nn.sigmoid(up_b) * up_a).astype(BF16)
    acc_ref[0:n_step, :] += jnp.dot(act, wd_ref[...], preferred_element_type=F32)

    @pl.when(f == n_f - 1)
    def _():
        _final_norm(acc_ref, g_ref, y_ref)


def _ffn_small(x, mix, wo, g2, wu, wc, wd, g, s, *, tf, n_step):
    rows = x.shape[0]
    n_f = D_FF // tf
    lo = lambda f: (0, f)
    hi = lambda f: (0, n_f + f)
    return pl.pallas_call(
        functools.partial(_ffn_small_kernel, n_f=n_f, n_step=n_step, n_tail=rows - n_step),
        grid=(n_f,),
        in_specs=[pl.BlockSpec((rows, D_MODEL), lambda f: (0, 0)),
                  pl.BlockSpec((rows, D_MODEL), lambda f: (0, 0)),
                  pl.BlockSpec((D_MODEL, D_MODEL), lambda f: (0, 0), pipeline_mode=pl.Buffered(1)),
                  pl.BlockSpec((1, D_MODEL), lambda f: (0, 0)),
                  pl.BlockSpec((D_MODEL, tf), lo),
                  pl.BlockSpec((D_MODEL, tf), hi),
                  pl.BlockSpec((CONV_W, tf), lo),
                  pl.BlockSpec((CONV_W, tf), hi),
                  pl.BlockSpec((tf, D_MODEL), lambda f: (f, 0)),
                  pl.BlockSpec((1, D_MODEL), lambda f: (0, 0)),
                  pl.BlockSpec((None, n_step, CONV_W - 1, tf), lambda f: (0, 0, 0, f)),
                  pl.BlockSpec((None, n_step, CONV_W - 1, tf), lambda f: (0, 0, 0, n_f + f))],
        out_specs=[pl.BlockSpec((n_step, D_MODEL), lambda f: (0, 0)),
                   pl.BlockSpec((n_step, CONV_W - 1, tf), lambda f: (0, 0, f)),
                   pl.BlockSpec((n_step, CONV_W - 1, tf), lambda f: (0, 0, f)),
                   pl.BlockSpec((n_f, 2, tf), lambda f: (0, 0, 0)),
                   pl.BlockSpec((n_f, 2, tf), lambda f: (0, 0, 0))],
        out_shape=[jax.ShapeDtypeStruct((n_step, D_MODEL), F32),
                   jax.ShapeDtypeStruct((n_step, CONV_W - 1, D_FF), F32),
                   jax.ShapeDtypeStruct((n_step, CONV_W - 1, D_FF), F32),
                   jax.ShapeDtypeStruct((n_f, 2, tf), F32),
                   jax.ShapeDtypeStruct((n_f, 2, tf), F32)],
        scratch_shapes=[pltpu.VMEM((rows, D_MODEL), F32),
                        pltpu.VMEM((rows, D_MODEL), BF16)],
        compiler_params=_params(1),
        name="ffn_small",
    )(x, mix, wo, g2, wu, wu, wc, wc, wd, g, s, s)


def _tail_to_halo(tail):
    return jnp.pad(tail, ((HALO - 2, 0), (0, 0)))


def _join_tail_blocks(t):
    b, n_f, _, tf = t.shape
    return t.transpose(0, 2, 1, 3).reshape(b, 2, n_f * tf)


def kernel(x_prompt, x_sample, state_mlstm_C, state_mlstm_n, state_mlstm_m, state_shortconv,
           state_ffnconv, meta_tokens, norm1_g, w_in, b_if, w_shortconv, w_out, norm2_g,
           w_up, w_ffconv, w_down, norm_f_g):
    assert w_in.shape[0] == 1, "single-layer trunk"
    n_batch, seq, _ = x_prompt.shape
    n_dec = x_sample.shape[0]

    wt = w_in[0].T
    wg = _gate_cols(wt)
    bif = jnp.pad(b_if[0], (0, LANES - 2 * N_HEADS)).reshape(1, LANES)
    wsc = w_shortconv[0]
    wfc = w_ffconv[0]
    g1 = norm1_g[0].reshape(1, D_MODEL)
    g2 = norm2_g[0].reshape(1, D_MODEL)
    gf = norm_f_g.reshape(1, D_MODEL)

    xm = jnp.pad(meta_tokens, ((0, CHUNK - N_META), (0, 0)))
    hn_m, gates_m = _norm_gates_seq(xm, g1, wg, bif, tm=CHUNK, n_valid=N_META)
    wp, mix_m, c_m, n_m, m_m, sc_m = _prep_win(wt, hn_m, gates_m, wsc, n_valid=N_META)

    xp = x_prompt.reshape(n_batch * seq, D_MODEL)
    hn_p, gates_p = _norm_gates_seq(xp, g1, wg, bif, tm=NORM_TM, n_valid=CHUNK)
    xs = x_sample.reshape(n_dec, D_MODEL)
    hn_s, gif_s = _norm_gates_step(xs, g1, wg, bif)
    mprev = jnp.pad(state_mlstm_m[0], ((0, 0), (0, LANES - N_HEADS)))
    n_hb = state_mlstm_n[0].transpose(1, 0, 2)
    mix_p, c_p, n_p, m_p, sc_p, mix_s, c_s, n_s, m_s, sc_s, wu, wd, wo = _mixers(
        hn_p, wp, gates_p, wsc, c_m, n_m, m_m, _tail_to_halo(sc_m),
        (hn_s, gif_s, mprev, state_shortconv, state_mlstm_C[0], n_hb, w_up[0], w_down[0], w_out[0]),
        n_batch=n_batch, tm=MIXER_TM, n_valid=MIXER_TM)

    x_sm = jnp.concatenate([xs, meta_tokens], axis=0)
    mix_sm = jnp.concatenate([mix_s, mix_m[:N_META].astype(F32)], axis=0)
    y_s, fca_s, fcb_s, fca_m, fcb_m = _ffn_small(
        x_sm, mix_sm, wo, g2, wu, wfc, wd, gf, state_ffnconv, tf=FFN_TF, n_step=n_dec)
    y_p, fc_p = _ffn_seq(xp, mix_p, wo, g2, wu, wfc, wd, gf,
                         _tail_to_halo(_join_tail_blocks(fca_m[None])[0]),
                         _tail_to_halo(_join_tail_blocks(fcb_m[None])[0]),
                         n_batch=n_batch, tm=FFN_TM, tf=FFN_TF, n_valid=FFN_TM)

    return (y_p.reshape(n_batch, seq, D_MODEL),
            y_s.reshape(n_dec, 1, D_MODEL),
            c_p[None],
            n_p.reshape(1, n_batch, N_HEADS, DK),
            m_p[:, :, 0, 0][None],
            sc_p[None],
            _join_tail_blocks(fc_p)[None],
            c_s[None],
            n_s.transpose(1, 0, 2)[None],
            m_s[:, :, 0].T[None],
            sc_s,
            jnp.concatenate([fca_s, fcb_s], axis=-1)[None])
```
